```python
import math
import jax, jax.numpy as jnp
from jax import lax
import numpy as np

D_MODEL = 2048
BATCH = 4
SEQ = 2048
DEPTH = 2

GRID_W = 64
NA_HEADS = 16
NA_HEAD_DIM = D_MODEL // NA_HEADS
NA_WIN_ROWS_MAX = 8
NA_WIN_COLS = 16
ML_PROJ_FACTOR = 2
ML_INNER = ML_PROJ_FACTOR * D_MODEL
ML_HEADS = 8
ML_HEAD_DIM = ML_INNER // ML_HEADS
ML_QKV_BLOCK = 4
ML_N_BLOCKS = ML_INNER // ML_QKV_BLOCK
ML_CONV_K = 5
ML_CHUNK = 128
FFN_DIM = 5632
N_EXPERTS = 8
TOP_K = 2
EXPERT_DIM = 7168
N_EVEN = (DEPTH + 1) // 2
N_ODD = DEPTH // 2
RMS_EPS = 1e-6
LN_EPS = 1e-5

kernel_name = "hybrid_natten_mlstm_moe_encoder"


def rms_norm(x, g):
    xf = x.astype(jnp.float32)
    y = xf * lax.rsqrt(jnp.mean(xf * xf, axis=-1, keepdims=True) + RMS_EPS)
    return (y * g.astype(jnp.float32)).astype(x.dtype)


def neighborhood_attention(h, w_in, q_g, k_g, rpb, w_out):
    B, T, D = h.shape
    rows = T // GRID_W
    kh = min(NA_WIN_ROWS_MAX, rows)
    q, k, v = jnp.split(h @ w_in, 3, axis=-1)

    def to_grid(t):
        return t.reshape(B, rows, GRID_W, NA_HEADS, NA_HEAD_DIM)

    q = rms_norm(to_grid(q), q_g) * (NA_HEAD_DIM ** -0.5)
    k = rms_norm(to_grid(k), k_g)
    v = to_grid(v)
    col = jnp.arange(GRID_W)
    col_start = jnp.clip(col - NA_WIN_COLS // 2, 0, GRID_W - NA_WIN_COLS)
    col_in = (col[None, :] >= col_start[:, None]) & (col[None, :] < col_start[:, None] + NA_WIN_COLS)
    dj_idx = jnp.clip(col[None, :] - col[:, None] + NA_WIN_COLS - 1, 0, 2 * NA_WIN_COLS - 2)
    rpb_cols = rpb[:, :, dj_idx].astype(jnp.float32)

    def row_block(r):
        r0 = jnp.clip(r - kh // 2, 0, rows - kh)
        k_band = lax.dynamic_slice_in_dim(k, r0, kh, axis=1)
        v_band = lax.dynamic_slice_in_dim(v, r0, kh, axis=1)
        q_r = lax.dynamic_index_in_dim(q, r, axis=1, keepdims=False)
        s = jnp.einsum('bqhd,bikhd->bhqik', q_r, k_band).astype(jnp.float32)
        di_idx = r0 + jnp.arange(kh) - r + NA_WIN_ROWS_MAX - 1
        bias = jnp.transpose(rpb_cols[:, di_idx], (0, 2, 1, 3))
        s = jnp.where(col_in[None, None, :, None, :], s + bias[None], -jnp.inf)
        p = jax.nn.softmax(s, axis=(-2, -1)).astype(v.dtype)
        return jnp.einsum('bhqik,bikhd->bqhd', p, v_band)

    out = lax.map(row_block, jnp.arange(rows))
    out = jnp.transpose(out, (1, 0, 2, 3, 4)).reshape(B, T, D)
    return out @ w_out


def mlstm_chunkwise(q, k, v, log_i, log_f):
    B, H, T, dq = q.shape
    dv = v.shape[-1]
    L = ML_CHUNK
    n = T // L

    def chunks(t):
        return jnp.moveaxis(t.reshape((B, H, n, L) + t.shape[3:]), 2, 0)

    causal = jnp.tril(jnp.ones((L, L), dtype=bool))

    def step(carry, inp):
        C, nv, m = carry
        qc, kc, vc, li, lf = inp
        b = jnp.cumsum(lf, axis=-1)
        g = b[..., -1]
        dlog = jnp.where(causal, b[..., :, None] - b[..., None, :] + li[..., None, :], -jnp.inf)
        m_inter = b + m[..., None]
        m_t = jnp.maximum(m_inter, jnp.max(dlog, axis=-1))
        s = jnp.einsum('bhtd,bhsd->bhts', qc, kc) * jnp.exp(dlog - m_t[..., None])
        inter = jnp.exp(m_inter - m_t)
        num = jnp.einsum('bhts,bhsv->bhtv', s, vc) + jnp.einsum('bhtd,bhvd->bhtv', qc, C) * inter[..., None]
        den = jnp.sum(s, axis=-1) + jnp.einsum('bhtd,bhd->bht', qc, nv) * inter
        hc = num / jnp.maximum(jnp.abs(den), jnp.exp(-m_t))[..., None]
        a = g[..., None] - b + li
        m_new = jnp.maximum(g + m, jnp.max(a, axis=-1))
        w = jnp.exp(a - m_new[..., None])
        decay = jnp.exp(g + m - m_new)
        C = decay[..., None, None] * C + jnp.einsum('bhsv,bhsd->bhvd', w[..., None] * vc, kc)
        nv = decay[..., None] * nv + jnp.einsum('bhs,bhsd->bhd', w, kc)
        return (C, nv, m_new), hc

    init = (jnp.zeros((B, H, dv, dq), jnp.float32), jnp.zeros((B, H, dq), jnp.float32),
            jnp.zeros((B, H), jnp.float32))
    _, hs = lax.scan(step, init, (chunks(q), chunks(k), chunks(v), chunks(log_i), chunks(log_f)))
    return jnp.moveaxis(hs, 0, 2).reshape(B, H, T, dv)


def mlstm_mixer(h, w_in, conv_w, conv_b, wq, wk, wv, w_if, b_if, skip, norm_g, w_out):
    B, T, _ = h.shape
    f32 = jnp.float32
    xm, z = jnp.split(h @ w_in, 2, axis=-1)
    pad = ML_CONV_K // 2
    xc = lax.conv_general_dilated(xm, conv_w[:, None, :], window_strides=(1,), padding=[(pad, pad)],
                                  dimension_numbers=('NWC', 'WIO', 'NWC'), feature_group_count=ML_INNER)
    xc = jax.nn.silu(xc + conv_b)

    def blockdiag(t, w):
        return jnp.einsum('btni,nio->btno', t.reshape(B, T, ML_N_BLOCKS, ML_QKV_BLOCK), w).reshape(B, T, ML_INNER)

    q = blockdiag(xc, wq)
    k = blockdiag(xc, wk)
    v = blockdiag(xm, wv)
    pre = (jnp.einsum('btc,cdgh->btdgh', jnp.concatenate([q, k, v], axis=-1), w_if) + b_if).astype(f32)
    log_i = jnp.transpose(pre[:, :, :, 0, :], (2, 0, 3, 1))
    log_f = jnp.transpose(jax.nn.log_sigmoid(pre[:, :, :, 1, :]), (2, 0, 3, 1))

    def heads(t):
        return jnp.transpose(t.reshape(B, T, ML_HEADS, ML_HEAD_DIM), (0, 2, 1, 3)).astype(f32)

    qh = heads(q)
    kh = heads(k) * (ML_HEAD_DIM ** -0.5)
    vh = heads(v)
    flip = lambda t: jnp.flip(t, axis=2)
    h_fwd = mlstm_chunkwise(qh, kh, vh, log_i[0], log_f[0])
    h_bwd = flip(mlstm_chunkwise(flip(qh), flip(kh), flip(vh), flip(log_i[1]), flip(log_f[1])))
    ht = h_fwd + h_bwd
    mu = jnp.mean(ht, axis=-1, keepdims=True)
    var = jnp.mean(jnp.square(ht - mu), axis=-1, keepdims=True)
    hn = (ht - mu) * lax.rsqrt(var + LN_EPS)
    hn = jnp.transpose(hn, (0, 2, 1, 3)).reshape(B, T, ML_INNER) * norm_g.astype(f32)
    out = (hn.astype(h.dtype) + skip * xc) * jax.nn.sigmoid(z)
    return out @ w_out


def swiglu(h, w_gate, w_up, w_down):
    return (jax.nn.silu(h @ w_gate) * (h @ w_up)) @ w_down


def moe_swiglu(h, router, w_gate, w_up, w_down):
    B, T, D = h.shape
    xt = h.reshape(B * T, D)
    logits = (xt @ router).astype(jnp.float32)
    top_v, top_i = lax.top_k(logits, TOP_K)
    top_w = jax.nn.softmax(top_v, axis=-1)
    gates = jnp.sum(jax.nn.one_hot(top_i, N_EXPERTS, dtype=jnp.float32) * top_w[..., None], axis=1)
    gates = gates.astype(xt.dtype)
    out = jnp.zeros_like(xt)
    for e in range(N_EXPERTS):
        out = out + gates[:, e:e + 1] * swiglu(xt, w_gate[e], w_up[e], w_down[e])
    return out.reshape(B, T, D)


def setup_inputs(seed: int = 0) -> dict:
    key = jax.random.key(seed)
    ks = iter(jax.random.split(key, 40))
    D = D_MODEL

    def nrm(shape, scale):
        return jax.random.normal(next(ks), shape, jnp.float32) * scale

    x = nrm((BATCH, SEQ, D), 1.0)
    c = nrm((BATCH, D), 1.0)
    mod_w = nrm((DEPTH, D, 6 * D), D ** -0.5)
    mod_b = nrm((DEPTH, 6 * D), 0.02)
    norm_mix_g = 1.0 + nrm((DEPTH, D), 0.02)
    norm_ffn_g = 1.0 + nrm((DEPTH, D), 0.02)
    na_w_in = nrm((N_EVEN, D, 3 * D), D ** -0.5)
    na_q_norm_g = 1.0 + nrm((N_EVEN, NA_HEAD_DIM), 0.02)
    na_k_norm_g = 1.0 + nrm((N_EVEN, NA_HEAD_DIM), 0.02)
    na_rpb = nrm((N_EVEN, NA_HEADS, 2 * NA_WIN_ROWS_MAX - 1, 2 * NA_WIN_COLS - 1), 0.1)
    na_w_out = nrm((N_EVEN, D, D), D ** -0.5)
    ffn_w_gate = nrm((N_EVEN, D, FFN_DIM), D ** -0.5)
    ffn_w_up = nrm((N_EVEN, D, FFN_DIM), D ** -0.5)
    ffn_w_down = nrm((N_EVEN, FFN_DIM, D), FFN_DIM ** -0.5)
    ml_w_in = nrm((N_ODD, D, 2 * ML_INNER), D ** -0.5)
    ml_conv_w = nrm((N_ODD, ML_CONV_K, ML_INNER), ML_CONV_K ** -0.5)
    ml_conv_b = nrm((N_ODD, ML_INNER), 0.02)
    ml_wq = nrm((N_ODD, ML_N_BLOCKS, ML_QKV_BLOCK, ML_QKV_BLOCK), ML_QKV_BLOCK ** -0.5)
    ml_wk = nrm((N_ODD, ML_N_BLOCKS, ML_QKV_BLOCK, ML_QKV_BLOCK), ML_QKV_BLOCK ** -0.5)
    ml_wv = nrm((N_ODD, ML_N_BLOCKS, ML_QKV_BLOCK, ML_QKV_BLOCK), ML_QKV_BLOCK ** -0.5)
    ml_w_if = nrm((N_ODD, 3 * ML_INNER, 2, 2, ML_HEADS), 0.1 * (3 * ML_INNER) ** -0.5)
    b_i = nrm((N_ODD, 2, 1, ML_HEADS), 0.1)
    b_f = 3.0 + 3.0 * jax.random.uniform(next(ks), (N_ODD, 2, 1, ML_HEADS), jnp.float32)
    ml_b_if = jnp.concatenate([b_i, b_f], axis=2)
    ml_skip = 1.0 + nrm((N_ODD, ML_INNER), 0.02)
    ml_norm_g = 1.0 + nrm((N_ODD, ML_INNER), 0.02)
    ml_w_out = nrm((N_ODD, ML_INNER, D), ML_INNER ** -0.5)
    moe_router = nrm((N_ODD, D, N_EXPERTS), D ** -0.5)
    moe_w_gate = nrm((N_ODD, N_EXPERTS, D, EXPERT_DIM), D ** -0.5)
    moe_w_up = nrm((N_ODD, N_EXPERTS, D, EXPERT_DIM), D ** -0.5)
    moe_w_down = nrm((N_ODD, N_EXPERTS, EXPERT_DIM, D), EXPERT_DIM ** -0.5)
    return {"x": x, "c": c, "mod_w": mod_w, "mod_b": mod_b, "norm_mix_g": norm_mix_g,
            "norm_ffn_g": norm_ffn_g, "na_w_in": na_w_in, "na_q_norm_g": na_q_norm_g,
            "na_k_norm_g": na_k_norm_g, "na_rpb": na_rpb, "na_w_out": na_w_out,
            "ffn_w_gate": ffn_w_gate, "ffn_w_up": ffn_w_up, "ffn_w_down": ffn_w_down,
            "ml_w_in": ml_w_in, "ml_conv_w": ml_conv_w, "ml_conv_b": ml_conv_b,
            "ml_wq": ml_wq, "ml_wk": ml_wk, "ml_wv": ml_wv, "ml_w_if": ml_w_if,
            "ml_b_if": ml_b_if, "ml_skip": ml_skip, "ml_norm_g": ml_norm_g, "ml_w_out": ml_w_out,
            "moe_router": moe_router, "moe_w_gate": moe_w_gate, "moe_w_up": moe_w_up,
            "moe_w_down": moe_w_down}


def reference(x, c, mod_w, mod_b, norm_mix_g, norm_ffn_g, na_w_in, na_q_norm_g, na_k_norm_g,
              na_rpb, na_w_out, ffn_w_gate, ffn_w_up, ffn_w_down, ml_w_in, ml_conv_w, ml_conv_b,
              ml_wq, ml_wk, ml_wv, ml_w_if, ml_b_if, ml_skip, ml_norm_g, ml_w_out,
              moe_router, moe_w_gate, moe_w_up, moe_w_down):
    c_act = jax.nn.silu(c)
    for i in range(DEPTH):
        j = i // 2
        mod = (c_act @ mod_w[i] + mod_b[i])[:, None, :]
        shift1, scale1, gate1, shift2, scale2, gate2 = jnp.split(mod, 6, axis=-1)
        h = rms_norm(x, norm_mix_g[i]) * (1 + scale1) + shift1
        if i % 2 == 0:
            y = neighborhood_attention(h, na_w_in[j], na_q_norm_g[j], na_k_norm_g[j], na_rpb[j], na_w_out[j])
        else:
            y = mlstm_mixer(h, ml_w_in[j], ml_conv_w[j], ml_conv_b[j], ml_wq[j], ml_wk[j], ml_wv[j],
                            ml_w_if[j], ml_b_if[j], ml_skip[j], ml_norm_g[j], ml_w_out[j])
        x = x + gate1 * y
        h = rms_norm(x, norm_ffn_g[i]) * (1 + scale2) + shift2
        if i % 2 == 0:
            y = swiglu(h, ffn_w_gate[j], ffn_w_up[j], ffn_w_down[j])
        else:
            y = moe_swiglu(h, moe_router[j], moe_w_gate[j], moe_w_up[j], moe_w_down[j])
        x = x + gate2 * y
    return x
```

```python
import functools

import jax
import jax.numpy as jnp
from jax import lax
from jax.experimental import pallas as pl
from jax.experimental.pallas import tpu as pltpu

F32 = jnp.float32
BF16 = jnp.bfloat16

GRID_W = 64
NA_HEADS = 16
NA_WIN_ROWS_MAX = 8
NA_WIN_COLS = 16
ML_HEADS = 8
ML_QKV_BLOCK = 4
ML_CONV_K = 5
N_EXPERTS = 8
TOP_K = 2
RMS_EPS = 1e-6
LN_EPS = 1e-5

V7X_VMEM_LIMIT_BYTES = 56 * 1024 * 1024
LANES = 128
BF16_SUBLANES = 16

MM_TM = 1024
ML_CHUNK = 128
FFN_SUB = 256
FFN_ROWS = 2048
FFN_TF = 256
MASK_NEG = -1e30


def _params(sem):
    return pltpu.CompilerParams(dimension_semantics=sem, vmem_limit_bytes=V7X_VMEM_LIMIT_BYTES)


def _mm_body(*refs, a_silu, has_bias, has_res):
    a_ref, w_ref = refs[0], refs[1]
    k = 2
    if has_bias:
        b_ref = refs[k]
        k += 1
    if has_res:
        r_ref, g_ref = refs[k], refs[k + 1]
        k += 2
    o_ref, wb_ref = refs[k], refs[k + 1]

    @pl.when(pl.program_id(1) == 0)
    def _():
        wb_ref[...] = w_ref[...].astype(BF16)

    a = a_ref[...]
    if a_silu:
        a = a * jax.nn.sigmoid(a)
    y = jnp.dot(a.astype(BF16), wb_ref[...], preferred_element_type=F32)
    if has_bias:
        y = y + b_ref[...]
    if has_res:
        y = r_ref[...] + g_ref[...] * y
    o_ref[...] = y.astype(o_ref.dtype)


def _matmul(a, w, layer, *, tn, tm, out_dtype, a_silu=False, bias=None, res=None, gate=None,
            gate_idx=0, rows_per_batch=None, name="matmul"):
    M, K = a.shape
    N = w.shape[2]
    grid = (N // tn, M // tm)
    in_specs = [pl.BlockSpec((tm, K), lambda j, i: (i, 0)),
                pl.BlockSpec((None, K, tn), lambda j, i: (layer, 0, j))]
    args = [a, w]
    if bias is not None:
        in_specs.append(pl.BlockSpec((None, 1, tn), lambda j, i: (layer, 0, j)))
        args.append(bias.reshape(bias.shape[0], 1, N))
    if res is not None:
        tiles_per_batch = rows_per_batch // tm
        in_specs.append(pl.BlockSpec((tm, tn), lambda j, i: (i, j)))
        in_specs.append(pl.BlockSpec((None, None, 1, tn),
                                     lambda j, i: (i // tiles_per_batch, gate_idx, 0, j)))
        args += [res, gate]
    body = functools.partial(_mm_body, a_silu=a_silu, has_bias=bias is not None,
                             has_res=res is not None)
    return pl.pallas_call(
        body,
        out_shape=jax.ShapeDtypeStruct((M, N), out_dtype),
        grid=grid,
        in_specs=in_specs,
        out_specs=pl.BlockSpec((tm, tn), lambda j, i: (i, j)),
        scratch_shapes=[pltpu.VMEM((K, tn), BF16)],
        compiler_params=_params(("parallel", "arbitrary")),
        name=name,
    )(*args)


def _norm_mod(x, g, sc, sh):
    y = x * lax.rsqrt(jnp.mean(x * x, axis=-1, keepdims=True) + RMS_EPS) * g
    return y * (1.0 + sc) + sh


def _normmod_body(x_ref, g_ref, sc_ref, sh_ref, o_ref):
    o_ref[...] = _norm_mod(x_ref[...], g_ref[...], sc_ref[...], sh_ref[...]).astype(o_ref.dtype)


def _mod_spec(k, D):
    return pl.BlockSpec((None, None, 1, D), lambda b, t: (b, k, 0, 0))


def _normmod(x, g, layer, mod, k_scale, k_shift, *, tt, out_dtype, name):
    B, T, D = x.shape
    return pl.pallas_call(
        _normmod_body,
        out_shape=jax.ShapeDtypeStruct((B, T, D), out_dtype),
        grid=(B, T // tt),
        in_specs=[pl.BlockSpec((None, tt, D), lambda b, t: (b, t, 0)),
                  pl.BlockSpec((None, 1, D), lambda b, t: (layer, 0, 0)),
                  _mod_spec(k_scale, D), _mod_spec(k_shift, D)],
        out_specs=pl.BlockSpec((None, tt, D), lambda b, t: (b, t, 0)),
        compiler_params=_params(("parallel", "parallel")),
        name=name,
    )(x, g, mod, mod)


def _na_body(q_ref, k_ref, v_ref, qg_ref, kg_ref, bias_ref, o_ref, qs, ks, *, rows, width, kh, dh):
    def nrm(t, g):
        tf = t.astype(F32)
        return tf * lax.rsqrt(jnp.mean(tf * tf, axis=-1, keepdims=True) + RMS_EPS) * g

    qs[...] = (nrm(q_ref[...], qg_ref[...]) * (dh ** -0.5)).astype(BF16)
    ks[...] = nrm(k_ref[...], kg_ref[...]).astype(BF16)

    def row_block(r, carry):
        r0 = jnp.clip(r - kh // 2, 0, rows - kh)
        q_off = pl.multiple_of(r * width, width)
        b_off = pl.multiple_of(r0 * width, width)
        q_r = qs[pl.ds(q_off, width), :]
        k_band = ks[pl.ds(b_off, kh * width), :]
        v_band = v_ref[pl.ds(b_off, kh * width), :]
        s = lax.dot_general(q_r, k_band, (((1,), (1,)), ((), ())), preferred_element_type=F32)
        s = s + bias_ref[r - r0]
        e = jnp.exp(s - jnp.max(s, axis=-1, keepdims=True))
        l = jnp.sum(e, axis=-1, keepdims=True)
        o = jnp.dot(e.astype(BF16), v_band, preferred_element_type=F32) / l
        o_ref[pl.ds(q_off, width), :] = o.astype(o_ref.dtype)
        return carry

    lax.fori_loop(0, rows, row_block, 0)


def _na_bias_table(rpb, kh):
    col = jnp.arange(GRID_W)
    col_start = jnp.clip(col - NA_WIN_COLS // 2, 0, GRID_W - NA_WIN_COLS)
    col_in = (col[None, :] >= col_start[:, None]) & (col[None, :] < col_start[:, None] + NA_WIN_COLS)
    dj_idx = jnp.clip(col[None, :] - col[:, None] + NA_WIN_COLS - 1, 0, 2 * NA_WIN_COLS - 2)
    rpb_cols = jnp.where(col_in[None, None], rpb[:, :, dj_idx].astype(F32), MASK_NEG)
    off = jnp.arange(kh)[:, None]
    band = jnp.arange(kh)[None, :]
    di = band - off + NA_WIN_ROWS_MAX - 1
    tab = rpb_cols[:, di]
    H = rpb.shape[0]
    return jnp.transpose(tab, (0, 1, 3, 2, 4)).reshape(H, kh, GRID_W, kh * GRID_W)


def _neighborhood_attention(qkv, q_g, k_g, rpb):
    B, T, D3 = qkv.shape
    D = D3 // 3
    H = NA_HEADS
    dh = D // H
    rows = T // GRID_W
    kh = min(NA_WIN_ROWS_MAX, rows)
    bias = _na_bias_table(rpb, kh)
    body = functools.partial(_na_body, rows=rows, width=GRID_W, kh=kh, dh=dh)
    return pl.pallas_call(
        body,
        out_shape=jax.ShapeDtypeStruct((B, T, D), BF16),
        grid=(B, H),
        in_specs=[pl.BlockSpec((None, T, dh), lambda b, h: (b, 0, h)),
                  pl.BlockSpec((None, T, dh), lambda b, h: (b, 0, H + h)),
                  pl.BlockSpec((None, T, dh), lambda b, h: (b, 0, 2 * H + h)),
                  pl.BlockSpec((1, dh), lambda b, h: (0, 0)),
                  pl.BlockSpec((1, dh), lambda b, h: (0, 0)),
                  pl.BlockSpec((None, kh, GRID_W, kh * GRID_W), lambda b, h: (h, 0, 0, 0))],
        out_specs=pl.BlockSpec((None, T, dh), lambda b, h: (b, 0, h)),
        scratch_shapes=[pltpu.VMEM((T, dh), BF16), pltpu.VMEM((T, dh), BF16)],
        compiler_params=_params(("parallel", "parallel")),
        name="neighborhood_attention",
    )(qkv, qkv, qkv, q_g.reshape(1, dh), k_g.reshape(1, dh), bias)


def _mlstm_pre_body(xm_ref, xp_ref, xn_ref, cw_ref, cb_ref, wq_ref, wk_ref, wv_ref, wif_ref, bif_ref,
                    q_ref, k_ref, v_ref, xc_ref, pre_ref, *, tt, inner, halo):
    t = pl.program_id(1)
    first = (t > 0).astype(F32)
    last = (t < pl.num_programs(1) - 1).astype(F32)
    pad = ML_CONV_K // 2
    n = tt + 2 * halo
    pre = jnp.zeros((tt, LANES), F32)
    for c in range(inner // LANES):
        sl = slice(c * LANES, (c + 1) * LANES)
        cur_b = xm_ref[:, sl]
        cat = jnp.concatenate([xp_ref[:, sl].astype(F32) * first, cur_b.astype(F32),
                               xn_ref[:, sl].astype(F32) * last], axis=0)
        xc = jnp.zeros((tt, LANES), F32) + cb_ref[:, sl]
        for j in range(ML_CONV_K):
            sh = pltpu.roll(cat, (pad - j) % n, axis=0) if j != pad else cat
            xc = xc + sh[halo:halo + tt] * cw_ref[j:j + 1, sl]
        xc = xc * jax.nn.sigmoid(xc)
        xc_b = xc.astype(BF16)
        q = jnp.dot(xc_b, wq_ref[c], preferred_element_type=F32).astype(BF16)
        k = jnp.dot(xc_b, wk_ref[c], preferred_element_type=F32).astype(BF16)
        v = jnp.dot(cur_b, wv_ref[c], preferred_element_type=F32).astype(BF16)
        pre = pre + jnp.dot(q, wif_ref[0, sl, :], preferred_element_type=F32)
        pre = pre + jnp.dot(k, wif_ref[1, sl, :], preferred_element_type=F32)
        pre = pre + jnp.dot(v, wif_ref[2, sl, :], preferred_element_type=F32)
        q_ref[:, sl] = q
        k_ref[:, sl] = k
        v_ref[:, sl] = v
        xc_ref[:, sl] = xc_b
    pre_ref[...] = pre + bif_ref[...]


def _block_diag_dense(w):
    nb = w.shape[0]
    per = LANES // ML_QKV_BLOCK
    wr = w.reshape(nb // per, per, ML_QKV_BLOCK, ML_QKV_BLOCK)
    eye = jnp.eye(per, dtype=w.dtype)
    dense = jnp.einsum('gnio,nm->gnimo', wr, eye)
    return dense.reshape(nb // per, LANES, LANES).astype(BF16)


def _mlstm_pre(xz, conv_w, conv_b, wq, wk, wv, w_if, b_if, *, tt):
    B, T, inner2 = xz.shape
    inner = inner2 // 2
    halo = BF16_SUBLANES
    n_gate = w_if.shape[1] * w_if.shape[2] * w_if.shape[3]
    wif = jnp.pad(w_if.reshape(3, inner, n_gate), ((0, 0), (0, 0), (0, LANES - n_gate))).astype(BF16)
    bif = jnp.pad(b_if.reshape(1, n_gate), ((0, 0), (0, LANES - n_gate)))
    nblk = inner // LANES
    hb = tt // halo
    nh = T // halo
    body = functools.partial(_mlstm_pre_body, tt=tt, inner=inner, halo=halo)
    act = jax.ShapeDtypeStruct((B, T, inner), BF16)
    act_spec = pl.BlockSpec((None, tt, inner), lambda b, t: (b, t, 0))
    const3 = lambda b, t: (0, 0, 0)
    return pl.pallas_call(
        body,
        out_shape=(act, act, act, act, jax.ShapeDtypeStruct((B, T, LANES), F32)),
        grid=(B, T // tt),
        in_specs=[act_spec,
                  pl.BlockSpec((None, halo, inner), lambda b, t: (b, jnp.maximum(t * hb - 1, 0), 0)),
                  pl.BlockSpec((None, halo, inner), lambda b, t: (b, jnp.minimum((t + 1) * hb, nh - 1), 0)),
                  pl.BlockSpec((ML_CONV_K, inner), lambda b, t: (0, 0)),
                  pl.BlockSpec((1, inner), lambda b, t: (0, 0)),
                  pl.BlockSpec((nblk, LANES, LANES), const3),
                  pl.BlockSpec((nblk, LANES, LANES), const3),
                  pl.BlockSpec((nblk, LANES, LANES), const3),
                  pl.BlockSpec((3, inner, LANES), const3),
                  pl.BlockSpec((1, LANES), lambda b, t: (0, 0))],
        out_specs=(act_spec, act_spec, act_spec, act_spec,
                   pl.BlockSpec((None, tt, LANES), lambda b, t: (b, t, 0))),
        compiler_params=_params(("parallel", "parallel")),
        name="mlstm_pre",
    )(xz, xz, xz, conv_w, conv_b.reshape(1, inner), _block_diag_dense(wq), _block_diag_dense(wk),
      _block_diag_dense(wv), wif, bif)


def _log_sigmoid(x):
    return jnp.minimum(x, 0.0) - jnp.log(1.0 + jnp.exp(-jnp.abs(x)))


def _mlstm_body(q_ref, k_ref, v_ref, li_r_ref, lf_r_ref, li_c_ref, lf_c_ref, o_ref, S, nv, m_ref, *, L, dq):
    d = pl.program_id(0)
    c = pl.program_id(3)

    @pl.when(c == 0)
    def _():
        S[...] = jnp.zeros_like(S)
        nv[...] = jnp.zeros_like(nv)
        m_ref[...] = jnp.zeros_like(m_ref)

    scale = dq ** -0.5
    q = q_ref[...]
    k = k_ref[...]
    v = v_ref[...]
    li_r = li_r_ref[...]
    li_c = li_c_ref[...]
    lf_r = _log_sigmoid(lf_r_ref[...])
    lf_c = _log_sigmoid(lf_c_ref[...])
    row = lax.broadcasted_iota(jnp.int32, (L, L), 0)
    col = lax.broadcasted_iota(jnp.int32, (L, L), 1)
    sign = 1 - 2 * d
    tri = (row - col) * sign >= 0
    b_c = jnp.sum(jnp.where(tri, lf_r, 0.0), axis=1, keepdims=True)
    b_r = jnp.sum(jnp.where((col - row) * sign >= 0, lf_c, 0.0), axis=0, keepdims=True)
    g = jnp.sum(lf_r, axis=1, keepdims=True)
    m = m_ref[...]

    dlog = jnp.where(tri, b_c - b_r + li_r, -jnp.inf)
    m_inter = b_c + m
    m_t = jnp.maximum(m_inter, jnp.max(dlog, axis=1, keepdims=True))
    qk = lax.dot_general(q, k, (((1,), (1,)), ((), ())), preferred_element_type=F32)
    s = qk * (jnp.exp(dlog - m_t) * scale)
    inter = jnp.exp(m_inter - m_t)
    num = jnp.dot(s.astype(BF16), v, preferred_element_type=F32)
    num = num + jnp.dot(q, S[...].astype(BF16), preferred_element_type=F32) * inter
    den = jnp.sum(s, axis=1, keepdims=True) + jnp.sum(q.astype(F32) * nv[...], axis=1, keepdims=True) * inter
    o_ref[...] = (num / jnp.maximum(jnp.abs(den), jnp.exp(-m_t))).astype(o_ref.dtype)

    a = g - b_c + li_c
    m_new = jnp.maximum(g + m, jnp.max(a, axis=0, keepdims=True))
    w = jnp.exp(a - m_new) * scale
    decay = jnp.exp(g + m - m_new)
    kw = k.astype(F32) * w
    S[...] = decay * S[...] + lax.dot_general(kw.astype(BF16), v, (((0,), (0,)), ((), ())),
                                              preferred_element_type=F32)
    nv[...] = decay * nv[...] + jnp.sum(kw, axis=0, keepdims=True)
    m_ref[...] = m_new


def _mlstm(q, k, v, pre, *, L):
    B, T, inner = q.shape
    H = ML_HEADS
    dh = inner // H
    n = T // L
    gates = jnp.transpose(pre[:, :, :4 * H].reshape(B, T, 2, 2, H), (3, 2, 0, 4, 1))
    li, lf = gates[0], gates[1]
    rows = lambda a: a.reshape(2, B, H, n, 1, L)
    cols = lambda a: a.reshape(2, B, H, n, L, 1)

    def cidx(d, c):
        return c + d * (n - 1 - 2 * c)

    qkv_spec = pl.BlockSpec((None, L, dh), lambda d, b, h, c: (b, cidx(d, c), h))
    row_spec = pl.BlockSpec((None, None, None, None, 1, L), lambda d, b, h, c: (d, b, h, cidx(d, c), 0, 0))
    col_spec = pl.BlockSpec((None, None, None, None, L, 1), lambda d, b, h, c: (d, b, h, cidx(d, c), 0, 0))
    body = functools.partial(_mlstm_body, L=L, dq=dh)
    return pl.pallas_call(
        body,
        out_shape=jax.ShapeDtypeStruct((2, B, T, inner), F32),
        grid=(2, B, H, n),
        in_specs=[qkv_spec, qkv_spec, qkv_spec, row_spec, row_spec, col_spec, col_spec],
        out_specs=pl.BlockSpec((None, None, L, dh), lambda d, b, h, c: (d, b, cidx(d, c), h)),
        scratch_shapes=[pltpu.VMEM((dh, dh), F32), pltpu.VMEM((1, dh), F32), pltpu.VMEM((1, 1), F32)],
        compiler_params=_params(("parallel", "parallel", "parallel", "arbitrary")),
        name="mlstm_chunkwise",
    )(q, k, v, rows(li), rows(lf), cols(li), cols(lf))


def _mlstm_post_body(hf_ref, hb_ref, xc_ref, z_ref, ng_ref, skip_ref, o_ref, *, dh, inner):
    for h in range(inner // dh):
        sl = slice(h * dh, (h + 1) * dh)
        ht = hf_ref[:, sl] + hb_ref[:, sl]
        mu = jnp.mean(ht, axis=-1, keepdims=True)
        var = jnp.mean(jnp.square(ht - mu), axis=-1, keepdims=True)
        hn = (ht - mu) * lax.rsqrt(var + LN_EPS) * ng_ref[:, sl]
        out = (hn + skip_ref[:, sl] * xc_ref[:, sl].astype(F32)) * jax.nn.sigmoid(z_ref[:, sl].astype(F32))
        o_ref[:, sl] = out.astype(o_ref.dtype)


def _mlstm_post(hd, xc, xz, norm_g, skip, *, tt):
    _, B, T, inner = hd.shape
    dh = inner // ML_HEADS
    body = functools.partial(_mlstm_post_body, dh=dh, inner=inner)
    vec = pl.BlockSpec((1, inner), lambda b, t: (0, 0))
    return pl.pallas_call(
        body,
        out_shape=jax.ShapeDtypeStruct((B, T, inner), BF16),
        grid=(B, T // tt),
        in_specs=[pl.BlockSpec((None, None, tt, inner), lambda b, t: (0, b, t, 0)),
                  pl.BlockSpec((None, None, tt, inner), lambda b, t: (1, b, t, 0)),
                  pl.BlockSpec((None, tt, inner), lambda b, t: (b, t, 0)),
                  pl.BlockSpec((None, tt, inner), lambda b, t: (b, t, 1)),
                  vec, vec],
        out_specs=pl.BlockSpec((None, tt, inner), lambda b, t: (b, t, 0)),
        compiler_params=_params(("parallel", "parallel")),
        name="mlstm_post",
    )(hd, hd, xc, xz, norm_g.reshape(1, inner), skip.reshape(1, inner))


def _router_body(x_ref, g_ref, sc_ref, sh_ref, r_ref, h_ref, idx_ref, wt_ref, *, n_experts):
    h = _norm_mod(x_ref[...], g_ref[...], sc_ref[...], sh_ref[...])
    h_ref[...] = h
    logits = jnp.dot(h, r_ref[...], precision=lax.Precision.HIGHEST, preferred_element_type=F32)
    lane = lax.broadcasted_iota(jnp.int32, logits.shape, 1)
    lg = jnp.where(lane < n_experts, logits, -jnp.inf)
    m1 = jnp.max(lg, axis=1, keepdims=True)
    i1 = jnp.min(jnp.where(lg == m1, lane, LANES), axis=1, keepdims=True)
    lg2 = jnp.where(lane == i1, -jnp.inf, lg)
    m2 = jnp.max(lg2, axis=1, keepdims=True)
    i2 = jnp.min(jnp.where(lg2 == m2, lane, LANES), axis=1, keepdims=True)
    e2 = jnp.exp(m2 - m1)
    w1 = 1.0 / (1.0 + e2)
    w2 = e2 * w1
    idx_ref[...] = jnp.where(lane == 0, i1, jnp.where(lane == 1, i2, 0))
    wt_ref[...] = jnp.where(lane == 0, w1, jnp.where(lane == 1, w2, 0.0))


def _router(x, g, layer, mod, router, *, tt):
    B, T, D = x.shape
    E = router.shape[1]
    r_pad = jnp.pad(router, ((0, 0), (0, LANES - E)))
    body = functools.partial(_router_body, n_experts=E)
    small = pl.BlockSpec((None, tt, LANES), lambda b, t: (b, t, 0))
    return pl.pallas_call(
        body,
        out_shape=(jax.ShapeDtypeStruct((B, T, D), F32),
                   jax.ShapeDtypeStruct((B, T, LANES), jnp.int32),
                   jax.ShapeDtypeStruct((B, T, LANES), F32)),
        grid=(B, T // tt),
        in_specs=[pl.BlockSpec((None, tt, D), lambda b, t: (b, t, 0)),
                  pl.BlockSpec((None, 1, D), lambda b, t: (layer, 0, 0)),
                  _mod_spec(4, D), _mod_spec(3, D),
                  pl.BlockSpec((D, LANES), lambda b, t: (0, 0))],
        out_specs=(pl.BlockSpec((None, tt, D), lambda b, t: (b, t, 0)), small, small),
        compiler_params=_params(("parallel", "parallel")),
        name="moe_router",
    )(x, g, mod, mod, r_pad)


def _route_tables(top_i, n_experts, n_tiles, n_slots):
    n_pairs = top_i.size
    e_flat = top_i.reshape(n_pairs)
    onehot = (e_flat[:, None] == jnp.arange(n_experts, dtype=jnp.int32)[None, :]).astype(jnp.int32)
    csum = jnp.cumsum(onehot, axis=0)
    counts = csum[-1]
    rank = jnp.sum(csum * onehot, axis=1) - 1
    nsub = (counts + FFN_SUB - 1) // FFN_SUB
    goff = jnp.cumsum(nsub) - nsub
    pos = jnp.sum(onehot * goff[None, :], axis=1) * FFN_SUB + rank
    src = jnp.zeros((n_slots,), jnp.int32).at[pos].set(jnp.arange(n_pairs, dtype=jnp.int32) // TOP_K)

    spr = FFN_ROWS // FFN_SUB
    ntile = (nsub + spr - 1) // spr
    per_tile = (nsub + jnp.maximum(ntile, 1) - 1) // jnp.maximum(ntile, 1)
    tcum = jnp.cumsum(ntile)
    toff = tcum - ntile
    ids = jnp.arange(n_tiles, dtype=jnp.int32)
    te = jnp.minimum(jnp.sum((ids[:, None] >= tcum[None, :]).astype(jnp.int32), axis=1), n_experts - 1)
    valid = ids < tcum[-1]
    j = ids - toff[te]
    t_start = goff[te] + j * per_tile[te]
    t_nsub = jnp.minimum(per_tile[te], nsub[te] - j * per_tile[te])
    last_e = te[jnp.maximum(tcum[-1] - 1, 0)]
    tile_e = jnp.where(valid, te, last_e).astype(jnp.int32)
    tile_start = jnp.where(valid, t_start, 0).astype(jnp.int32)
    tile_nsub = jnp.where(valid, t_nsub, 0).astype(jnp.int32)
    n_used_sub = jnp.sum(nsub).astype(jnp.int32).reshape(1)
    return pos.astype(jnp.int32), src, tile_e, tile_start, tile_nsub, n_used_sub


def _dispatch_body(src_ref, h_ref, xs_ref, sem, *, chunk):
    base = pl.program_id(0) * chunk

    def row_copy(i):
        return pltpu.make_async_copy(h_ref.at[pl.ds(src_ref[base + i], 1)],
                                     xs_ref.at[pl.ds(base + i, 1)], sem)

    def issue(i, carry):
        row_copy(i).start()
        return carry

    def drain(i, carry):
        row_copy(i).wait()
        return carry

    lax.fori_loop(0, chunk, issue, 0)
    lax.fori_loop(0, chunk, drain, 0)


def _dispatch(src, h, n_slots, *, chunk):
    N, D = h.shape
    body = functools.partial(_dispatch_body, chunk=chunk)
    return pl.pallas_call(
        body,
        out_shape=jax.ShapeDtypeStruct((n_slots, D), F32),
        grid_spec=pltpu.PrefetchScalarGridSpec(
            num_scalar_prefetch=1,
            grid=(n_slots // chunk,),
            in_specs=[pl.BlockSpec(memory_space=pl.ANY)],
            out_specs=pl.BlockSpec(memory_space=pl.ANY),
            scratch_shapes=[pltpu.SemaphoreType.DMA]),
        compiler_params=_params(("arbitrary",)),
        name="moe_dispatch",
    )(src, h)


def _expert_body(te_ref, ts_ref, tn_ref, used_ref, xs_ref, wg_ref, wu_ref, wd_ref, y_ref,
                 xb, acc, stage, wgb, wub, wdb, sem_in, sem_out, *, nf):
    s = pl.program_id(0)
    f = pl.program_id(1)
    nsub = tn_ref[s]
    start = ts_ref[s]

    @pl.when((s == 0) & (f == 0))
    def _():
        def fill_copy(j):
            row = pl.multiple_of(j * FFN_SUB, FFN_SUB)
            return pltpu.make_async_copy(stage.at[0], y_ref.at[pl.ds(row, FFN_SUB)], sem_out)

        def fill_start(j, carry):
            fill_copy(j).start()
            return carry

        def fill_wait(j, carry):
            fill_copy(j).wait()
            return carry

        n_sub_total = y_ref.shape[0] // FFN_SUB
        stage[0] = jnp.zeros(stage.shape[1:], F32)
        lax.fori_loop(used_ref[0], n_sub_total, fill_start, 0)
        lax.fori_loop(used_ref[0], n_sub_total, fill_wait, 0)

    def in_copy(j, slot):
        row = pl.multiple_of((start + j) * FFN_SUB, FFN_SUB)
        return pltpu.make_async_copy(xs_ref.at[pl.ds(row, FFN_SUB)], stage.at[slot], sem_in.at[slot])

    def out_copy(j):
        row = pl.multiple_of((start + j) * FFN_SUB, FFN_SUB)
        loc = pl.multiple_of(j * FFN_SUB, FFN_SUB)
        return pltpu.make_async_copy(acc.at[pl.ds(loc, FFN_SUB)], y_ref.at[pl.ds(row, FFN_SUB)], sem_out)

    @pl.when(nsub > 0)
    def _():
        @pl.when(f == 0)
        def _():
            in_copy(0, 0).start()

            def load(j, carry):
                slot = j % 2

                @pl.when(j + 1 < nsub)
                def _():
                    in_copy(j + 1, 1 - slot).start()

                in_copy(j, slot).wait()
                loc = pl.multiple_of(j * FFN_SUB, FFN_SUB)
                xb[pl.ds(loc, FFN_SUB), :] = stage[slot].astype(BF16)
                acc[pl.ds(loc, FFN_SUB), :] = jnp.zeros((FFN_SUB, acc.shape[1]), F32)
                return carry

            lax.fori_loop(0, nsub, load, 0)

        wgb[...] = wg_ref[...].astype(BF16)
        wub[...] = wu_ref[...].astype(BF16)
        wdb[...] = wd_ref[...].astype(BF16)

        def compute(j, carry):
            loc = pl.multiple_of(j * FFN_SUB, FFN_SUB)
            x = xb[pl.ds(loc, FFN_SUB), :]
            hg = jnp.dot(x, wgb[...], preferred_element_type=F32)
            hu = jnp.dot(x, wub[...], preferred_element_type=F32)
            hid = (hg * jax.nn.sigmoid(hg) * hu).astype(BF16)
            acc[pl.ds(loc, FFN_SUB), :] += jnp.dot(hid, wdb[...], preferred_element_type=F32)

            @pl.when(f == nf - 1)
            def _():
                out_copy(j).start()

            return carry

        lax.fori_loop(0, nsub, compute, 0)

        @pl.when(f == nf - 1)
        def _():
            def drain(j, carry):
                out_copy(j).wait()
                return carry

            lax.fori_loop(0, nsub, drain, 0)


def _grouped_swiglu(tile_e, tile_start, tile_nsub, n_used_sub, xs, w_gate, w_up, w_down):
    P, D = xs.shape
    F = w_gate.shape[2]
    nf = F // FFN_TF
    n_tiles = tile_e.shape[0]

    def f_eff(s, f, tn):
        return jnp.where(tn[s] > 0, f, nf - 1)

    body = functools.partial(_expert_body, nf=nf)
    return pl.pallas_call(
        body,
        out_shape=jax.ShapeDtypeStruct((P, D), F32),
        grid_spec=pltpu.PrefetchScalarGridSpec(
            num_scalar_prefetch=4,
            grid=(n_tiles, nf),
            in_specs=[pl.BlockSpec(memory_space=pl.ANY),
                      pl.BlockSpec((None, D, FFN_TF), lambda s, f, te, ts, tn, used: (te[s], 0, f_eff(s, f, tn))),
                      pl.BlockSpec((None, D, FFN_TF), lambda s, f, te, ts, tn, used: (te[s], 0, f_eff(s, f, tn))),
                      pl.BlockSpec((None, FFN_TF, D), lambda s, f, te, ts, tn, used: (te[s], f_eff(s, f, tn), 0))],
            out_specs=pl.BlockSpec(memory_space=pl.ANY),
            scratch_shapes=[pltpu.VMEM((FFN_ROWS, D), BF16),
                            pltpu.VMEM((FFN_ROWS, D), F32),
                            pltpu.VMEM((2, FFN_SUB, D), F32),
                            pltpu.VMEM((D, FFN_TF), BF16),
                            pltpu.VMEM((D, FFN_TF), BF16),
                            pltpu.VMEM((FFN_TF, D), BF16),
                            pltpu.SemaphoreType.DMA((2,)),
                            pltpu.SemaphoreType.DMA]),
        compiler_params=_params(("arbitrary", "arbitrary")),
        name="grouped_swiglu",
    )(tile_e, tile_start, tile_nsub, n_used_sub, xs, w_gate, w_up, w_down)


def _combine_body(pos_ref, y_ref, x_ref, w_ref, g_ref, o_ref, ybuf, sem, *, tt, seq):
    base = (pl.program_id(0) * seq + pl.program_id(1) * tt) * TOP_K

    def row_copy(i, k):
        return pltpu.make_async_copy(y_ref.at[pl.ds(pos_ref[base + i * TOP_K + k], 1)],
                                     ybuf.at[k, pl.ds(i, 1)], sem)

    def issue(i, carry):
        for k in range(TOP_K):
            row_copy(i, k).start()
        return carry

    def drain(i, carry):
        for k in range(TOP_K):
            row_copy(i, k).wait()
        return carry

    lax.fori_loop(0, tt, issue, 0)
    lax.fori_loop(0, tt, drain, 0)
    w = w_ref[...]
    y = w[:, 0:1] * ybuf[0]
    for k in range(1, TOP_K):
        y = y + w[:, k:k + 1] * ybuf[k]
    o_ref[...] = x_ref[...] + g_ref[...] * y


def _combine(pos, y, x, wts, mod, *, tt):
    B, T, D = x.shape
    body = functools.partial(_combine_body, tt=tt, seq=T)
    return pl.pallas_call(
        body,
        out_shape=jax.ShapeDtypeStruct((B, T, D), F32),
        grid_spec=pltpu.PrefetchScalarGridSpec(
            num_scalar_prefetch=1,
            grid=(B, T // tt),
            in_specs=[pl.BlockSpec(memory_space=pl.ANY),
                      pl.BlockSpec((None, tt, D), lambda b, t, p: (b, t, 0)),
                      pl.BlockSpec((None, tt, LANES), lambda b, t, p: (b, t, 0)),
                      pl.BlockSpec((None, None, 1, D), lambda b, t, p: (b, 5, 0, 0))],
            out_specs=pl.BlockSpec((None, tt, D), lambda b, t, p: (b, t, 0)),
            scratch_shapes=[pltpu.VMEM((TOP_K, tt, D), F32), pltpu.SemaphoreType.DMA]),
        compiler_params=_params(("arbitrary", "arbitrary")),
        name="moe_combine",
    )(pos, y, x, wts, mod)


def _gated_residual_body(x_ref, y_ref, g_ref, o_ref):
    o_ref[...] = x_ref[...] + g_ref[...] * y_ref[...]


def _gated_residual(x, y, mod, k_gate, *, tt):
    B, T, D = x.shape
    spec = pl.BlockSpec((None, tt, D), lambda b, t: (b, t, 0))
    return pl.pallas_call(
        _gated_residual_body,
        out_shape=jax.ShapeDtypeStruct((B, T, D), F32),
        grid=(B, T // tt),
        in_specs=[spec, spec, _mod_spec(k_gate, D)],
        out_specs=spec,
        compiler_params=_params(("parallel", "parallel")),
        name="gated_residual",
    )(x, y, mod)


def kernel(x, c, mod_w, mod_b, norm_mix_g, norm_ffn_g, na_w_in, na_q_norm_g, na_k_norm_g, na_rpb, na_w_out,
           ffn_w_gate, ffn_w_up, ffn_w_down, ml_w_in, ml_conv_w, ml_conv_b, ml_wq, ml_wk, ml_wv, ml_w_if,
           ml_b_if, ml_skip, ml_norm_g, ml_w_out, moe_router, moe_w_gate, moe_w_up, moe_w_down):
    B, T, D = x.shape
    N = B * T
    depth = mod_w.shape[0]
    c_pad = jnp.pad(c, ((0, 8 - B), (0, 0)))
    mix_g = norm_mix_g.reshape(depth, 1, D)
    ffn_g = norm_ffn_g.reshape(depth, 1, D)

    def modulation(i):
        mod = _matmul(c_pad, mod_w, i, tn=1024, tm=8, out_dtype=F32, a_silu=True, bias=mod_b,
                      name="adaln_modulation")
        return mod[:B].reshape(B, 6, 1, D)

    mod = modulation(0)
    h = _normmod(x, mix_g, 0, mod, 1, 0, tt=512, out_dtype=BF16, name="norm_mix0")
    qkv = _matmul(h.reshape(N, D), na_w_in, 0, tn=1024, tm=MM_TM, out_dtype=BF16, name="na_qkv")
    att = _neighborhood_attention(qkv.reshape(B, T, 3 * D), na_q_norm_g[0], na_k_norm_g[0], na_rpb[0])
    x = _matmul(att.reshape(N, D), na_w_out, 0, tn=1024, tm=MM_TM, out_dtype=F32, res=x.reshape(N, D),
                gate=mod, gate_idx=2, rows_per_batch=T, name="na_out").reshape(B, T, D)

    h = _normmod(x, ffn_g, 0, mod, 4, 3, tt=512, out_dtype=F32, name="norm_ffn0")
    n_dense_tiles = N // FFN_ROWS
    spr = FFN_ROWS // FFN_SUB
    y = _grouped_swiglu(jnp.zeros((n_dense_tiles,), jnp.int32),
                        jnp.arange(n_dense_tiles, dtype=jnp.int32) * spr,
                        jnp.full((n_dense_tiles,), spr, jnp.int32),
                        jnp.full((1,), N // FFN_SUB, jnp.int32),
                        h.reshape(N, D), ffn_w_gate, ffn_w_up, ffn_w_down)
    x = _gated_residual(x, y.reshape(B, T, D), mod, 5, tt=512)

    mod = modulation(1)
    h = _normmod(x, mix_g, 1, mod, 1, 0, tt=512, out_dtype=BF16, name="norm_mix1")
    inner = ml_w_in.shape[2] // 2
    xz = _matmul(h.reshape(N, D), ml_w_in, 0, tn=1024, tm=MM_TM, out_dtype=BF16,
                 name="mlstm_in").reshape(B, T, 2 * inner)
    q, k, v, xc, pre = _mlstm_pre(xz, ml_conv_w[0], ml_conv_b[0], ml_wq[0], ml_wk[0], ml_wv[0],
                                  ml_w_if[0], ml_b_if[0], tt=256)
    hd = _mlstm(q, k, v, pre, L=ML_CHUNK)
    u = _mlstm_post(hd, xc, xz, ml_norm_g[0], ml_skip[0], tt=256)
    x = _matmul(u.reshape(N, inner), ml_w_out, 0, tn=512, tm=MM_TM, out_dtype=F32, res=x.reshape(N, D),
                gate=mod, gate_idx=2, rows_per_batch=T, name="mlstm_out").reshape(B, T, D)

    h2, top_idx, top_w = _router(x, ffn_g, 1, mod, moe_router[0], tt=256)
    n_slots = N * TOP_K + N_EXPERTS * FFN_SUB
    n_tiles = N * TOP_K // FFN_ROWS + N_EXPERTS
    pos, src, tile_e, tile_start, tile_nsub, n_used_sub = _route_tables(
        top_idx.reshape(N, LANES)[:, :TOP_K], N_EXPERTS, n_tiles, n_slots)
    xs = _dispatch(src, h2.reshape(N, D), n_slots, chunk=1024)
    ys = _grouped_swiglu(tile_e, tile_start, tile_nsub, n_used_sub, xs, moe_w_gate[0], moe_w_up[0], moe_w_down[0])
    return _combine(pos, ys, x, top_w, mod, tt=256)
```

```python
import functools

import jax
import jax.numpy as jnp
from jax import lax
from jax.experimental import pallas as pl
from jax.experimental.pallas import tpu as pltpu

F32 = jnp.float32
BF16 = jnp.bfloat16

GRID_W = 64
NA_HEADS = 16
NA_WIN_ROWS_MAX = 8
NA_WIN_COLS = 16
ML_HEADS = 8
ML_QKV_BLOCK = 4
ML_CONV_K = 5
N_EXPERTS = 8
TOP_K = 2
RMS_EPS = 1e-6
LN_EPS = 1e-5

V7X_VMEM_LIMIT_BYTES = 56 * 1024 * 1024
LANES = 128
BF16_SUBLANES = 16

MM_TM = 1024
ML_CHUNK = 256
FFN_SUB = 256
FFN_ROWS = 2560
FFN_BLOCK_SUBS = 2
FFN_TF = 256
MASK_NEG = -1e30
NA_ROW_GROUP = 8


def _params(sem):
    return pltpu.CompilerParams(dimension_semantics=sem, vmem_limit_bytes=V7X_VMEM_LIMIT_BYTES)


def _mm_body(*refs, a_silu, has_bias, has_res):
    a_ref, w_ref = refs[0], refs[1]
    k = 2
    if has_bias:
        b_ref = refs[k]
        k += 1
    if has_res:
        r_ref, g_ref = refs[k], refs[k + 1]
        k += 2
    o_ref, wb_ref = refs[k], refs[k + 1]

    @pl.when(pl.program_id(1) == 0)
    def _():
        wb_ref[...] = w_ref[...].astype(BF16)

    a = a_ref[...]
    if a_silu:
        a = a * jax.nn.sigmoid(a)
    y = jnp.dot(a.astype(BF16), wb_ref[...], preferred_element_type=F32)
    if has_bias:
        y = y + b_ref[...]
    if has_res:
        y = r_ref[...] + g_ref[...] * y
    o_ref[...] = y.astype(o_ref.dtype)


def _matmul(a, w, layer, *, tn, tm, out_dtype, a_silu=False, bias=None, res=None, gate=None,
            gate_idx=0, rows_per_batch=None, name="matmul"):
    M, K = a.shape
    N = w.shape[2]
    grid = (N // tn, M // tm)
    in_specs = [pl.BlockSpec((tm, K), lambda j, i: (i, 0)),
                pl.BlockSpec((None, K, tn), lambda j, i: (layer, 0, j))]
    args = [a, w]
    if bias is not None:
        in_specs.append(pl.BlockSpec((None, 1, tn), lambda j, i: (layer, 0, j)))
        args.append(bias.reshape(bias.shape[0], 1, N))
    if res is not None:
        tiles_per_batch = rows_per_batch // tm
        in_specs.append(pl.BlockSpec((tm, tn), lambda j, i: (i, j)))
        in_specs.append(pl.BlockSpec((None, None, 1, tn),
                                     lambda j, i: (i // tiles_per_batch, gate_idx, 0, j)))
        args += [res, gate]
    body = functools.partial(_mm_body, a_silu=a_silu, has_bias=bias is not None,
                             has_res=res is not None)
    return pl.pallas_call(
        body,
        out_shape=jax.ShapeDtypeStruct((M, N), out_dtype),
        grid=grid,
        in_specs=in_specs,
        out_specs=pl.BlockSpec((tm, tn), lambda j, i: (i, j)),
        scratch_shapes=[pltpu.VMEM((K, tn), BF16)],
        compiler_params=_params(("parallel", "arbitrary")),
        name=name,
    )(*args)


def _norm_mod(x, g, sc, sh):
    y = x * lax.rsqrt(jnp.mean(x * x, axis=-1, keepdims=True) + RMS_EPS) * g
    return y * (1.0 + sc) + sh


def _normmod_body(x_ref, g_ref, sc_ref, sh_ref, o_ref):
    o_ref[...] = _norm_mod(x_ref[...], g_ref[...], sc_ref[...], sh_ref[...]).astype(o_ref.dtype)


def _mod_spec(k, D):
    return pl.BlockSpec((None, None, 1, D), lambda b, t: (b, k, 0, 0))


def _normmod(x, g, layer, mod, k_scale, k_shift, *, tt, out_dtype, name):
    B, T, D = x.shape
    return pl.pallas_call(
        _normmod_body,
        out_shape=jax.ShapeDtypeStruct((B, T, D), out_dtype),
        grid=(B, T // tt),
        in_specs=[pl.BlockSpec((None, tt, D), lambda b, t: (b, t, 0)),
                  pl.BlockSpec((None, 1, D), lambda b, t: (layer, 0, 0)),
                  _mod_spec(k_scale, D), _mod_spec(k_shift, D)],
        out_specs=pl.BlockSpec((None, tt, D), lambda b, t: (b, t, 0)),
        compiler_params=_params(("parallel", "parallel")),
        name=name,
    )(x, g, mod, mod)


def _na_body(q_ref, k_ref, v_ref, qg_ref, kg_ref, bias_ref, o_ref, qs, ks, *, rows, width, kh, dh):
    def nrm(t, g):
        tf = t.astype(F32)
        return tf * lax.rsqrt(jnp.mean(tf * tf, axis=-1, keepdims=True) + RMS_EPS) * g

    qs[...] = (nrm(q_ref[...], qg_ref[...]) * (dh ** -0.5)).astype(BF16)
    ks[...] = nrm(k_ref[...], kg_ref[...]).astype(BF16)

    def row_group(gi, carry):
        offs, scores, probs = [], [], []
        for u in range(NA_ROW_GROUP):
            r = gi * NA_ROW_GROUP + u
            r0 = jnp.clip(r - kh // 2, 0, rows - kh)
            q_off = pl.multiple_of(r * width, width)
            b_off = pl.multiple_of(r0 * width, width)
            q_r = qs[pl.ds(q_off, width), :]
            k_band = ks[pl.ds(b_off, kh * width), :]
            s = lax.dot_general(q_r, k_band, (((1,), (1,)), ((), ())), preferred_element_type=F32)
            scores.append(s + bias_ref[r - r0])
            offs.append((q_off, b_off))
        for s in scores:
            e = jnp.exp(s - jnp.max(s, axis=-1, keepdims=True))
            probs.append((e.astype(BF16), jnp.sum(e, axis=-1, keepdims=True)))
        for (q_off, b_off), (e, l) in zip(offs, probs):
            v_band = v_ref[pl.ds(b_off, kh * width), :]
            o = jnp.dot(e, v_band, preferred_element_type=F32) / l
            o_ref[pl.ds(q_off, width), :] = o.astype(o_ref.dtype)
        return carry

    lax.fori_loop(0, rows // NA_ROW_GROUP, row_group, 0)


def _na_bias_table(rpb, kh):
    col = jnp.arange(GRID_W)
    col_start = jnp.clip(col - NA_WIN_COLS // 2, 0, GRID_W - NA_WIN_COLS)
    col_in = (col[None, :] >= col_start[:, None]) & (col[None, :] < col_start[:, None] + NA_WIN_COLS)
    dj_idx = jnp.clip(col[None, :] - col[:, None] + NA_WIN_COLS - 1, 0, 2 * NA_WIN_COLS - 2)
    rpb_cols = jnp.where(col_in[None, None], rpb[:, :, dj_idx].astype(F32), MASK_NEG)
    off = jnp.arange(kh)[:, None]
    band = jnp.arange(kh)[None, :]
    di = band - off + NA_WIN_ROWS_MAX - 1
    tab = rpb_cols[:, di]
    H = rpb.shape[0]
    return jnp.transpose(tab, (0, 1, 3, 2, 4)).reshape(H, kh, GRID_W, kh * GRID_W)


def _neighborhood_attention(qkv, q_g, k_g, rpb):
    B, T, D3 = qkv.shape
    D = D3 // 3
    H = NA_HEADS
    dh = D // H
    rows = T // GRID_W
    kh = min(NA_WIN_ROWS_MAX, rows)
    bias = _na_bias_table(rpb, kh)
    body = functools.partial(_na_body, rows=rows, width=GRID_W, kh=kh, dh=dh)
    return pl.pallas_call(
        body,
        out_shape=jax.ShapeDtypeStruct((B, T, D), BF16),
        grid=(B, H),
        in_specs=[pl.BlockSpec((None, T, dh), lambda b, h: (b, 0, h)),
                  pl.BlockSpec((None, T, dh), lambda b, h: (b, 0, H + h)),
                  pl.BlockSpec((None, T, dh), lambda b, h: (b, 0, 2 * H + h)),
                  pl.BlockSpec((1, dh), lambda b, h: (0, 0)),
                  pl.BlockSpec((1, dh), lambda b, h: (0, 0)),
                  pl.BlockSpec((None, kh, GRID_W, kh * GRID_W), lambda b, h: (h, 0, 0, 0))],
        out_specs=pl.BlockSpec((None, T, dh), lambda b, h: (b, 0, h)),
        scratch_shapes=[pltpu.VMEM((T, dh), BF16), pltpu.VMEM((T, dh), BF16)],
        compiler_params=_params(("parallel", "parallel")),
        name="neighborhood_attention",
    )(qkv, qkv, qkv, q_g.reshape(1, dh), k_g.reshape(1, dh), bias)


def _mlstm_pre_body(xm_ref, xp_ref, xn_ref, cw_ref, cb_ref, wq_ref, wk_ref, wv_ref, wif_ref, bif_ref,
                    q_ref, k_ref, v_ref, xc_ref, pre_ref, *, tt, inner, halo):
    t = pl.program_id(1)
    first = (t > 0).astype(F32)
    last = (t < pl.num_programs(1) - 1).astype(F32)
    pad = ML_CONV_K // 2
    n = tt + 2 * halo
    pre = jnp.zeros((tt, LANES), F32)
    for c in range(inner // LANES):
        sl = slice(c * LANES, (c + 1) * LANES)
        cur_b = xm_ref[:, sl]
        cat = jnp.concatenate([xp_ref[:, sl].astype(F32) * first, cur_b.astype(F32),
                               xn_ref[:, sl].astype(F32) * last], axis=0)
        xc = jnp.zeros((tt, LANES), F32) + cb_ref[:, sl]
        for j in range(ML_CONV_K):
            sh = pltpu.roll(cat, (pad - j) % n, axis=0) if j != pad else cat
            xc = xc + sh[halo:halo + tt] * cw_ref[j:j + 1, sl]
        xc = xc * jax.nn.sigmoid(xc)
        xc_b = xc.astype(BF16)
        q = jnp.dot(xc_b, wq_ref[c], preferred_element_type=F32).astype(BF16)
        k = jnp.dot(xc_b, wk_ref[c], preferred_element_type=F32).astype(BF16)
        v = jnp.dot(cur_b, wv_ref[c], preferred_element_type=F32).astype(BF16)
        pre = pre + jnp.dot(q, wif_ref[0, sl, :], preferred_element_type=F32)
        pre = pre + jnp.dot(k, wif_ref[1, sl, :], preferred_element_type=F32)
        pre = pre + jnp.dot(v, wif_ref[2, sl, :], preferred_element_type=F32)
        q_ref[:, sl] = q
        k_ref[:, sl] = k
        v_ref[:, sl] = v
        xc_ref[:, sl] = xc_b
    pre_ref[...] = pre + bif_ref[...]


def _block_diag_dense(w):
    nb = w.shape[0]
    per = LANES // ML_QKV_BLOCK
    wr = w.reshape(nb // per, per, ML_QKV_BLOCK, ML_QKV_BLOCK)
    eye = jnp.eye(per, dtype=w.dtype)
    dense = jnp.einsum('gnio,nm->gnimo', wr, eye)
    return dense.reshape(nb // per, LANES, LANES).astype(BF16)


def _mlstm_pre(xz, conv_w, conv_b, wq, wk, wv, w_if, b_if, *, tt):
    B, T, inner2 = xz.shape
    inner = inner2 // 2
    halo = BF16_SUBLANES
    n_gate = w_if.shape[1] * w_if.shape[2] * w_if.shape[3]
    wif = jnp.pad(w_if.reshape(3, inner, n_gate), ((0, 0), (0, 0), (0, LANES - n_gate))).astype(BF16)
    bif = jnp.pad(b_if.reshape(1, n_gate), ((0, 0), (0, LANES - n_gate)))
    nblk = inner // LANES
    hb = tt // halo
    nh = T // halo
    body = functools.partial(_mlstm_pre_body, tt=tt, inner=inner, halo=halo)
    act = jax.ShapeDtypeStruct((B, T, inner), BF16)
    act_spec = pl.BlockSpec((None, tt, inner), lambda b, t: (b, t, 0))
    const3 = lambda b, t: (0, 0, 0)
    return pl.pallas_call(
        body,
        out_shape=(act, act, act, act, jax.ShapeDtypeStruct((B, T, LANES), F32)),
        grid=(B, T // tt),
        in_specs=[act_spec,
                  pl.BlockSpec((None, halo, inner), lambda b, t: (b, jnp.maximum(t * hb - 1, 0), 0)),
                  pl.BlockSpec((None, halo, inner), lambda b, t: (b, jnp.minimum((t + 1) * hb, nh - 1), 0)),
                  pl.BlockSpec((ML_CONV_K, inner), lambda b, t: (0, 0)),
                  pl.BlockSpec((1, inner), lambda b, t: (0, 0)),
                  pl.BlockSpec((nblk, LANES, LANES), const3),
                  pl.BlockSpec((nblk, LANES, LANES), const3),
                  pl.BlockSpec((nblk, LANES, LANES), const3),
                  pl.BlockSpec((3, inner, LANES), const3),
                  pl.BlockSpec((1, LANES), lambda b, t: (0, 0))],
        out_specs=(act_spec, act_spec, act_spec, act_spec,
                   pl.BlockSpec((None, tt, LANES), lambda b, t: (b, t, 0))),
        compiler_params=_params(("parallel", "parallel")),
        name="mlstm_pre",
    )(xz, xz, xz, conv_w, conv_b.reshape(1, inner), _block_diag_dense(wq), _block_diag_dense(wk),
      _block_diag_dense(wv), wif, bif)


def _log_sigmoid(x):
    return jnp.minimum(x, 0.0) - jnp.log(1.0 + jnp.exp(-jnp.abs(x)))


def _mlstm_direction(q, k, v, li_r, lf_r, li_c, lf_c, S, nv, m_ref, o_ref, *, reverse, L, dq):
    scale = dq ** -0.5
    lf_r = _log_sigmoid(lf_r)
    lf_c = _log_sigmoid(lf_c)
    row = lax.broadcasted_iota(jnp.int32, (L, L), 0)
    col = lax.broadcasted_iota(jnp.int32, (L, L), 1)
    vis = (col >= row) if reverse else (col <= row)
    vis_t = (row >= col) if reverse else (row <= col)
    b_c = jnp.sum(jnp.where(vis, lf_r, 0.0), axis=1, keepdims=True)
    b_r = jnp.sum(jnp.where(vis_t, lf_c, 0.0), axis=0, keepdims=True)
    g = jnp.sum(lf_r, axis=1, keepdims=True)
    m = m_ref[...]

    dlog = jnp.where(vis, b_c - b_r + li_r, -jnp.inf)
    m_inter = b_c + m
    m_t = jnp.maximum(m_inter, jnp.max(dlog, axis=1, keepdims=True))
    qk = lax.dot_general(q, k, (((1,), (1,)), ((), ())), preferred_element_type=F32)
    s = qk * (jnp.exp(dlog - m_t) * scale)
    inter = jnp.exp(m_inter - m_t)
    num = jnp.dot(s.astype(BF16), v, preferred_element_type=F32)
    num = num + jnp.dot(q, S[...].astype(BF16), preferred_element_type=F32) * inter
    den = jnp.sum(s, axis=1, keepdims=True) + jnp.sum(q.astype(F32) * nv[...], axis=1, keepdims=True) * inter
    o_ref[...] = (num / jnp.maximum(jnp.abs(den), jnp.exp(-m_t))).astype(o_ref.dtype)

    a = g - b_c + li_c
    m_new = jnp.maximum(g + m, jnp.max(a, axis=0, keepdims=True))
    w = jnp.exp(a - m_new) * scale
    decay = jnp.exp(g + m - m_new)
    kw = k.astype(F32) * w
    S[...] = decay * S[...] + lax.dot_general(kw.astype(BF16), v, (((0,), (0,)), ((), ())),
                                              preferred_element_type=F32)
    nv[...] = decay * nv[...] + jnp.sum(kw, axis=0, keepdims=True)
    m_ref[...] = m_new


def _mlstm_body(qf_ref, kf_ref, vf_ref, qb_ref, kb_ref, vb_ref, gr_f_ref, gr_b_ref, gc_f_ref, gc_b_ref,
                of_ref, ob_ref, S, nv, m_ref, *, L, dq, heads):
    h = pl.program_id(1)

    @pl.when(pl.program_id(2) == 0)
    def _():
        S[...] = jnp.zeros_like(S)
        nv[...] = jnp.zeros_like(nv)
        m_ref[...] = jnp.zeros_like(m_ref)

    lane = lax.broadcasted_iota(jnp.int32, (L, LANES), 1)
    streams = ((qf_ref, kf_ref, vf_ref, gr_f_ref, gc_f_ref, of_ref),
               (qb_ref, kb_ref, vb_ref, gr_b_ref, gc_b_ref, ob_ref))
    for d, (q_ref, k_ref, v_ref, gr_ref, gc_ref, o_ref) in enumerate(streams):
        i_idx = d * 2 * heads + h
        f_idx = i_idx + heads
        gc = gc_ref[...]
        pick = lambda idx: jnp.sum(jnp.where(lane == idx, gc, 0.0), axis=1, keepdims=True)
        _mlstm_direction(q_ref[...], k_ref[...], v_ref[...],
                         gr_ref[pl.ds(i_idx, 1), :], gr_ref[pl.ds(f_idx, 1), :], pick(i_idx), pick(f_idx),
                         S.at[d], nv.at[d], m_ref.at[d], o_ref, reverse=bool(d), L=L, dq=dq)


def _mlstm(q, k, v, pre, *, L):
    B, T, inner = q.shape
    H = ML_HEADS
    dh = inner // H
    n = T // L
    n_gate = 4 * H
    gate_rows = jnp.transpose(pre[:, :, :n_gate], (0, 2, 1))

    fwd = lambda c: c
    bwd = lambda c: n - 1 - c
    qkv_spec = lambda at: pl.BlockSpec((None, L, dh), lambda b, h, c: (b, at(c), h))
    row_spec = lambda at: pl.BlockSpec((None, n_gate, L), lambda b, h, c: (b, 0, at(c)))
    col_spec = lambda at: pl.BlockSpec((None, L, LANES), lambda b, h, c: (b, at(c), 0))
    out = jax.ShapeDtypeStruct((B, T, inner), BF16)
    body = functools.partial(_mlstm_body, L=L, dq=dh, heads=H)
    return pl.pallas_call(
        body,
        out_shape=(out, out),
        grid=(B, H, n),
        in_specs=[qkv_spec(fwd), qkv_spec(fwd), qkv_spec(fwd), qkv_spec(bwd), qkv_spec(bwd), qkv_spec(bwd),
                  row_spec(fwd), row_spec(bwd), col_spec(fwd), col_spec(bwd)],
        out_specs=(qkv_spec(fwd), qkv_spec(bwd)),
        scratch_shapes=[pltpu.VMEM((2, dh, dh), F32), pltpu.VMEM((2, 1, dh), F32), pltpu.VMEM((2, 1, 1), F32)],
        compiler_params=_params(("parallel", "parallel", "arbitrary")),
        name="mlstm_chunkwise",
    )(q, k, v, q, k, v, gate_rows, gate_rows, pre, pre)


def _mlstm_post_body(hf_ref, hb_ref, xc_ref, z_ref, ng_ref, skip_ref, o_ref, *, dh, inner):
    for h in range(inner // dh):
        sl = slice(h * dh, (h + 1) * dh)
        ht = hf_ref[:, sl].astype(F32) + hb_ref[:, sl].astype(F32)
        mu = jnp.mean(ht, axis=-1, keepdims=True)
        var = jnp.mean(jnp.square(ht - mu), axis=-1, keepdims=True)
        hn = (ht - mu) * lax.rsqrt(var + LN_EPS) * ng_ref[:, sl]
        out = (hn + skip_ref[:, sl] * xc_ref[:, sl].astype(F32)) * jax.nn.sigmoid(z_ref[:, sl].astype(F32))
        o_ref[:, sl] = out.astype(o_ref.dtype)


def _mlstm_post(hf, hb, xc, xz, norm_g, skip, *, tt):
    B, T, inner = hf.shape
    dh = inner // ML_HEADS
    body = functools.partial(_mlstm_post_body, dh=dh, inner=inner)
    vec = pl.BlockSpec((1, inner), lambda b, t: (0, 0))
    return pl.pallas_call(
        body,
        out_shape=jax.ShapeDtypeStruct((B, T, inner), BF16),
        grid=(B, T // tt),
        in_specs=[pl.BlockSpec((None, tt, inner), lambda b, t: (b, t, 0)),
                  pl.BlockSpec((None, tt, inner), lambda b, t: (b, t, 0)),
                  pl.BlockSpec((None, tt, inner), lambda b, t: (b, t, 0)),
                  pl.BlockSpec((None, tt, inner), lambda b, t: (b, t, 1)),
                  vec, vec],
        out_specs=pl.BlockSpec((None, tt, inner), lambda b, t: (b, t, 0)),
        compiler_params=_params(("parallel", "parallel")),
        name="mlstm_post",
    )(hf, hb, xc, xz, norm_g.reshape(1, inner), skip.reshape(1, inner))


def _router_body(x_ref, g_ref, sc_ref, sh_ref, r_ref, h_ref, idx_ref, wt_ref, *, n_experts):
    h = _norm_mod(x_ref[...], g_ref[...], sc_ref[...], sh_ref[...])
    h_ref[...] = h
    logits = jnp.dot(h, r_ref[...], precision=lax.Precision.HIGHEST, preferred_element_type=F32)
    lane = lax.broadcasted_iota(jnp.int32, logits.shape, 1)
    lg = jnp.where(lane < n_experts, logits, -jnp.inf)
    m1 = jnp.max(lg, axis=1, keepdims=True)
    i1 = jnp.min(jnp.where(lg == m1, lane, LANES), axis=1, keepdims=True)
    lg2 = jnp.where(lane == i1, -jnp.inf, lg)
    m2 = jnp.max(lg2, axis=1, keepdims=True)
    i2 = jnp.min(jnp.where(lg2 == m2, lane, LANES), axis=1, keepdims=True)
    e2 = jnp.exp(m2 - m1)
    w1 = 1.0 / (1.0 + e2)
    w2 = e2 * w1
    idx_ref[...] = jnp.where(lane == 0, i1, jnp.where(lane == 1, i2, 0))
    wt_ref[...] = jnp.where(lane == 0, w1, jnp.where(lane == 1, w2, 0.0))


def _router(x, g, layer, mod, router, *, tt):
    B, T, D = x.shape
    E = router.shape[1]
    r_pad = jnp.pad(router, ((0, 0), (0, LANES - E)))
    body = functools.partial(_router_body, n_experts=E)
    small = pl.BlockSpec((None, tt, LANES), lambda b, t: (b, t, 0))
    return pl.pallas_call(
        body,
        out_shape=(jax.ShapeDtypeStruct((B, T, D), F32),
                   jax.ShapeDtypeStruct((B, T, LANES), jnp.int32),
                   jax.ShapeDtypeStruct((B, T, LANES), F32)),
        grid=(B, T // tt),
        in_specs=[pl.BlockSpec((None, tt, D), lambda b, t: (b, t, 0)),
                  pl.BlockSpec((None, 1, D), lambda b, t: (layer, 0, 0)),
                  _mod_spec(4, D), _mod_spec(3, D),
                  pl.BlockSpec((D, LANES), lambda b, t: (0, 0))],
        out_specs=(pl.BlockSpec((None, tt, D), lambda b, t: (b, t, 0)), small, small),
        compiler_params=_params(("parallel", "parallel")),
        name="moe_router",
    )(x, g, mod, mod, r_pad)


def _route_tables(top_i, n_experts, n_tiles, n_slots):
    n_pairs = top_i.size
    e_flat = top_i.reshape(n_pairs)
    onehot = (e_flat[:, None] == jnp.arange(n_experts, dtype=jnp.int32)[None, :]).astype(jnp.int32)
    csum = jnp.cumsum(onehot, axis=0)
    counts = csum[-1]
    rank = jnp.sum(csum * onehot, axis=1) - 1
    nsub = (counts + FFN_SUB - 1) // FFN_SUB
    goff = jnp.cumsum(nsub) - nsub
    pos = jnp.sum(onehot * goff[None, :], axis=1) * FFN_SUB + rank
    src = jnp.zeros((n_slots,), jnp.int32).at[pos].set(jnp.arange(n_pairs, dtype=jnp.int32) // TOP_K)

    spr = FFN_ROWS // FFN_SUB
    ntile = (nsub + spr - 1) // spr
    per_tile = (nsub + jnp.maximum(ntile, 1) - 1) // jnp.maximum(ntile, 1)
    tcum = jnp.cumsum(ntile)
    toff = tcum - ntile
    ids = jnp.arange(n_tiles, dtype=jnp.int32)
    te = jnp.minimum(jnp.sum((ids[:, None] >= tcum[None, :]).astype(jnp.int32), axis=1), n_experts - 1)
    valid = ids < tcum[-1]
    j = ids - toff[te]
    t_start = goff[te] + j * per_tile[te]
    t_nsub = jnp.minimum(per_tile[te], nsub[te] - j * per_tile[te])
    last_e = te[jnp.maximum(tcum[-1] - 1, 0)]
    tile_e = jnp.where(valid, te, last_e).astype(jnp.int32)
    tile_start = jnp.where(valid, t_start, 0).astype(jnp.int32)
    tile_nsub = jnp.where(valid, t_nsub, 0).astype(jnp.int32)
    n_used_sub = jnp.sum(nsub).astype(jnp.int32).reshape(1)
    return pos.astype(jnp.int32), src, tile_e, tile_start, tile_nsub, n_used_sub


def _dense_tiles(n_rows):
    nsub = n_rows // FFN_SUB
    spr = FFN_ROWS // FFN_SUB
    ntile = -(-nsub // spr)
    per_tile = -(-nsub // ntile)
    starts = [j * per_tile for j in range(ntile)]
    counts = [min(per_tile, nsub - st) for st in starts]
    return (jnp.zeros((ntile,), jnp.int32), jnp.array(starts, jnp.int32), jnp.array(counts, jnp.int32))


def _dispatch_body(src_ref, h_ref, o_ref, buf, sem, *, chunk):
    i = pl.program_id(0)

    def gather(step, slot):
        base = step * chunk

        def issue(r, carry):
            pltpu.make_async_copy(h_ref.at[pl.ds(src_ref[base + r], 1)], buf.at[slot, pl.ds(r, 1)],
                                  sem.at[slot]).start()
            return carry

        lax.fori_loop(0, chunk, issue, 0, unroll=8)

    @pl.when(i == 0)
    def _():
        gather(0, 0)

    @pl.when(i + 1 < pl.num_programs(0))
    def _():
        gather(i + 1, (i + 1) % 2)

    slot = i % 2
    pltpu.make_async_copy(h_ref.at[pl.ds(0, chunk)], buf.at[slot], sem.at[slot]).wait()
    o_ref[...] = buf[slot].astype(o_ref.dtype)


def _dispatch(src, h, n_slots, *, chunk):
    N, D = h.shape
    body = functools.partial(_dispatch_body, chunk=chunk)
    return pl.pallas_call(
        body,
        out_shape=jax.ShapeDtypeStruct((n_slots, D), BF16),
        grid_spec=pltpu.PrefetchScalarGridSpec(
            num_scalar_prefetch=1,
            grid=(n_slots // chunk,),
            in_specs=[pl.BlockSpec(memory_space=pl.ANY)],
            out_specs=pl.BlockSpec((chunk, D), lambda i, src: (i, 0)),
            scratch_shapes=[pltpu.VMEM((2, chunk, D), F32), pltpu.SemaphoreType.DMA((2,))]),
        compiler_params=_params(("arbitrary",)),
        name="moe_dispatch",
    )(src, h)


def _expert_body(te_ref, ts_ref, tn_ref, used_ref, xs_ref, wg_ref, wu_ref, wd_ref, y_ref,
                 xb, acc, wgb, wub, wdb, sem_in, sem_out, *, nf):
    s = pl.program_id(0)
    f = pl.program_id(1)
    nsub = tn_ref[s]
    start = ts_ref[s]
    d_model = acc.shape[1]

    def local_rows(j, n=1):
        return pl.ds(pl.multiple_of(j * FFN_SUB, FFN_SUB), n * FFN_SUB)

    def hbm_rows(j):
        return pl.ds(pl.multiple_of((start + j) * FFN_SUB, FFN_SUB), FFN_SUB)

    def in_copy(j):
        return pltpu.make_async_copy(xs_ref.at[hbm_rows(j)], xb.at[local_rows(j)], sem_in)

    def out_copy(j):
        return pltpu.make_async_copy(acc.at[local_rows(j)], y_ref.at[hbm_rows(j)], sem_out)

    def for_each_sub(lo, hi, fn):
        def step(j, carry):
            fn(j)
            return carry

        lax.fori_loop(lo, hi, step, 0)

    def zero_acc(j):
        acc[local_rows(j), :] = jnp.zeros((FFN_SUB, d_model), F32)

    @pl.when((s == 0) & (f == 0))
    def _():
        def fill_copy(j):
            row = pl.multiple_of(j * FFN_SUB, FFN_SUB)
            return pltpu.make_async_copy(acc.at[local_rows(0)], y_ref.at[pl.ds(row, FFN_SUB)], sem_out)

        n_sub_total = y_ref.shape[0] // FFN_SUB
        zero_acc(0)
        for_each_sub(used_ref[0], n_sub_total, lambda j: fill_copy(j).start())
        for_each_sub(used_ref[0], n_sub_total, lambda j: fill_copy(j).wait())

    @pl.when(nsub > 0)
    def _():
        @pl.when(f == 0)
        def _():
            for_each_sub(0, nsub, lambda j: in_copy(j).start())
            for_each_sub(0, nsub, zero_acc)
            for_each_sub(0, nsub, lambda j: in_copy(j).wait())

        wgb[...] = wg_ref[...].astype(BF16)
        wub[...] = wu_ref[...].astype(BF16)
        wdb[...] = wd_ref[...].astype(BF16)

        def block(j0, n):
            rows = local_rows(j0, n)
            x = xb[rows, :]
            hg = jnp.dot(x, wgb[...], preferred_element_type=F32)
            hu = jnp.dot(x, wub[...], preferred_element_type=F32)
            hid = (hg * jax.nn.sigmoid(hg) * hu).astype(BF16)
            acc[rows, :] += jnp.dot(hid, wdb[...], preferred_element_type=F32)

            @pl.when(f == nf - 1)
            def _():
                for u in range(n):
                    out_copy(j0 + u).start()

        for_each_sub(0, nsub // FFN_BLOCK_SUBS, lambda p: block(p * FFN_BLOCK_SUBS, FFN_BLOCK_SUBS))
        for_each_sub(nsub - nsub % FFN_BLOCK_SUBS, nsub, lambda j: block(j, 1))

        @pl.when(f == nf - 1)
        def _():
            for_each_sub(0, nsub, lambda j: out_copy(j).wait())


def _grouped_swiglu(tile_e, tile_start, tile_nsub, n_used_sub, xs, w_gate, w_up, w_down):
    P, D = xs.shape
    F = w_gate.shape[2]
    nf = F // FFN_TF
    n_tiles = tile_e.shape[0]

    def f_eff(s, f, tn):
        return jnp.where(tn[s] > 0, f, nf - 1)

    body = functools.partial(_expert_body, nf=nf)
    return pl.pallas_call(
        body,
        out_shape=jax.ShapeDtypeStruct((P, D), F32),
        grid_spec=pltpu.PrefetchScalarGridSpec(
            num_scalar_prefetch=4,
            grid=(n_tiles, nf),
            in_specs=[pl.BlockSpec(memory_space=pl.ANY),
                      pl.BlockSpec((None, D, FFN_TF), lambda s, f, te, ts, tn, used: (te[s], 0, f_eff(s, f, tn))),
                      pl.BlockSpec((None, D, FFN_TF), lambda s, f, te, ts, tn, used: (te[s], 0, f_eff(s, f, tn))),
                      pl.BlockSpec((None, FFN_TF, D), lambda s, f, te, ts, tn, used: (te[s], f_eff(s, f, tn), 0))],
            out_specs=pl.BlockSpec(memory_space=pl.ANY),
            scratch_shapes=[pltpu.VMEM((FFN_ROWS, D), BF16),
                            pltpu.VMEM((FFN_ROWS, D), F32),
                            pltpu.VMEM((D, FFN_TF), BF16),
                            pltpu.VMEM((D, FFN_TF), BF16),
                            pltpu.VMEM((FFN_TF, D), BF16),
                            pltpu.SemaphoreType.DMA,
                            pltpu.SemaphoreType.DMA]),
        compiler_params=_params(("arbitrary", "arbitrary")),
        name="grouped_swiglu",
    )(tile_e, tile_start, tile_nsub, n_used_sub, xs, w_gate, w_up, w_down)


def _combine_body(pos_ref, y_ref, x_ref, w_ref, g_ref, o_ref, ybuf, sem, *, tt):
    i = pl.program_id(0)

    def gather(step, slot):
        base = step * (tt * TOP_K)

        def issue(r, carry):
            for k in range(TOP_K):
                pltpu.make_async_copy(y_ref.at[pl.ds(pos_ref[base + r * TOP_K + k], 1)],
                                      ybuf.at[slot, k, pl.ds(r, 1)], sem.at[slot]).start()
            return carry

        lax.fori_loop(0, tt, issue, 0, unroll=4)

    @pl.when(i == 0)
    def _():
        gather(0, 0)

    @pl.when(i + 1 < pl.num_programs(0))
    def _():
        gather(i + 1, (i + 1) % 2)

    slot = i % 2
    for k in range(TOP_K):
        pltpu.make_async_copy(y_ref.at[pl.ds(0, tt)], ybuf.at[slot, k], sem.at[slot]).wait()
    w = w_ref[...]
    y = w[:, 0:1] * ybuf[slot, 0]
    for k in range(1, TOP_K):
        y = y + w[:, k:k + 1] * ybuf[slot, k]
    o_ref[...] = x_ref[...] + g_ref[...] * y


def _combine(pos, y, x, wts, mod, *, tt):
    N, D = x.shape
    tiles_per_batch = N // mod.shape[0] // tt
    body = functools.partial(_combine_body, tt=tt)
    return pl.pallas_call(
        body,
        out_shape=jax.ShapeDtypeStruct((N, D), F32),
        grid_spec=pltpu.PrefetchScalarGridSpec(
            num_scalar_prefetch=1,
            grid=(N // tt,),
            in_specs=[pl.BlockSpec(memory_space=pl.ANY),
                      pl.BlockSpec((tt, D), lambda i, p: (i, 0)),
                      pl.BlockSpec((tt, LANES), lambda i, p: (i, 0)),
                      pl.BlockSpec((None, None, 1, D), lambda i, p: (i // tiles_per_batch, 5, 0, 0))],
            out_specs=pl.BlockSpec((tt, D), lambda i, p: (i, 0)),
            scratch_shapes=[pltpu.VMEM((2, TOP_K, tt, D), F32), pltpu.SemaphoreType.DMA((2,))]),
        compiler_params=_params(("arbitrary",)),
        name="moe_combine",
    )(pos, y, x, wts, mod)


def _gated_residual_body(x_ref, y_ref, g_ref, o_ref):
    o_ref[...] = x_ref[...] + g_ref[...] * y_ref[...]


def _gated_residual(x, y, mod, k_gate, *, tt):
    B, T, D = x.shape
    spec = pl.BlockSpec((None, tt, D), lambda b, t: (b, t, 0))
    return pl.pallas_call(
        _gated_residual_body,
        out_shape=jax.ShapeDtypeStruct((B, T, D), F32),
        grid=(B, T // tt),
        in_specs=[spec, spec, _mod_spec(k_gate, D)],
        out_specs=spec,
        compiler_params=_params(("parallel", "parallel")),
        name="gated_residual",
    )(x, y, mod)


def kernel(x, c, mod_w, mod_b, norm_mix_g, norm_ffn_g, na_w_in, na_q_norm_g, na_k_norm_g, na_rpb, na_w_out,
           ffn_w_gate, ffn_w_up, ffn_w_down, ml_w_in, ml_conv_w, ml_conv_b, ml_wq, ml_wk, ml_wv, ml_w_if,
           ml_b_if, ml_skip, ml_norm_g, ml_w_out, moe_router, moe_w_gate, moe_w_up, moe_w_down):
    B, T, D = x.shape
    N = B * T
    depth = mod_w.shape[0]
    c_pad = jnp.pad(c, ((0, 8 - B), (0, 0)))
    mix_g = norm_mix_g.reshape(depth, 1, D)
    ffn_g = norm_ffn_g.reshape(depth, 1, D)

    def modulation(i):
        mod = _matmul(c_pad, mod_w, i, tn=1024, tm=8, out_dtype=F32, a_silu=True, bias=mod_b,
                      name="adaln_modulation")
        return mod[:B].reshape(B, 6, 1, D)

    mod = modulation(0)
    h = _normmod(x, mix_g, 0, mod, 1, 0, tt=512, out_dtype=BF16, name="norm_mix0")
    qkv = _matmul(h.reshape(N, D), na_w_in, 0, tn=1024, tm=MM_TM, out_dtype=BF16, name="na_qkv")
    att = _neighborhood_attention(qkv.reshape(B, T, 3 * D), na_q_norm_g[0], na_k_norm_g[0], na_rpb[0])
    x = _matmul(att.reshape(N, D), na_w_out, 0, tn=1024, tm=MM_TM, out_dtype=F32, res=x.reshape(N, D),
                gate=mod, gate_idx=2, rows_per_batch=T, name="na_out").reshape(B, T, D)

    h = _normmod(x, ffn_g, 0, mod, 4, 3, tt=512, out_dtype=BF16, name="norm_ffn0")
    dense_e, dense_start, dense_nsub = _dense_tiles(N)
    y = _grouped_swiglu(dense_e, dense_start, dense_nsub, jnp.full((1,), N // FFN_SUB, jnp.int32),
                        h.reshape(N, D), ffn_w_gate, ffn_w_up, ffn_w_down)
    x = _gated_residual(x, y.reshape(B, T, D), mod, 5, tt=512)

    mod = modulation(1)
    h = _normmod(x, mix_g, 1, mod, 1, 0, tt=512, out_dtype=BF16, name="norm_mix1")
    inner = ml_w_in.shape[2] // 2
    xz = _matmul(h.reshape(N, D), ml_w_in, 0, tn=1024, tm=MM_TM, out_dtype=BF16,
                 name="mlstm_in").reshape(B, T, 2 * inner)
    q, k, v, xc, pre = _mlstm_pre(xz, ml_conv_w[0], ml_conv_b[0], ml_wq[0], ml_wk[0], ml_wv[0],
                                  ml_w_if[0], ml_b_if[0], tt=256)
    hf, hb = _mlstm(q, k, v, pre, L=ML_CHUNK)
    u = _mlstm_post(hf, hb, xc, xz, ml_norm_g[0], ml_skip[0], tt=256)
    x = _matmul(u.reshape(N, inner), ml_w_out, 0, tn=512, tm=MM_TM, out_dtype=F32, res=x.reshape(N, D),
                gate=mod, gate_idx=2, rows_per_batch=T, name="mlstm_out").reshape(B, T, D)

    h2, top_idx, top_w = _router(x, ffn_g, 1, mod, moe_router[0], tt=256)
    n_slots = N * TOP_K + N_EXPERTS * FFN_SUB
    n_tiles = -(-N * TOP_K // FFN_ROWS) + N_EXPERTS
    pos, src, tile_e, tile_start, tile_nsub, n_used_sub = _route_tables(
        top_idx.reshape(N, LANES)[:, :TOP_K], N_EXPERTS, n_tiles, n_slots)
    xs = _dispatch(src, h2.reshape(N, D), n_slots, chunk=FFN_SUB)
    ys = _grouped_swiglu(tile_e, tile_start, tile_nsub, n_used_sub, xs, moe_w_gate[0], moe_w_up[0], moe_w_down[0])
    return _combine(pos, ys, x.reshape(N, D), top_w.reshape(N, LANES), mod, tt=256).reshape(B, T, D)
```

```python
import functools

import jax
import jax.numpy as jnp
from jax import lax
from jax.experimental import pallas as pl
from jax.experimental.pallas import tpu as pltpu

F32 = jnp.float32
BF16 = jnp.bfloat16

GRID_W = 64
NA_HEADS = 16
NA_WIN_ROWS_MAX = 8
NA_WIN_COLS = 16
ML_HEADS = 8
ML_QKV_BLOCK = 4
ML_CONV_K = 5
N_EXPERTS = 8
TOP_K = 2
RMS_EPS = 1e-6
LN_EPS = 1e-5

V7X_VMEM_LIMIT_BYTES = 56 * 1024 * 1024
LANES = 128
BF16_SUBLANES = 16

MM_TM = 1024
ML_CHUNK = 256
ML_HEADS_PER_STEP = 2
FFN_SUB = 256
FFN_ROWS = 3072
FFN_BLOCK_SUBS = (4, 2, 1)
FFN_TF = 256
MASK_NEG = -1e30
NA_ROW_GROUP = 8


def _params(sem):
    return pltpu.CompilerParams(dimension_semantics=sem, vmem_limit_bytes=V7X_VMEM_LIMIT_BYTES)


def _mm_body(*refs, a_silu, has_bias, has_res):
    a_ref, w_ref = refs[0], refs[1]
    k = 2
    if has_bias:
        b_ref = refs[k]
        k += 1
    if has_res:
        r_ref, g_ref = refs[k], refs[k + 1]
        k += 2
    o_ref, wb_ref = refs[k], refs[k + 1]

    @pl.when(pl.program_id(1) == 0)
    def _():
        wb_ref[...] = w_ref[...].astype(BF16)

    a = a_ref[...]
    if a_silu:
        a = a * jax.nn.sigmoid(a)
    y = jnp.dot(a.astype(BF16), wb_ref[...], preferred_element_type=F32)
    if has_bias:
        y = y + b_ref[...]
    if has_res:
        y = r_ref[...] + g_ref[...] * y
    o_ref[...] = y.astype(o_ref.dtype)


def _matmul(a, w, layer, *, tn, tm, out_dtype, a_silu=False, bias=None, res=None, gate=None,
            gate_idx=0, rows_per_batch=None, name="matmul"):
    M, K = a.shape
    N = w.shape[2]
    grid = (N // tn, M // tm)
    in_specs = [pl.BlockSpec((tm, K), lambda j, i: (i, 0)),
                pl.BlockSpec((None, K, tn), lambda j, i: (layer, 0, j))]
    args = [a, w]
    if bias is not None:
        in_specs.append(pl.BlockSpec((None, 1, tn), lambda j, i: (layer, 0, j)))
        args.append(bias.reshape(bias.shape[0], 1, N))
    if res is not None:
        tiles_per_batch = rows_per_batch // tm
        in_specs.append(pl.BlockSpec((tm, tn), lambda j, i: (i, j)))
        in_specs.append(pl.BlockSpec((None, None, 1, tn),
                                     lambda j, i: (i // tiles_per_batch, gate_idx, 0, j)))
        args += [res, gate]
    body = functools.partial(_mm_body, a_silu=a_silu, has_bias=bias is not None,
                             has_res=res is not None)
    return pl.pallas_call(
        body,
        out_shape=jax.ShapeDtypeStruct((M, N), out_dtype),
        grid=grid,
        in_specs=in_specs,
        out_specs=pl.BlockSpec((tm, tn), lambda j, i: (i, j)),
        scratch_shapes=[pltpu.VMEM((K, tn), BF16)],
        compiler_params=_params(("parallel", "arbitrary")),
        name=name,
    )(*args)


def _norm_mod(x, g, sc, sh):
    y = x * lax.rsqrt(jnp.mean(x * x, axis=-1, keepdims=True) + RMS_EPS) * g
    return y * (1.0 + sc) + sh


def _normmod_body(x_ref, g_ref, sc_ref, sh_ref, o_ref):
    o_ref[...] = _norm_mod(x_ref[...], g_ref[...], sc_ref[...], sh_ref[...]).astype(o_ref.dtype)


def _mod_spec(k, D):
    return pl.BlockSpec((None, None, 1, D), lambda b, t: (b, k, 0, 0))


def _normmod(x, g, layer, mod, k_scale, k_shift, *, tt, out_dtype, name):
    B, T, D = x.shape
    return pl.pallas_call(
        _normmod_body,
        out_shape=jax.ShapeDtypeStruct((B, T, D), out_dtype),
        grid=(B, T // tt),
        in_specs=[pl.BlockSpec((None, tt, D), lambda b, t: (b, t, 0)),
                  pl.BlockSpec((None, 1, D), lambda b, t: (layer, 0, 0)),
                  _mod_spec(k_scale, D), _mod_spec(k_shift, D)],
        out_specs=pl.BlockSpec((None, tt, D), lambda b, t: (b, t, 0)),
        compiler_params=_params(("parallel", "parallel")),
        name=name,
    )(x, g, mod, mod)


def _na_body(q_ref, k_ref, v_ref, qg_ref, kg_ref, bias_ref, o_ref, qs, ks, *, rows, width, kh, dh):
    def nrm(t, g):
        tf = t.astype(F32)
        return tf * lax.rsqrt(jnp.mean(tf * tf, axis=-1, keepdims=True) + RMS_EPS) * g

    qs[...] = (nrm(q_ref[...], qg_ref[...]) * (dh ** -0.5)).astype(BF16)
    ks[...] = nrm(k_ref[...], kg_ref[...]).astype(BF16)

    def row_group(gi, carry):
        offs, scores, probs = [], [], []
        for u in range(NA_ROW_GROUP):
            r = gi * NA_ROW_GROUP + u
            r0 = jnp.clip(r - kh // 2, 0, rows - kh)
            q_off = pl.multiple_of(r * width, width)
            b_off = pl.multiple_of(r0 * width, width)
            q_r = qs[pl.ds(q_off, width), :]
            k_band = ks[pl.ds(b_off, kh * width), :]
            s = lax.dot_general(q_r, k_band, (((1,), (1,)), ((), ())), preferred_element_type=F32)
            scores.append(s + bias_ref[r - r0])
            offs.append((q_off, b_off))
        for s in scores:
            e = jnp.exp(s - jnp.max(s, axis=-1, keepdims=True))
            probs.append((e.astype(BF16), jnp.sum(e, axis=-1, keepdims=True)))
        for (q_off, b_off), (e, l) in zip(offs, probs):
            v_band = v_ref[pl.ds(b_off, kh * width), :]
            o = jnp.dot(e, v_band, preferred_element_type=F32) / l
            o_ref[pl.ds(q_off, width), :] = o.astype(o_ref.dtype)
        return carry

    lax.fori_loop(0, rows // NA_ROW_GROUP, row_group, 0)


def _na_bias_table(rpb, kh):
    col = jnp.arange(GRID_W)
    col_start = jnp.clip(col - NA_WIN_COLS // 2, 0, GRID_W - NA_WIN_COLS)
    col_in = (col[None, :] >= col_start[:, None]) & (col[None, :] < col_start[:, None] + NA_WIN_COLS)
    dj_idx = jnp.clip(col[None, :] - col[:, None] + NA_WIN_COLS - 1, 0, 2 * NA_WIN_COLS - 2)
    rpb_cols = jnp.where(col_in[None, None], rpb[:, :, dj_idx].astype(F32), MASK_NEG)
    off = jnp.arange(kh)[:, None]
    band = jnp.arange(kh)[None, :]
    di = band - off + NA_WIN_ROWS_MAX - 1
    tab = rpb_cols[:, di]
    H = rpb.shape[0]
    return jnp.transpose(tab, (0, 1, 3, 2, 4)).reshape(H, kh, GRID_W, kh * GRID_W)


def _neighborhood_attention(qkv, q_g, k_g, rpb):
    B, T, D3 = qkv.shape
    D = D3 // 3
    H = NA_HEADS
    dh = D // H
    rows = T // GRID_W
    kh = min(NA_WIN_ROWS_MAX, rows)
    bias = _na_bias_table(rpb, kh)
    body = functools.partial(_na_body, rows=rows, width=GRID_W, kh=kh, dh=dh)
    return pl.pallas_call(
        body,
        out_shape=jax.ShapeDtypeStruct((B, T, D), BF16),
        grid=(B, H),
        in_specs=[pl.BlockSpec((None, T, dh), lambda b, h: (b, 0, h)),
                  pl.BlockSpec((None, T, dh), lambda b, h: (b, 0, H + h)),
                  pl.BlockSpec((None, T, dh), lambda b, h: (b, 0, 2 * H + h)),
                  pl.BlockSpec((1, dh), lambda b, h: (0, 0)),
                  pl.BlockSpec((1, dh), lambda b, h: (0, 0)),
                  pl.BlockSpec((None, kh, GRID_W, kh * GRID_W), lambda b, h: (h, 0, 0, 0))],
        out_specs=pl.BlockSpec((None, T, dh), lambda b, h: (b, 0, h)),
        scratch_shapes=[pltpu.VMEM((T, dh), BF16), pltpu.VMEM((T, dh), BF16)],
        compiler_params=_params(("parallel", "parallel")),
        name="neighborhood_attention",
    )(qkv, qkv, qkv, q_g.reshape(1, dh), k_g.reshape(1, dh), bias)


def _mlstm_pre_body(xm_ref, xp_ref, xn_ref, cw_ref, cb_ref, wq_ref, wk_ref, wv_ref, wif_ref, bif_ref,
                    q_ref, kt_ref, v_ref, xc_ref, pre_ref, *, tt, inner, halo):
    t = pl.program_id(1)
    first = (t > 0).astype(F32)
    last = (t < pl.num_programs(1) - 1).astype(F32)
    pad = ML_CONV_K // 2
    n = tt + 2 * halo
    pre = jnp.zeros((tt, LANES), F32)
    for c in range(inner // LANES):
        sl = slice(c * LANES, (c + 1) * LANES)
        cur_b = xm_ref[:, sl]
        cat = jnp.concatenate([xp_ref[:, sl].astype(F32) * first, cur_b.astype(F32),
                               xn_ref[:, sl].astype(F32) * last], axis=0)
        xc = jnp.zeros((tt, LANES), F32) + cb_ref[:, sl]
        for j in range(ML_CONV_K):
            sh = pltpu.roll(cat, (pad - j) % n, axis=0) if j != pad else cat
            xc = xc + sh[halo:halo + tt] * cw_ref[j:j + 1, sl]
        xc = xc * jax.nn.sigmoid(xc)
        xc_b = xc.astype(BF16)
        q = jnp.dot(xc_b, wq_ref[c], preferred_element_type=F32).astype(BF16)
        k = jnp.dot(xc_b, wk_ref[c], preferred_element_type=F32).astype(BF16)
        v = jnp.dot(cur_b, wv_ref[c], preferred_element_type=F32).astype(BF16)
        pre = pre + jnp.dot(q, wif_ref[0, sl, :], preferred_element_type=F32)
        pre = pre + jnp.dot(k, wif_ref[1, sl, :], preferred_element_type=F32)
        pre = pre + jnp.dot(v, wif_ref[2, sl, :], preferred_element_type=F32)
        q_ref[:, sl] = q
        kt_ref[sl, :] = k.astype(F32).T.astype(BF16)
        v_ref[:, sl] = v
        xc_ref[:, sl] = xc_b
    pre_ref[...] = pre + bif_ref[...]


def _block_diag_dense(w):
    nb = w.shape[0]
    per = LANES // ML_QKV_BLOCK
    wr = w.reshape(nb // per, per, ML_QKV_BLOCK, ML_QKV_BLOCK)
    eye = jnp.eye(per, dtype=w.dtype)
    dense = jnp.einsum('gnio,nm->gnimo', wr, eye)
    return dense.reshape(nb // per, LANES, LANES).astype(BF16)


def _mlstm_pre(xz, conv_w, conv_b, wq, wk, wv, w_if, b_if, *, tt):
    B, T, inner2 = xz.shape
    inner = inner2 // 2
    halo = BF16_SUBLANES
    n_gate = w_if.shape[1] * w_if.shape[2] * w_if.shape[3]
    wif = jnp.pad(w_if.reshape(3, inner, n_gate), ((0, 0), (0, 0), (0, LANES - n_gate))).astype(BF16)
    bif = jnp.pad(b_if.reshape(1, n_gate), ((0, 0), (0, LANES - n_gate)))
    nblk = inner // LANES
    hb = tt // halo
    nh = T // halo
    body = functools.partial(_mlstm_pre_body, tt=tt, inner=inner, halo=halo)
    act = jax.ShapeDtypeStruct((B, T, inner), BF16)
    act_spec = pl.BlockSpec((None, tt, inner), lambda b, t: (b, t, 0))
    const3 = lambda b, t: (0, 0, 0)
    return pl.pallas_call(
        body,
        out_shape=(act, jax.ShapeDtypeStruct((B, inner, T), BF16), act, act,
                   jax.ShapeDtypeStruct((B, T, LANES), F32)),
        grid=(B, T // tt),
        in_specs=[act_spec,
                  pl.BlockSpec((None, halo, inner), lambda b, t: (b, jnp.maximum(t * hb - 1, 0), 0)),
                  pl.BlockSpec((None, halo, inner), lambda b, t: (b, jnp.minimum((t + 1) * hb, nh - 1), 0)),
                  pl.BlockSpec((ML_CONV_K, inner), lambda b, t: (0, 0)),
                  pl.BlockSpec((1, inner), lambda b, t: (0, 0)),
                  pl.BlockSpec((nblk, LANES, LANES), const3),
                  pl.BlockSpec((nblk, LANES, LANES), const3),
                  pl.BlockSpec((nblk, LANES, LANES), const3),
                  pl.BlockSpec((3, inner, LANES), const3),
                  pl.BlockSpec((1, LANES), lambda b, t: (0, 0))],
        out_specs=(act_spec, pl.BlockSpec((None, inner, tt), lambda b, t: (b, 0, t)), act_spec, act_spec,
                   pl.BlockSpec((None, tt, LANES), lambda b, t: (b, t, 0))),
        compiler_params=_params(("parallel", "parallel")),
        name="mlstm_pre",
    )(xz, xz, xz, conv_w, conv_b.reshape(1, inner), _block_diag_dense(wq), _block_diag_dense(wk),
      _block_diag_dense(wv), wif, bif)


def _log_sigmoid(x):
    return jnp.minimum(x, 0.0) - jnp.log(1.0 + jnp.exp(-jnp.abs(x)))


def _mlstm_direction(q, kt, v, li_r, lf_r, lf_c, S, m_ref, o_ref, *, reverse, L, dq):
    scale = dq ** -0.5
    lf_r = _log_sigmoid(lf_r)
    lf_c = _log_sigmoid(lf_c)
    row = lax.broadcasted_iota(jnp.int32, (L, L), 0)
    col = lax.broadcasted_iota(jnp.int32, (L, L), 1)
    vis = (col >= row) if reverse else (col <= row)
    vis_t = (row >= col) if reverse else (row <= col)
    b_c = jnp.sum(jnp.where(vis, lf_r, 0.0), axis=1, keepdims=True)
    b_r = jnp.sum(jnp.where(vis_t, lf_c, 0.0), axis=0, keepdims=True)
    g = jnp.sum(lf_r, axis=1, keepdims=True)
    m = m_ref[...]
    ones_col = (lax.broadcasted_iota(jnp.int32, (L, LANES), 1) == 0).astype(BF16)
    v_aug = jnp.concatenate([v, ones_col], axis=1)

    dlog = jnp.where(vis, b_c - b_r + li_r, -jnp.inf)
    m_inter = b_c + m
    m_t = jnp.maximum(m_inter, jnp.max(dlog, axis=1, keepdims=True))
    s = jnp.dot(q, kt, preferred_element_type=F32) * (jnp.exp(dlog - m_t) * scale)
    inter = jnp.exp(m_inter - m_t)
    out = jnp.dot(s.astype(BF16), v_aug, preferred_element_type=F32)
    out = out + jnp.dot(q, S[...].astype(BF16), preferred_element_type=F32) * inter
    den = out[:, dq:dq + 1]
    o_ref[...] = (out[:, :dq] / jnp.maximum(jnp.abs(den), jnp.exp(-m_t))).astype(o_ref.dtype)

    a = g - b_r + li_r
    m_new = jnp.maximum(g + m, jnp.max(a, axis=1, keepdims=True))
    w = jnp.exp(a - m_new) * scale
    decay = jnp.exp(g + m - m_new)
    S[...] = decay * S[...] + jnp.dot(kt * w.astype(BF16), v_aug, preferred_element_type=F32)
    m_ref[...] = m_new


def _mlstm_body(qf_ref, kf_ref, vf_ref, qb_ref, kb_ref, vb_ref, gr_f_ref, gr_b_ref, gc_f_ref, gc_b_ref,
                of_ref, ob_ref, S, m_ref, *, L, dq, heads):
    h0 = pl.program_id(1) * ML_HEADS_PER_STEP

    @pl.when(pl.program_id(2) == 0)
    def _():
        S[...] = jnp.zeros_like(S)
        m_ref[...] = jnp.zeros_like(m_ref)

    lane = lax.broadcasted_iota(jnp.int32, (L, LANES), 1)
    streams = ((qf_ref, kf_ref, vf_ref, gr_f_ref, gc_f_ref, of_ref),
               (qb_ref, kb_ref, vb_ref, gr_b_ref, gc_b_ref, ob_ref))
    for d, (q_ref, kt_ref, v_ref, gr_ref, gc_ref, o_ref) in enumerate(streams):
        for hh in range(ML_HEADS_PER_STEP):
            sl = slice(hh * dq, (hh + 1) * dq)
            i_idx = d * 2 * heads + h0 + hh
            f_idx = i_idx + heads
            lf_c = jnp.sum(jnp.where(lane == f_idx, gc_ref[...], 0.0), axis=1, keepdims=True)
            _mlstm_direction(q_ref[:, sl], kt_ref[sl, :], v_ref[:, sl],
                             gr_ref[pl.ds(i_idx, 1), :], gr_ref[pl.ds(f_idx, 1), :], lf_c,
                             S.at[d, hh], m_ref.at[d, hh], o_ref.at[:, sl], reverse=bool(d), L=L, dq=dq)


def _mlstm(q, kt, v, pre, *, L):
    B, T, inner = q.shape
    H = ML_HEADS
    dh = inner // H
    n = T // L
    n_gate = 4 * H
    gate_rows = jnp.transpose(pre[:, :, :n_gate], (0, 2, 1))

    fwd = lambda c: c
    bwd = lambda c: n - 1 - c
    hps = ML_HEADS_PER_STEP
    qv_spec = lambda at: pl.BlockSpec((None, L, hps * dh), lambda b, h, c: (b, at(c), h))
    kt_spec = lambda at: pl.BlockSpec((None, hps * dh, L), lambda b, h, c: (b, h, at(c)))
    row_spec = lambda at: pl.BlockSpec((None, n_gate, L), lambda b, h, c: (b, 0, at(c)))
    col_spec = lambda at: pl.BlockSpec((None, L, LANES), lambda b, h, c: (b, at(c), 0))
    out = jax.ShapeDtypeStruct((B, T, inner), BF16)
    body = functools.partial(_mlstm_body, L=L, dq=dh, heads=H)
    return pl.pallas_call(
        body,
        out_shape=(out, out),
        grid=(B, H // hps, n),
        in_specs=[qv_spec(fwd), kt_spec(fwd), qv_spec(fwd), qv_spec(bwd), kt_spec(bwd), qv_spec(bwd),
                  row_spec(fwd), row_spec(bwd), col_spec(fwd), col_spec(bwd)],
        out_specs=(qv_spec(fwd), qv_spec(bwd)),
        scratch_shapes=[pltpu.VMEM((2, hps, dh, dh + LANES), F32), pltpu.VMEM((2, hps, 1, 1), F32)],
        compiler_params=_params(("parallel", "parallel", "arbitrary")),
        name="mlstm_chunkwise",
    )(q, kt, v, q, kt, v, gate_rows, gate_rows, pre, pre)


def _mlstm_post_body(hf_ref, hb_ref, xc_ref, z_ref, ng_ref, skip_ref, o_ref, *, dh, inner):
    for h in range(inner // dh):
        sl = slice(h * dh, (h + 1) * dh)
        ht = hf_ref[:, sl].astype(F32) + hb_ref[:, sl].astype(F32)
        mu = jnp.mean(ht, axis=-1, keepdims=True)
        var = jnp.mean(jnp.square(ht - mu), axis=-1, keepdims=True)
        hn = (ht - mu) * lax.rsqrt(var + LN_EPS) * ng_ref[:, sl]
        out = (hn + skip_ref[:, sl] * xc_ref[:, sl].astype(F32)) * jax.nn.sigmoid(z_ref[:, sl].astype(F32))
        o_ref[:, sl] = out.astype(o_ref.dtype)


def _mlstm_post(hf, hb, xc, xz, norm_g, skip, *, tt):
    B, T, inner = hf.shape
    dh = inner // ML_HEADS
    body = functools.partial(_mlstm_post_body, dh=dh, inner=inner)
    vec = pl.BlockSpec((1, inner), lambda b, t: (0, 0))
    return pl.pallas_call(
        body,
        out_shape=jax.ShapeDtypeStruct((B, T, inner), BF16),
        grid=(B, T // tt),
        in_specs=[pl.BlockSpec((None, tt, inner), lambda b, t: (b, t, 0)),
                  pl.BlockSpec((None, tt, inner), lambda b, t: (b, t, 0)),
                  pl.BlockSpec((None, tt, inner), lambda b, t: (b, t, 0)),
                  pl.BlockSpec((None, tt, inner), lambda b, t: (b, t, 1)),
                  vec, vec],
        out_specs=pl.BlockSpec((None, tt, inner), lambda b, t: (b, t, 0)),
        compiler_params=_params(("parallel", "parallel")),
        name="mlstm_post",
    )(hf, hb, xc, xz, norm_g.reshape(1, inner), skip.reshape(1, inner))


def _pack_bf16_pairs(x):
    half = x.shape[1] // 2
    bits = pltpu.bitcast(x.astype(BF16).astype(F32), jnp.uint32)
    return (bits[:, :half] >> 16) | bits[:, half:]


def _unpack_bf16_pairs(w):
    lo = pltpu.bitcast(w << 16, F32).astype(BF16)
    hi = pltpu.bitcast(w & jnp.uint32(0xFFFF0000), F32).astype(BF16)
    return lo, hi


def _router_body(x_ref, g_ref, sc_ref, sh_ref, r_ref, h_ref, idx_ref, wt_ref, *, n_experts):
    h = _norm_mod(x_ref[...], g_ref[...], sc_ref[...], sh_ref[...])
    h_ref[...] = _pack_bf16_pairs(h)
    logits = jnp.dot(h, r_ref[...], precision=lax.Precision.HIGHEST, preferred_element_type=F32)
    lane = lax.broadcasted_iota(jnp.int32, logits.shape, 1)
    lg = jnp.where(lane < n_experts, logits, -jnp.inf)
    m1 = jnp.max(lg, axis=1, keepdims=True)
    i1 = jnp.min(jnp.where(lg == m1, lane, LANES), axis=1, keepdims=True)
    lg2 = jnp.where(lane == i1, -jnp.inf, lg)
    m2 = jnp.max(lg2, axis=1, keepdims=True)
    i2 = jnp.min(jnp.where(lg2 == m2, lane, LANES), axis=1, keepdims=True)
    e2 = jnp.exp(m2 - m1)
    w1 = 1.0 / (1.0 + e2)
    w2 = e2 * w1
    idx_ref[...] = jnp.where(lane == 0, i1, jnp.where(lane == 1, i2, 0))
    wt_ref[...] = jnp.where(lane == 0, w1, jnp.where(lane == 1, w2, 0.0))


def _router(x, g, layer, mod, router, *, tt):
    B, T, D = x.shape
    E = router.shape[1]
    r_pad = jnp.pad(router, ((0, 0), (0, LANES - E)))
    body = functools.partial(_router_body, n_experts=E)
    small = pl.BlockSpec((None, tt, LANES), lambda b, t: (b, t, 0))
    return pl.pallas_call(
        body,
        out_shape=(jax.ShapeDtypeStruct((B, T, D // 2), jnp.uint32),
                   jax.ShapeDtypeStruct((B, T, LANES), jnp.int32),
                   jax.ShapeDtypeStruct((B, T, LANES), F32)),
        grid=(B, T // tt),
        in_specs=[pl.BlockSpec((None, tt, D), lambda b, t: (b, t, 0)),
                  pl.BlockSpec((None, 1, D), lambda b, t: (layer, 0, 0)),
                  _mod_spec(4, D), _mod_spec(3, D),
                  pl.BlockSpec((D, LANES), lambda b, t: (0, 0))],
        out_specs=(pl.BlockSpec((None, tt, D // 2), lambda b, t: (b, t, 0)), small, small),
        compiler_params=_params(("parallel", "parallel")),
        name="moe_router",
    )(x, g, mod, mod, r_pad)


def _route_tables(top_i, n_experts, n_tiles, n_slots):
    n_pairs = top_i.size
    e_flat = top_i.reshape(n_pairs)
    onehot = (e_flat[:, None] == jnp.arange(n_experts, dtype=jnp.int32)[None, :]).astype(jnp.int32)
    csum = jnp.cumsum(onehot, axis=0)
    counts = csum[-1]
    rank = jnp.sum(csum * onehot, axis=1) - 1
    nsub = (counts + FFN_SUB - 1) // FFN_SUB
    goff = jnp.cumsum(nsub) - nsub
    pos = jnp.sum(onehot * goff[None, :], axis=1) * FFN_SUB + rank
    src = jnp.zeros((n_slots,), jnp.int32).at[pos].set(jnp.arange(n_pairs, dtype=jnp.int32) // TOP_K)

    spr = FFN_ROWS // FFN_SUB
    ntile = (nsub + spr - 1) // spr
    per_tile = (nsub + jnp.maximum(ntile, 1) - 1) // jnp.maximum(ntile, 1)
    tcum = jnp.cumsum(ntile)
    toff = tcum - ntile
    ids = jnp.arange(n_tiles, dtype=jnp.int32)
    te = jnp.minimum(jnp.sum((ids[:, None] >= tcum[None, :]).astype(jnp.int32), axis=1), n_experts - 1)
    valid = ids < tcum[-1]
    j = ids - toff[te]
    t_start = goff[te] + j * per_tile[te]
    t_nsub = jnp.minimum(per_tile[te], nsub[te] - j * per_tile[te])
    last_e = te[jnp.maximum(tcum[-1] - 1, 0)]
    tile_e = jnp.where(valid, te, last_e).astype(jnp.int32)
    tile_start = jnp.where(valid, t_start, 0).astype(jnp.int32)
    tile_nsub = jnp.where(valid, t_nsub, 0).astype(jnp.int32)
    n_used_sub = jnp.sum(nsub).astype(jnp.int32).reshape(1)
    return pos.astype(jnp.int32), src, tile_e, tile_start, tile_nsub, n_used_sub


def _dense_tiles(n_rows):
    nsub = n_rows // FFN_SUB
    spr = FFN_ROWS // FFN_SUB
    ntile = -(-nsub // spr)
    per_tile = -(-nsub // ntile)
    starts = [j * per_tile for j in range(ntile)]
    counts = [min(per_tile, nsub - st) for st in starts]
    return (jnp.zeros((ntile,), jnp.int32), jnp.array(starts, jnp.int32), jnp.array(counts, jnp.int32))


def _dispatch_body(src_ref, h_ref, o_ref, buf, sem, *, chunk):
    i = pl.program_id(0)

    def gather(step, slot):
        base = step * chunk

        def issue(r, carry):
            pltpu.make_async_copy(h_ref.at[pl.ds(src_ref[base + r], 1)], buf.at[slot, pl.ds(r, 1)],
                                  sem.at[slot]).start()
            return carry

        lax.fori_loop(0, chunk, issue, 0, unroll=8)

    @pl.when(i == 0)
    def _():
        gather(0, 0)

    @pl.when(i + 1 < pl.num_programs(0))
    def _():
        gather(i + 1, (i + 1) % 2)

    slot = i % 2
    pltpu.make_async_copy(h_ref.at[pl.ds(0, chunk)], buf.at[slot], sem.at[slot]).wait()
    half = buf.shape[2]
    o_ref[:, :half], o_ref[:, half:] = _unpack_bf16_pairs(buf[slot])


def _dispatch(src, h, n_slots, *, chunk):
    N, half = h.shape
    D = 2 * half
    body = functools.partial(_dispatch_body, chunk=chunk)
    return pl.pallas_call(
        body,
        out_shape=jax.ShapeDtypeStruct((n_slots, D), BF16),
        grid_spec=pltpu.PrefetchScalarGridSpec(
            num_scalar_prefetch=1,
            grid=(n_slots // chunk,),
            in_specs=[pl.BlockSpec(memory_space=pl.ANY)],
            out_specs=pl.BlockSpec((chunk, D), lambda i, src: (i, 0)),
            scratch_shapes=[pltpu.VMEM((2, chunk, half), jnp.uint32), pltpu.SemaphoreType.DMA((2,))]),
        compiler_params=_params(("arbitrary",)),
        name="moe_dispatch",
    )(src, h)


def _expert_body(te_ref, ts_ref, tn_ref, used_ref, xs_ref, wg_ref, wu_ref, wd_ref, y_ref,
                 xb, acc, wgb, wub, wdb, sem_in, sem_out, *, nf):
    s = pl.program_id(0)
    f = pl.program_id(1)
    nsub = tn_ref[s]
    start = ts_ref[s]
    d_model = acc.shape[1]

    def local_rows(j, n=1):
        return pl.ds(pl.multiple_of(j * FFN_SUB, FFN_SUB), n * FFN_SUB)

    def hbm_rows(j):
        return pl.ds(pl.multiple_of((start + j) * FFN_SUB, FFN_SUB), FFN_SUB)

    def in_copy(j):
        return pltpu.make_async_copy(xs_ref.at[hbm_rows(j)], xb.at[local_rows(j)], sem_in)

    def out_copy(j):
        return pltpu.make_async_copy(acc.at[local_rows(j)], y_ref.at[hbm_rows(j)], sem_out)

    def for_each_sub(lo, hi, fn):
        def step(j, carry):
            fn(j)
            return carry

        lax.fori_loop(lo, hi, step, 0)

    def zero_acc(j):
        acc[local_rows(j), :] = jnp.zeros((FFN_SUB, d_model), F32)

    @pl.when((s == 0) & (f == 0))
    def _():
        def fill_copy(j):
            row = pl.multiple_of(j * FFN_SUB, FFN_SUB)
            return pltpu.make_async_copy(acc.at[local_rows(0)], y_ref.at[pl.ds(row, FFN_SUB)], sem_out)

        n_sub_total = y_ref.shape[0] // FFN_SUB
        zero_acc(0)
        for_each_sub(used_ref[0], n_sub_total, lambda j: fill_copy(j).start())
        for_each_sub(used_ref[0], n_sub_total, lambda j: fill_copy(j).wait())

    @pl.when(nsub > 0)
    def _():
        @pl.when(f == 0)
        def _():
            for_each_sub(0, nsub, lambda j: in_copy(j).start())
            for_each_sub(0, nsub, zero_acc)
            for_each_sub(0, nsub, lambda j: in_copy(j).wait())

        def block(j0, n, cast_weights=False):
            if cast_weights:
                wg, wu, wd = (r[...].astype(BF16) for r in (wg_ref, wu_ref, wd_ref))
                wgb[...], wub[...], wdb[...] = wg, wu, wd
            else:
                wg, wu, wd = wgb[...], wub[...], wdb[...]
            rows = local_rows(j0, n)
            x = xb[rows, :]
            hg = jnp.dot(x, wg, preferred_element_type=F32)
            hu = jnp.dot(x, wu, preferred_element_type=F32)
            hid = (hg * jax.nn.sigmoid(hg) * hu).astype(BF16)
            acc[rows, :] += jnp.dot(hid, wd, preferred_element_type=F32)

            @pl.when(f == nf - 1)
            def _():
                for u in range(n):
                    out_copy(j0 + u).start()

        big = FFN_BLOCK_SUBS[0]
        n_big = nsub // big

        @pl.when(n_big > 0)
        def _():
            block(0, big, cast_weights=True)

        for_each_sub(1, n_big, lambda p: block(p * big, big))
        off = n_big * big
        started = n_big > 0
        for size in FFN_BLOCK_SUBS[1:]:
            has = ((nsub - off) // size) > 0

            @pl.when(has & jnp.logical_not(started))
            def _():
                block(0, size, cast_weights=True)

            @pl.when(has & started)
            def _():
                block(off, size)

            off = off + jnp.where(has, size, 0)
            started = started | has

        @pl.when(f == nf - 1)
        def _():
            for_each_sub(0, nsub, lambda j: out_copy(j).wait())


def _grouped_swiglu(tile_e, tile_start, tile_nsub, n_used_sub, xs, w_gate, w_up, w_down):
    P, D = xs.shape
    F = w_gate.shape[2]
    nf = F // FFN_TF
    n_tiles = tile_e.shape[0]

    def f_eff(s, f, tn):
        return jnp.where(tn[s] > 0, f, nf - 1)

    body = functools.partial(_expert_body, nf=nf)
    return pl.pallas_call(
        body,
        out_shape=jax.ShapeDtypeStruct((P, D), F32),
        grid_spec=pltpu.PrefetchScalarGridSpec(
            num_scalar_prefetch=4,
            grid=(n_tiles, nf),
            in_specs=[pl.BlockSpec(memory_space=pl.ANY),
                      pl.BlockSpec((None, D, FFN_TF), lambda s, f, te, ts, tn, used: (te[s], 0, f_eff(s, f, tn))),
                      pl.BlockSpec((None, D, FFN_TF), lambda s, f, te, ts, tn, used: (te[s], 0, f_eff(s, f, tn))),
                      pl.BlockSpec((None, FFN_TF, D), lambda s, f, te, ts, tn, used: (te[s], f_eff(s, f, tn), 0))],
            out_specs=pl.BlockSpec(memory_space=pl.ANY),
            scratch_shapes=[pltpu.VMEM((FFN_ROWS, D), BF16),
                            pltpu.VMEM((FFN_ROWS, D), F32),
                            pltpu.VMEM((D, FFN_TF), BF16),
                            pltpu.VMEM((D, FFN_TF), BF16),
                            pltpu.VMEM((FFN_TF, D), BF16),
                            pltpu.SemaphoreType.DMA,
                            pltpu.SemaphoreType.DMA]),
        compiler_params=_params(("arbitrary", "arbitrary")),
        name="grouped_swiglu",
    )(tile_e, tile_start, tile_nsub, n_used_sub, xs, w_gate, w_up, w_down)


def _combine_body(pos_ref, y_ref, x_ref, w_ref, g_ref, o_ref, ybuf, sem, *, tt):
    i = pl.program_id(0)

    def gather(step, slot):
        base = step * (tt * TOP_K)

        def issue(r, carry):
            for k in range(TOP_K):
                pltpu.make_async_copy(y_ref.at[pl.ds(pos_ref[base + r * TOP_K + k], 1)],
                                      ybuf.at[slot, k, pl.ds(r, 1)], sem.at[slot]).start()
            return carry

        lax.fori_loop(0, tt, issue, 0, unroll=4)

    @pl.when(i == 0)
    def _():
        gather(0, 0)

    @pl.when(i + 1 < pl.num_programs(0))
    def _():
        gather(i + 1, (i + 1) % 2)

    slot = i % 2
    for k in range(TOP_K):
        pltpu.make_async_copy(y_ref.at[pl.ds(0, tt)], ybuf.at[slot, k], sem.at[slot]).wait()
    w = w_ref[...]
    y = w[:, 0:1] * ybuf[slot, 0]
    for k in range(1, TOP_K):
        y = y + w[:, k:k + 1] * ybuf[slot, k]
    o_ref[...] = x_ref[...] + g_ref[...] * y


def _combine(pos, y, x, wts, mod, *, tt):
    N, D = x.shape
    tiles_per_batch = N // mod.shape[0] // tt
    body = functools.partial(_combine_body, tt=tt)
    return pl.pallas_call(
        body,
        out_shape=jax.ShapeDtypeStruct((N, D), F32),
        grid_spec=pltpu.PrefetchScalarGridSpec(
            num_scalar_prefetch=1,
            grid=(N // tt,),
            in_specs=[pl.BlockSpec(memory_space=pl.ANY),
                      pl.BlockSpec((tt, D), lambda i, p: (i, 0)),
                      pl.BlockSpec((tt, LANES), lambda i, p: (i, 0)),
                      pl.BlockSpec((None, None, 1, D), lambda i, p: (i // tiles_per_batch, 5, 0, 0))],
            out_specs=pl.BlockSpec((tt, D), lambda i, p: (i, 0)),
            scratch_shapes=[pltpu.VMEM((2, TOP_K, tt, D), F32), pltpu.SemaphoreType.DMA((2,))]),
        compiler_params=_params(("arbitrary",)),
        name="moe_combine",
    )(pos, y, x, wts, mod)


def _gated_residual_body(x_ref, y_ref, g_ref, o_ref):
    o_ref[...] = x_ref[...] + g_ref[...] * y_ref[...]


def _gated_residual(x, y, mod, k_gate, *, tt):
    B, T, D = x.shape
    spec = pl.BlockSpec((None, tt, D), lambda b, t: (b, t, 0))
    return pl.pallas_call(
        _gated_residual_body,
        out_shape=jax.ShapeDtypeStruct((B, T, D), F32),
        grid=(B, T // tt),
        in_specs=[spec, spec, _mod_spec(k_gate, D)],
        out_specs=spec,
        compiler_params=_params(("parallel", "parallel")),
        name="gated_residual",
    )(x, y, mod)


def kernel(x, c, mod_w, mod_b, norm_mix_g, norm_ffn_g, na_w_in, na_q_norm_g, na_k_norm_g, na_rpb, na_w_out,
           ffn_w_gate, ffn_w_up, ffn_w_down, ml_w_in, ml_conv_w, ml_conv_b, ml_wq, ml_wk, ml_wv, ml_w_if,
           ml_b_if, ml_skip, ml_norm_g, ml_w_out, moe_router, moe_w_gate, moe_w_up, moe_w_down):
    B, T, D = x.shape
    N = B * T
    depth = mod_w.shape[0]
    c_pad = jnp.pad(c, ((0, 8 - B), (0, 0)))
    mix_g = norm_mix_g.reshape(depth, 1, D)
    ffn_g = norm_ffn_g.reshape(depth, 1, D)

    def modulation(i):
        mod = _matmul(c_pad, mod_w, i, tn=1024, tm=8, out_dtype=F32, a_silu=True, bias=mod_b,
                      name="adaln_modulation")
        return mod[:B].reshape(B, 6, 1, D)

    mod = modulation(0)
    h = _normmod(x, mix_g, 0, mod, 1, 0, tt=512, out_dtype=BF16, name="norm_mix0")
    qkv = _matmul(h.reshape(N, D), na_w_in, 0, tn=1024, tm=MM_TM, out_dtype=BF16, name="na_qkv")
    att = _neighborhood_attention(qkv.reshape(B, T, 3 * D), na_q_norm_g[0], na_k_norm_g[0], na_rpb[0])
    x = _matmul(att.reshape(N, D), na_w_out, 0, tn=1024, tm=MM_TM, out_dtype=F32, res=x.reshape(N, D),
                gate=mod, gate_idx=2, rows_per_batch=T, name="na_out").reshape(B, T, D)

    h = _normmod(x, ffn_g, 0, mod, 4, 3, tt=512, out_dtype=BF16, name="norm_ffn0")
    dense_e, dense_start, dense_nsub = _dense_tiles(N)
    y = _grouped_swiglu(dense_e, dense_start, dense_nsub, jnp.full((1,), N // FFN_SUB, jnp.int32),
                        h.reshape(N, D), ffn_w_gate, ffn_w_up, ffn_w_down)
    x = _gated_residual(x, y.reshape(B, T, D), mod, 5, tt=512)

    mod = modulation(1)
    h = _normmod(x, mix_g, 1, mod, 1, 0, tt=512, out_dtype=BF16, name="norm_mix1")
    inner = ml_w_in.shape[2] // 2
    xz = _matmul(h.reshape(N, D), ml_w_in, 0, tn=1024, tm=MM_TM, out_dtype=BF16,
                 name="mlstm_in").reshape(B, T, 2 * inner)
    q, kt, v, xc, pre = _mlstm_pre(xz, ml_conv_w[0], ml_conv_b[0], ml_wq[0], ml_wk[0], ml_wv[0],
                                  ml_w_if[0], ml_b_if[0], tt=256)
    hf, hb = _mlstm(q, kt, v, pre, L=ML_CHUNK)
    u = _mlstm_post(hf, hb, xc, xz, ml_norm_g[0], ml_skip[0], tt=256)
    x = _matmul(u.reshape(N, inner), ml_w_out, 0, tn=512, tm=MM_TM, out_dtype=F32, res=x.reshape(N, D),
                gate=mod, gate_idx=2, rows_per_batch=T, name="mlstm_out").reshape(B, T, D)

    h2, top_idx, top_w = _router(x, ffn_g, 1, mod, moe_router[0], tt=256)
    n_slots = N * TOP_K + N_EXPERTS * FFN_SUB
    n_tiles = -(-N * TOP_K // FFN_ROWS) + N_EXPERTS
    pos, src, tile_e, tile_start, tile_nsub, n_used_sub = _route_tables(
        top_idx.reshape(N, LANES)[:, :TOP_K], N_EXPERTS, n_tiles, n_slots)
    xs = _dispatch(src, h2.reshape(N, D // 2), n_slots, chunk=FFN_SUB)
    ys = _grouped_swiglu(tile_e, tile_start, tile_nsub, n_used_sub, xs, moe_w_gate[0], moe_w_up[0], moe_w_down[0])
    return _combine(pos, ys, x.reshape(N, D), top_w.reshape(N, LANES), mod, tt=256).reshape(B, T, D)
```

```python
import functools

import jax
import jax.numpy as jnp
from jax import lax
from jax.experimental import pallas as pl
from jax.experimental.pallas import tpu as pltpu

F32 = jnp.float32
BF16 = jnp.bfloat16

GRID_W = 64
NA_HEADS = 16
NA_WIN_ROWS_MAX = 8
NA_WIN_COLS = 16
ML_HEADS = 8
ML_QKV_BLOCK = 4
ML_CONV_K = 5
N_EXPERTS = 8
TOP_K = 2
RMS_EPS = 1e-6
LN_EPS = 1e-5

V7X_VMEM_LIMIT_BYTES = 56 * 1024 * 1024
LANES = 128
BF16_SUBLANES = 16

MM_TM = 1024
ML_CHUNK = 256
ML_HEADS_PER_STEP = 2
FFN_SUB = 256
FFN_ROWS = 3072
FFN_BLOCK_SUBS = (4, 2, 1)
FFN_TF = 256
MASK_NEG = -1e30
NA_ROW_GROUP = 8


def _params(sem):
    return pltpu.CompilerParams(dimension_semantics=sem, vmem_limit_bytes=V7X_VMEM_LIMIT_BYTES)


def _mm_body(*refs, a_silu, has_bias, has_res, norm_blocks, head_dim):
    a_ref, w_ref = refs[0], refs[1]
    k = 2
    if has_bias:
        b_ref = refs[k]
        k += 1
    if has_res:
        r_ref, g_ref = refs[k], refs[k + 1]
        k += 2
    if norm_blocks:
        hg_ref = refs[k]
        k += 1
    o_ref, wb_ref = refs[k], refs[k + 1]
    j = pl.program_id(0)
    i = pl.program_id(1)

    def finish(w):
        a = a_ref[...]
        if a_silu:
            a = a * jax.nn.sigmoid(a)
        y = jnp.dot(a.astype(BF16), w, preferred_element_type=F32)
        if has_bias:
            y = y + b_ref[...]
        if has_res:
            y = r_ref[...] + g_ref[...] * y
        if not norm_blocks:
            o_ref[...] = y.astype(o_ref.dtype)
            return

        normed = j < norm_blocks
        for h in range(y.shape[1] // head_dim):
            sl = slice(h * head_dim, (h + 1) * head_dim)
            yh = y[:, sl]
            inv = lax.rsqrt(jnp.mean(yh * yh, axis=-1, keepdims=True) + RMS_EPS)
            o_ref[:, sl] = (yh * (jnp.where(normed, inv, 1.0) * hg_ref[:, sl])).astype(o_ref.dtype)

    @pl.when(i == 0)
    def _():
        w = w_ref[...].astype(BF16)
        wb_ref[...] = w
        finish(w)

    @pl.when(i > 0)
    def _():
        finish(wb_ref[...])


def _matmul(a, w, layer, *, tn, tm, out_dtype, a_silu=False, bias=None, res=None, gate=None,
            gate_idx=0, rows_per_batch=None, head_gain=None, norm_cols=0, head_dim=None, name="matmul"):
    M, K = a.shape
    N = w.shape[2]
    grid = (N // tn, M // tm)
    in_specs = [pl.BlockSpec((tm, K), lambda j, i: (i, 0)),
                pl.BlockSpec((None, K, tn), lambda j, i: (layer, 0, j))]
    args = [a, w]
    if bias is not None:
        in_specs.append(pl.BlockSpec((None, 1, tn), lambda j, i: (layer, 0, j)))
        args.append(bias.reshape(bias.shape[0], 1, N))
    if res is not None:
        tiles_per_batch = rows_per_batch // tm
        in_specs.append(pl.BlockSpec((tm, tn), lambda j, i: (i, j)))
        in_specs.append(pl.BlockSpec((None, None, 1, tn),
                                     lambda j, i: (i // tiles_per_batch, gate_idx, 0, j)))
        args += [res, gate]
    if head_gain is not None:
        in_specs.append(pl.BlockSpec((1, tn), lambda j, i: (0, j)))
        args.append(head_gain)
    body = functools.partial(_mm_body, a_silu=a_silu, has_bias=bias is not None,
                             has_res=res is not None, norm_blocks=norm_cols // tn, head_dim=head_dim)
    return pl.pallas_call(
        body,
        out_shape=jax.ShapeDtypeStruct((M, N), out_dtype),
        grid=grid,
        in_specs=in_specs,
        out_specs=pl.BlockSpec((tm, tn), lambda j, i: (i, j)),
        scratch_shapes=[pltpu.VMEM((K, tn), BF16)],
        compiler_params=_params(("parallel", "arbitrary")),
        name=name,
    )(*args)


def _norm_mod(x, g, sc, sh):
    y = x * lax.rsqrt(jnp.mean(x * x, axis=-1, keepdims=True) + RMS_EPS) * g
    return y * (1.0 + sc) + sh


def _normmod_body(x_ref, g_ref, sc_ref, sh_ref, o_ref):
    o_ref[...] = _norm_mod(x_ref[...], g_ref[...], sc_ref[...], sh_ref[...]).astype(o_ref.dtype)


def _mod_spec(k, D):
    return pl.BlockSpec((None, None, 1, D), lambda b, t: (b, k, 0, 0))


def _normmod(x, g, layer, mod, k_scale, k_shift, *, tt, out_dtype, name):
    B, T, D = x.shape
    return pl.pallas_call(
        _normmod_body,
        out_shape=jax.ShapeDtypeStruct((B, T, D), out_dtype),
        grid=(B, T // tt),
        in_specs=[pl.BlockSpec((None, tt, D), lambda b, t: (b, t, 0)),
                  pl.BlockSpec((None, 1, D), lambda b, t: (layer, 0, 0)),
                  _mod_spec(k_scale, D), _mod_spec(k_shift, D)],
        out_specs=pl.BlockSpec((None, tt, D), lambda b, t: (b, t, 0)),
        compiler_params=_params(("parallel", "parallel")),
        name=name,
    )(x, g, mod, mod)


def _na_body(q_ref, k_ref, v_ref, bias_ref, o_ref, *, rows, width, kh):
    def row_group(gi, carry):
        offs, scores, probs = [], [], []
        for u in range(NA_ROW_GROUP):
            r = gi * NA_ROW_GROUP + u
            r0 = jnp.clip(r - kh // 2, 0, rows - kh)
            q_off = pl.multiple_of(r * width, width)
            b_off = pl.multiple_of(r0 * width, width)
            q_r = q_ref[pl.ds(q_off, width), :]
            k_band = k_ref[pl.ds(b_off, kh * width), :]
            s = lax.dot_general(q_r, k_band, (((1,), (1,)), ((), ())), preferred_element_type=F32)
            scores.append(s + bias_ref[r - r0])
            offs.append((q_off, b_off))
        for s in scores:
            e = jnp.exp(s - jnp.max(s, axis=-1, keepdims=True))
            probs.append((e.astype(BF16), jnp.sum(e, axis=-1, keepdims=True)))
        for (q_off, b_off), (e, l) in zip(offs, probs):
            v_band = v_ref[pl.ds(b_off, kh * width), :]
            o = jnp.dot(e, v_band, preferred_element_type=F32) / l
            o_ref[pl.ds(q_off, width), :] = o.astype(o_ref.dtype)
        return carry

    lax.fori_loop(0, rows // NA_ROW_GROUP, row_group, 0)


def _na_bias_table(rpb, kh):
    col = jnp.arange(GRID_W)
    col_start = jnp.clip(col - NA_WIN_COLS // 2, 0, GRID_W - NA_WIN_COLS)
    col_in = (col[None, :] >= col_start[:, None]) & (col[None, :] < col_start[:, None] + NA_WIN_COLS)
    dj_idx = jnp.clip(col[None, :] - col[:, None] + NA_WIN_COLS - 1, 0, 2 * NA_WIN_COLS - 2)
    rpb_cols = jnp.where(col_in[None, None], rpb[:, :, dj_idx].astype(F32), MASK_NEG)
    top = NA_WIN_ROWS_MAX - 1
    tab = jnp.stack([rpb_cols[:, top - off:top - off + kh] for off in range(kh)], axis=1)
    H = rpb.shape[0]
    return jnp.transpose(tab, (0, 1, 3, 2, 4)).reshape(H, kh, GRID_W, kh * GRID_W)


def _neighborhood_attention(qkv, rpb):
    B, T, D3 = qkv.shape
    D = D3 // 3
    H = NA_HEADS
    dh = D // H
    rows = T // GRID_W
    kh = min(NA_WIN_ROWS_MAX, rows)
    bias = _na_bias_table(rpb, kh)
    body = functools.partial(_na_body, rows=rows, width=GRID_W, kh=kh)
    return pl.pallas_call(
        body,
        out_shape=jax.ShapeDtypeStruct((B, T, D), BF16),
        grid=(B, H),
        in_specs=[pl.BlockSpec((None, T, dh), lambda b, h: (b, 0, h)),
                  pl.BlockSpec((None, T, dh), lambda b, h: (b, 0, H + h)),
                  pl.BlockSpec((None, T, dh), lambda b, h: (b, 0, 2 * H + h)),
                  pl.BlockSpec((None, kh, GRID_W, kh * GRID_W), lambda b, h: (h, 0, 0, 0))],
        out_specs=pl.BlockSpec((None, T, dh), lambda b, h: (b, 0, h)),
        compiler_params=_params(("parallel", "parallel")),
        name="neighborhood_attention",
    )(qkv, qkv, qkv, bias)


def _mlstm_pre_body(xm_ref, xp_ref, xn_ref, cw_ref, cb_ref, wq_ref, wk_ref, wv_ref, wif_ref, bif_ref,
                    q_ref, kt_ref, v_ref, xc_ref, pre_ref, *, tt, inner, halo):
    t = pl.program_id(1)
    first = (t > 0).astype(F32)
    last = (t < pl.num_programs(1) - 1).astype(F32)
    pad = ML_CONV_K // 2
    n = tt + 2 * halo
    pre = jnp.zeros((tt, LANES), F32)
    for c in range(inner // LANES):
        sl = slice(c * LANES, (c + 1) * LANES)
        cur_b = xm_ref[:, sl]
        cat = jnp.concatenate([xp_ref[:, sl].astype(F32) * first, cur_b.astype(F32),
                               xn_ref[:, sl].astype(F32) * last], axis=0)
        xc = jnp.zeros((tt, LANES), F32) + cb_ref[:, sl]
        for j in range(ML_CONV_K):
            sh = pltpu.roll(cat, (pad - j) % n, axis=0) if j != pad else cat
            xc = xc + sh[halo:halo + tt] * cw_ref[j:j + 1, sl]
        xc = xc * jax.nn.sigmoid(xc)
        xc_b = xc.astype(BF16)
        q = jnp.dot(xc_b, wq_ref[c], preferred_element_type=F32).astype(BF16)
        k = jnp.dot(xc_b, wk_ref[c], preferred_element_type=F32).astype(BF16)
        v = jnp.dot(cur_b, wv_ref[c], preferred_element_type=F32).astype(BF16)
        pre = pre + jnp.dot(q, wif_ref[0, sl, :], preferred_element_type=F32)
        pre = pre + jnp.dot(k, wif_ref[1, sl, :], preferred_element_type=F32)
        pre = pre + jnp.dot(v, wif_ref[2, sl, :], preferred_element_type=F32)
        q_ref[:, sl] = q
        kt_ref[sl, :] = k.astype(F32).T.astype(BF16)
        v_ref[:, sl] = v
        xc_ref[:, sl] = xc_b
    pre_ref[...] = pre + bif_ref[...]


def _block_diag_dense(w):
    nb = w.shape[0]
    per = LANES // ML_QKV_BLOCK
    wr = w.reshape(nb // per, per, ML_QKV_BLOCK, ML_QKV_BLOCK)
    eye = jnp.eye(per, dtype=w.dtype)
    dense = jnp.einsum('gnio,nm->gnimo', wr, eye)
    return dense.reshape(nb // per, LANES, LANES).astype(BF16)


def _mlstm_pre(xz, conv_w, conv_b, wq, wk, wv, w_if, b_if, *, tt):
    B, T, inner2 = xz.shape
    inner = inner2 // 2
    halo = BF16_SUBLANES
    n_gate = w_if.shape[1] * w_if.shape[2] * w_if.shape[3]
    wif = jnp.pad(w_if.reshape(3, inner, n_gate), ((0, 0), (0, 0), (0, LANES - n_gate))).astype(BF16)
    bif = jnp.pad(b_if.reshape(1, n_gate), ((0, 0), (0, LANES - n_gate)))
    nblk = inner // LANES
    hb = tt // halo
    nh = T // halo
    body = functools.partial(_mlstm_pre_body, tt=tt, inner=inner, halo=halo)
    act = jax.ShapeDtypeStruct((B, T, inner), BF16)
    act_spec = pl.BlockSpec((None, tt, inner), lambda b, t: (b, t, 0))
    const3 = lambda b, t: (0, 0, 0)
    return pl.pallas_call(
        body,
        out_shape=(act, jax.ShapeDtypeStruct((B, inner, T), BF16), act, act,
                   jax.ShapeDtypeStruct((B, T, LANES), F32)),
        grid=(B, T // tt),
        in_specs=[act_spec,
                  pl.BlockSpec((None, halo, inner), lambda b, t: (b, jnp.maximum(t * hb - 1, 0), 0)),
                  pl.BlockSpec((None, halo, inner), lambda b, t: (b, jnp.minimum((t + 1) * hb, nh - 1), 0)),
                  pl.BlockSpec((ML_CONV_K, inner), lambda b, t: (0, 0)),
                  pl.BlockSpec((1, inner), lambda b, t: (0, 0)),
                  pl.BlockSpec((nblk, LANES, LANES), const3),
                  pl.BlockSpec((nblk, LANES, LANES), const3),
                  pl.BlockSpec((nblk, LANES, LANES), const3),
                  pl.BlockSpec((3, inner, LANES), const3),
                  pl.BlockSpec((1, LANES), lambda b, t: (0, 0))],
        out_specs=(act_spec, pl.BlockSpec((None, inner, tt), lambda b, t: (b, 0, t)), act_spec, act_spec,
                   pl.BlockSpec((None, tt, LANES), lambda b, t: (b, t, 0))),
        compiler_params=_params(("parallel", "parallel")),
        name="mlstm_pre",
    )(xz, xz, xz, conv_w, conv_b.reshape(1, inner), _block_diag_dense(wq), _block_diag_dense(wk),
      _block_diag_dense(wv), wif, bif)


def _log_sigmoid(x):
    return jnp.minimum(x, 0.0) - jnp.log(1.0 + jnp.exp(-jnp.abs(x)))


def _mlstm_direction(q, kt, v, li_r, lf_r, lf_c, S, m_ref, o_ref, *, reverse, L, dq):
    scale = dq ** -0.5
    lf_r = _log_sigmoid(lf_r)
    lf_c = _log_sigmoid(lf_c)
    row = lax.broadcasted_iota(jnp.int32, (L, L), 0)
    col = lax.broadcasted_iota(jnp.int32, (L, L), 1)
    vis = (col >= row) if reverse else (col <= row)
    vis_t = (row >= col) if reverse else (row <= col)
    b_c = jnp.sum(jnp.where(vis, lf_r, 0.0), axis=1, keepdims=True)
    b_r = jnp.sum(jnp.where(vis_t, lf_c, 0.0), axis=0, keepdims=True)
    g = jnp.sum(lf_r, axis=1, keepdims=True)
    m = m_ref[...]
    ones_col = (lax.broadcasted_iota(jnp.int32, (L, LANES), 1) == 0).astype(BF16)
    v_aug = jnp.concatenate([v, ones_col], axis=1)

    dlog = jnp.where(vis, b_c - b_r + li_r, -jnp.inf)
    m_inter = b_c + m
    m_t = jnp.maximum(m_inter, jnp.max(dlog, axis=1, keepdims=True))
    s = jnp.dot(q, kt, preferred_element_type=F32) * (jnp.exp(dlog - m_t) * scale)
    inter = jnp.exp(m_inter - m_t)
    out = jnp.dot(s.astype(BF16), v_aug, preferred_element_type=F32)
    out = out + jnp.dot(q, S[...].astype(BF16), preferred_element_type=F32) * inter
    den = out[:, dq:dq + 1]
    o_ref[...] = (out[:, :dq] / jnp.maximum(jnp.abs(den), jnp.exp(-m_t))).astype(o_ref.dtype)

    a = g - b_r + li_r
    m_new = jnp.maximum(g + m, jnp.max(a, axis=1, keepdims=True))
    w = jnp.exp(a - m_new) * scale
    decay = jnp.exp(g + m - m_new)
    S[...] = decay * S[...] + jnp.dot(kt * w.astype(BF16), v_aug, preferred_element_type=F32)
    m_ref[...] = m_new


def _mlstm_body(qf_ref, kf_ref, vf_ref, qb_ref, kb_ref, vb_ref, gr_f_ref, gr_b_ref, gc_f_ref, gc_b_ref,
                of_ref, ob_ref, S, m_ref, *, L, dq, heads):
    h0 = pl.program_id(1) * ML_HEADS_PER_STEP

    @pl.when(pl.program_id(2) == 0)
    def _():
        S[...] = jnp.zeros_like(S)
        m_ref[...] = jnp.zeros_like(m_ref)

    lane = lax.broadcasted_iota(jnp.int32, (L, LANES), 1)
    streams = ((qf_ref, kf_ref, vf_ref, gr_f_ref, gc_f_ref, of_ref),
               (qb_ref, kb_ref, vb_ref, gr_b_ref, gc_b_ref, ob_ref))
    for d, (q_ref, kt_ref, v_ref, gr_ref, gc_ref, o_ref) in enumerate(streams):
        for hh in range(ML_HEADS_PER_STEP):
            sl = slice(hh * dq, (hh + 1) * dq)
            i_idx = d * 2 * heads + h0 + hh
            f_idx = i_idx + heads
            lf_c = jnp.sum(jnp.where(lane == f_idx, gc_ref[...], 0.0), axis=1, keepdims=True)
            _mlstm_direction(q_ref[:, sl], kt_ref[sl, :], v_ref[:, sl],
                             gr_ref[pl.ds(i_idx, 1), :], gr_ref[pl.ds(f_idx, 1), :], lf_c,
                             S.at[d, hh], m_ref.at[d, hh], o_ref.at[:, sl], reverse=bool(d), L=L, dq=dq)


def _mlstm(q, kt, v, pre, *, L):
    B, T, inner = q.shape
    H = ML_HEADS
    dh = inner // H
    n = T // L
    n_gate = 4 * H
    gate_rows = jnp.transpose(pre[:, :, :n_gate], (0, 2, 1))

    fwd = lambda c: c
    bwd = lambda c: n - 1 - c
    hps = ML_HEADS_PER_STEP
    qv_spec = lambda at: pl.BlockSpec((None, L, hps * dh), lambda b, h, c: (b, at(c), h))
    kt_spec = lambda at: pl.BlockSpec((None, hps * dh, L), lambda b, h, c: (b, h, at(c)))
    row_spec = lambda at: pl.BlockSpec((None, n_gate, L), lambda b, h, c: (b, 0, at(c)))
    col_spec = lambda at: pl.BlockSpec((None, L, LANES), lambda b, h, c: (b, at(c), 0))
    out = jax.ShapeDtypeStruct((B, T, inner), BF16)
    body = functools.partial(_mlstm_body, L=L, dq=dh, heads=H)
    return pl.pallas_call(
        body,
        out_shape=(out, out),
        grid=(B, H // hps, n),
        in_specs=[qv_spec(fwd), kt_spec(fwd), qv_spec(fwd), qv_spec(bwd), kt_spec(bwd), qv_spec(bwd),
                  row_spec(fwd), row_spec(bwd), col_spec(fwd), col_spec(bwd)],
        out_specs=(qv_spec(fwd), qv_spec(bwd)),
        scratch_shapes=[pltpu.VMEM((2, hps, dh, dh + LANES), F32), pltpu.VMEM((2, hps, 1, 1), F32)],
        compiler_params=_params(("parallel", "parallel", "arbitrary")),
        name="mlstm_chunkwise",
    )(q, kt, v, q, kt, v, gate_rows, gate_rows, pre, pre)


def _mlstm_post_body(hf_ref, hb_ref, xc_ref, z_ref, ng_ref, skip_ref, o_ref, *, dh, inner):
    for h in range(inner // dh):
        sl = slice(h * dh, (h + 1) * dh)
        ht = hf_ref[:, sl].astype(F32) + hb_ref[:, sl].astype(F32)
        mu = jnp.mean(ht, axis=-1, keepdims=True)
        var = jnp.mean(jnp.square(ht - mu), axis=-1, keepdims=True)
        hn = (ht - mu) * lax.rsqrt(var + LN_EPS) * ng_ref[:, sl]
        out = (hn + skip_ref[:, sl] * xc_ref[:, sl].astype(F32)) * jax.nn.sigmoid(z_ref[:, sl].astype(F32))
        o_ref[:, sl] = out.astype(o_ref.dtype)


def _mlstm_post(hf, hb, xc, xz, norm_g, skip, *, tt):
    B, T, inner = hf.shape
    dh = inner // ML_HEADS
    body = functools.partial(_mlstm_post_body, dh=dh, inner=inner)
    vec = pl.BlockSpec((1, inner), lambda b, t: (0, 0))
    return pl.pallas_call(
        body,
        out_shape=jax.ShapeDtypeStruct((B, T, inner), BF16),
        grid=(B, T // tt),
        in_specs=[pl.BlockSpec((None, tt, inner), lambda b, t: (b, t, 0)),
                  pl.BlockSpec((None, tt, inner), lambda b, t: (b, t, 0)),
                  pl.BlockSpec((None, tt, inner), lambda b, t: (b, t, 0)),
                  pl.BlockSpec((None, tt, inner), lambda b, t: (b, t, 1)),
                  vec, vec],
        out_specs=pl.BlockSpec((None, tt, inner), lambda b, t: (b, t, 0)),
        compiler_params=_params(("parallel", "parallel")),
        name="mlstm_post",
    )(hf, hb, xc, xz, norm_g.reshape(1, inner), skip.reshape(1, inner))


def _pack_bf16_pairs(x):
    half = x.shape[1] // 2
    bits = pltpu.bitcast(x.astype(BF16).astype(F32), jnp.uint32)
    return (bits[:, :half] >> 16) | bits[:, half:]


def _unpack_bf16_pairs(w):
    lo = pltpu.bitcast(w << 16, F32).astype(BF16)
    hi = pltpu.bitcast(w & jnp.uint32(0xFFFF0000), F32).astype(BF16)
    return lo, hi


def _router_body(x_ref, g_ref, sc_ref, sh_ref, r_ref, h_ref, idx_ref, wt_ref, *, n_experts):
    h = _norm_mod(x_ref[...], g_ref[...], sc_ref[...], sh_ref[...])
    h_ref[...] = _pack_bf16_pairs(h)
    logits = jnp.dot(h, r_ref[...], precision=lax.Precision.HIGHEST, preferred_element_type=F32)
    lane = lax.broadcasted_iota(jnp.int32, logits.shape, 1)
    lg = jnp.where(lane < n_experts, logits, -jnp.inf)
    m1 = jnp.max(lg, axis=1, keepdims=True)
    i1 = jnp.min(jnp.where(lg == m1, lane, LANES), axis=1, keepdims=True)
    lg2 = jnp.where(lane == i1, -jnp.inf, lg)
    m2 = jnp.max(lg2, axis=1, keepdims=True)
    i2 = jnp.min(jnp.where(lg2 == m2, lane, LANES), axis=1, keepdims=True)
    e2 = jnp.exp(m2 - m1)
    w1 = 1.0 / (1.0 + e2)
    w2 = e2 * w1
    idx_ref[...] = jnp.where(lane == 0, i1, jnp.where(lane == 1, i2, 0))
    wt_ref[...] = jnp.where(lane == 0, w1, jnp.where(lane == 1, w2, 0.0))


def _router(x, g, layer, mod, router, *, tt):
    B, T, D = x.shape
    E = router.shape[1]
    r_pad = jnp.pad(router, ((0, 0), (0, LANES - E)))
    body = functools.partial(_router_body, n_experts=E)
    small = pl.BlockSpec((None, tt, LANES), lambda b, t: (b, t, 0))
    return pl.pallas_call(
        body,
        out_shape=(jax.ShapeDtypeStruct((B, T, D // 2), jnp.uint32),
                   jax.ShapeDtypeStruct((B, T, LANES), jnp.int32),
                   jax.ShapeDtypeStruct((B, T, LANES), F32)),
        grid=(B, T // tt),
        in_specs=[pl.BlockSpec((None, tt, D), lambda b, t: (b, t, 0)),
                  pl.BlockSpec((None, 1, D), lambda b, t: (layer, 0, 0)),
                  _mod_spec(4, D), _mod_spec(3, D),
                  pl.BlockSpec((D, LANES), lambda b, t: (0, 0))],
        out_specs=(pl.BlockSpec((None, tt, D // 2), lambda b, t: (b, t, 0)), small, small),
        compiler_params=_params(("parallel", "parallel")),
        name="moe_router",
    )(x, g, mod, mod, r_pad)


def _route_tables(top_i, n_experts, n_tiles, n_slots):
    n_pairs = top_i.size
    e_flat = top_i.reshape(n_pairs)
    onehot = (e_flat[:, None] == jnp.arange(n_experts, dtype=jnp.int32)[None, :]).astype(jnp.int32)
    csum = jnp.cumsum(onehot, axis=0)
    counts = csum[-1]
    rank = jnp.sum(csum * onehot, axis=1) - 1
    nsub = (counts + FFN_SUB - 1) // FFN_SUB
    goff = jnp.cumsum(nsub) - nsub
    pos = jnp.sum(onehot * goff[None, :], axis=1) * FFN_SUB + rank
    src = jnp.zeros((n_slots,), jnp.int32).at[pos].set(jnp.arange(n_pairs, dtype=jnp.int32) // TOP_K)

    spr = FFN_ROWS // FFN_SUB
    ntile = (nsub + spr - 1) // spr
    per_tile = (nsub + jnp.maximum(ntile, 1) - 1) // jnp.maximum(ntile, 1)
    tcum = jnp.cumsum(ntile)
    toff = tcum - ntile
    ids = jnp.arange(n_tiles, dtype=jnp.int32)
    te = jnp.minimum(jnp.sum((ids[:, None] >= tcum[None, :]).astype(jnp.int32), axis=1), n_experts - 1)
    valid = ids < tcum[-1]
    j = ids - toff[te]
    t_start = goff[te] + j * per_tile[te]
    t_nsub = jnp.minimum(per_tile[te], nsub[te] - j * per_tile[te])
    last_e = te[jnp.maximum(tcum[-1] - 1, 0)]
    tile_e = jnp.where(valid, te, last_e).astype(jnp.int32)
    tile_start = jnp.where(valid, t_start, 0).astype(jnp.int32)
    tile_nsub = jnp.where(valid, t_nsub, 0).astype(jnp.int32)
    n_used_sub = jnp.sum(nsub).astype(jnp.int32).reshape(1)
    return pos.astype(jnp.int32), src, tile_e, tile_start, tile_nsub, n_used_sub


def _dense_tiles(n_rows):
    nsub = n_rows // FFN_SUB
    spr = FFN_ROWS // FFN_SUB
    ntile = -(-nsub // spr)
    per_tile = -(-nsub // ntile)
    starts = [j * per_tile for j in range(ntile)]
    counts = [min(per_tile, nsub - st) for st in starts]
    return (jnp.zeros((ntile,), jnp.int32), jnp.array(starts, jnp.int32), jnp.array(counts, jnp.int32))


def _dispatch_body(src_ref, h_ref, o_ref, buf, sem, *, chunk):
    i = pl.program_id(0)

    def gather(step, slot):
        base = step * chunk

        def issue(r, carry):
            pltpu.make_async_copy(h_ref.at[pl.ds(src_ref[base + r], 1)], buf.at[slot, pl.ds(r, 1)],
                                  sem.at[slot]).start()
            return carry

        lax.fori_loop(0, chunk, issue, 0, unroll=8)

    @pl.when(i == 0)
    def _():
        gather(0, 0)

    @pl.when(i + 1 < pl.num_programs(0))
    def _():
        gather(i + 1, (i + 1) % 2)

    slot = i % 2
    pltpu.make_async_copy(h_ref.at[pl.ds(0, chunk)], buf.at[slot], sem.at[slot]).wait()
    half = buf.shape[2]
    o_ref[:, :half], o_ref[:, half:] = _unpack_bf16_pairs(buf[slot])


def _dispatch(src, h, n_slots, *, chunk):
    N, half = h.shape
    D = 2 * half
    body = functools.partial(_dispatch_body, chunk=chunk)
    return pl.pallas_call(
        body,
        out_shape=jax.ShapeDtypeStruct((n_slots, D), BF16),
        grid_spec=pltpu.PrefetchScalarGridSpec(
            num_scalar_prefetch=1,
            grid=(n_slots // chunk,),
            in_specs=[pl.BlockSpec(memory_space=pl.ANY)],
            out_specs=pl.BlockSpec((chunk, D), lambda i, src: (i, 0)),
            scratch_shapes=[pltpu.VMEM((2, chunk, half), jnp.uint32), pltpu.SemaphoreType.DMA((2,))]),
        compiler_params=_params(("arbitrary",)),
        name="moe_dispatch",
    )(src, h)


def _expert_body(te_ref, ts_ref, tn_ref, used_ref, xs_ref, wg_ref, wu_ref, wd_ref, y_ref,
                 xb, acc, wgb, wub, wdb, sem_in, sem_out, *, nf):
    s = pl.program_id(0)
    f = pl.program_id(1)
    nsub = tn_ref[s]
    start = ts_ref[s]
    d_model = acc.shape[1]

    def local_rows(j, n=1):
        return pl.ds(pl.multiple_of(j * FFN_SUB, FFN_SUB), n * FFN_SUB)

    def hbm_rows(j):
        return pl.ds(pl.multiple_of((start + j) * FFN_SUB, FFN_SUB), FFN_SUB)

    def in_copy(j):
        return pltpu.make_async_copy(xs_ref.at[hbm_rows(j)], xb.at[local_rows(j)], sem_in)

    def out_copy(j):
        return pltpu.make_async_copy(acc.at[local_rows(j)], y_ref.at[hbm_rows(j)], sem_out)

    def for_each_sub(lo, hi, fn):
        def step(j, carry):
            fn(j)
            return carry

        lax.fori_loop(lo, hi, step, 0)

    def zero_acc(j):
        acc[local_rows(j), :] = jnp.zeros((FFN_SUB, d_model), F32)

    @pl.when((s == 0) & (f == 0))
    def _():
        def fill_copy(j):
            row = pl.multiple_of(j * FFN_SUB, FFN_SUB)
            return pltpu.make_async_copy(acc.at[local_rows(0)], y_ref.at[pl.ds(row, FFN_SUB)], sem_out)

        n_sub_total = y_ref.shape[0] // FFN_SUB
        zero_acc(0)
        for_each_sub(used_ref[0], n_sub_total, lambda j: fill_copy(j).start())
        for_each_sub(used_ref[0], n_sub_total, lambda j: fill_copy(j).wait())

    @pl.when(nsub > 0)
    def _():
        @pl.when(f == 0)
        def _():
            for_each_sub(0, nsub, lambda j: in_copy(j).start())
            for_each_sub(0, nsub, zero_acc)
            for_each_sub(0, nsub, lambda j: in_copy(j).wait())

        def block(j0, n, cast_weights=False):
            if cast_weights:
                wg, wu, wd = (r[...].astype(BF16) for r in (wg_ref, wu_ref, wd_ref))
                wgb[...], wub[...], wdb[...] = wg, wu, wd
            else:
                wg, wu, wd = wgb[...], wub[...], wdb[...]
            rows = local_rows(j0, n)
            x = xb[rows, :]
            hg = jnp.dot(x, wg, preferred_element_type=F32)
            hu = jnp.dot(x, wu, preferred_element_type=F32)
            hid = (hg * jax.nn.sigmoid(hg) * hu).astype(BF16)
            acc[rows, :] += jnp.dot(hid, wd, preferred_element_type=F32)

            @pl.when(f == nf - 1)
            def _():
                for u in range(n):
                    out_copy(j0 + u).start()

        big = FFN_BLOCK_SUBS[0]
        n_big = nsub // big

        @pl.when(n_big > 0)
        def _():
            block(0, big, cast_weights=True)

        for_each_sub(1, n_big, lambda p: block(p * big, big))
        off = n_big * big
        started = n_big > 0
        for size in FFN_BLOCK_SUBS[1:]:
            has = ((nsub - off) // size) > 0

            @pl.when(has & jnp.logical_not(started))
            def _():
                block(0, size, cast_weights=True)

            @pl.when(has & started)
            def _():
                block(off, size)

            off = off + jnp.where(has, size, 0)
            started = started | has

        @pl.when(f == nf - 1)
        def _():
            for_each_sub(0, nsub, lambda j: out_copy(j).wait())


def _grouped_swiglu(tile_e, tile_start, tile_nsub, n_used_sub, xs, w_gate, w_up, w_down):
    P, D = xs.shape
    F = w_gate.shape[2]
    nf = F // FFN_TF
    n_tiles = tile_e.shape[0]

    def f_eff(s, f, tn):
        return jnp.where(tn[s] > 0, f, nf - 1)

    body = functools.partial(_expert_body, nf=nf)
    return pl.pallas_call(
        body,
        out_shape=jax.ShapeDtypeStruct((P, D), F32),
        grid_spec=pltpu.PrefetchScalarGridSpec(
            num_scalar_prefetch=4,
            grid=(n_tiles, nf),
            in_specs=[pl.BlockSpec(memory_space=pl.ANY),
                      pl.BlockSpec((None, D, FFN_TF), lambda s, f, te, ts, tn, used: (te[s], 0, f_eff(s, f, tn))),
                      pl.BlockSpec((None, D, FFN_TF), lambda s, f, te, ts, tn, used: (te[s], 0, f_eff(s, f, tn))),
                      pl.BlockSpec((None, FFN_TF, D), lambda s, f, te, ts, tn, used: (te[s], f_eff(s, f, tn), 0))],
            out_specs=pl.BlockSpec(memory_space=pl.ANY),
            scratch_shapes=[pltpu.VMEM((FFN_ROWS, D), BF16),
                            pltpu.VMEM((FFN_ROWS, D), F32),
                            pltpu.VMEM((D, FFN_TF), BF16),
                            pltpu.VMEM((D, FFN_TF), BF16),
                            pltpu.VMEM((FFN_TF, D), BF16),
                            pltpu.SemaphoreType.DMA,
                            pltpu.SemaphoreType.DMA]),
        compiler_params=_params(("arbitrary", "arbitrary")),
        name="grouped_swiglu",
    )(tile_e, tile_start, tile_nsub, n_used_sub, xs, w_gate, w_up, w_down)


def _combine_body(pos_ref, y_ref, x_ref, w_ref, g_ref, o_ref, ybuf, sem, *, tt):
    i = pl.program_id(0)

    def gather(step, slot):
        base = step * (tt * TOP_K)

        def issue(r, carry):
            for k in range(TOP_K):
                pltpu.make_async_copy(y_ref.at[pl.ds(pos_ref[base + r * TOP_K + k], 1)],
                                      ybuf.at[slot, k, pl.ds(r, 1)], sem.at[slot]).start()
            return carry

        lax.fori_loop(0, tt, issue, 0, unroll=4)

    @pl.when(i == 0)
    def _():
        gather(0, 0)

    @pl.when(i + 1 < pl.num_programs(0))
    def _():
        gather(i + 1, (i + 1) % 2)

    slot = i % 2
    for k in range(TOP_K):
        pltpu.make_async_copy(y_ref.at[pl.ds(0, tt)], ybuf.at[slot, k], sem.at[slot]).wait()
    w = w_ref[...]
    y = w[:, 0:1] * ybuf[slot, 0]
    for k in range(1, TOP_K):
        y = y + w[:, k:k + 1] * ybuf[slot, k]
    o_ref[...] = x_ref[...] + g_ref[...] * y


def _combine(pos, y, x, wts, mod, *, tt):
    N, D = x.shape
    tiles_per_batch = N // mod.shape[0] // tt
    body = functools.partial(_combine_body, tt=tt)
    return pl.pallas_call(
        body,
        out_shape=jax.ShapeDtypeStruct((N, D), F32),
        grid_spec=pltpu.PrefetchScalarGridSpec(
            num_scalar_prefetch=1,
            grid=(N // tt,),
            in_specs=[pl.BlockSpec(memory_space=pl.ANY),
                      pl.BlockSpec((tt, D), lambda i, p: (i, 0)),
                      pl.BlockSpec((tt, LANES), lambda i, p: (i, 0)),
                      pl.BlockSpec((None, None, 1, D), lambda i, p: (i // tiles_per_batch, 5, 0, 0))],
            out_specs=pl.BlockSpec((tt, D), lambda i, p: (i, 0)),
            scratch_shapes=[pltpu.VMEM((2, TOP_K, tt, D), F32), pltpu.SemaphoreType.DMA((2,))]),
        compiler_params=_params(("arbitrary",)),
        name="moe_combine",
    )(pos, y, x, wts, mod)


def _residual_normmod_body(x_ref, y_ref, gate_ref, g_ref, sc_ref, sh_ref, x_out_ref, h_ref):
    x_new = x_ref[...] + gate_ref[...] * y_ref[...]
    x_out_ref[...] = x_new
    h_ref[...] = _norm_mod(x_new, g_ref[...], sc_ref[...], sh_ref[...]).astype(h_ref.dtype)


def _residual_normmod(x, y, mod_prev, k_gate, g, layer, mod_next, k_scale, k_shift, *, tt, name):
    B, T, D = x.shape
    spec = pl.BlockSpec((None, tt, D), lambda b, t: (b, t, 0))
    return pl.pallas_call(
        _residual_normmod_body,
        out_shape=(jax.ShapeDtypeStruct((B, T, D), F32), jax.ShapeDtypeStruct((B, T, D), BF16)),
        grid=(B, T // tt),
        in_specs=[spec, spec, _mod_spec(k_gate, D),
                  pl.BlockSpec((None, 1, D), lambda b, t: (layer, 0, 0)),
                  _mod_spec(k_scale, D), _mod_spec(k_shift, D)],
        out_specs=(spec, spec),
        compiler_params=_params(("parallel", "parallel")),
        name=name,
    )(x, y, mod_prev, g, mod_next, mod_next)


def kernel(x, c, mod_w, mod_b, norm_mix_g, norm_ffn_g, na_w_in, na_q_norm_g, na_k_norm_g, na_rpb, na_w_out,
           ffn_w_gate, ffn_w_up, ffn_w_down, ml_w_in, ml_conv_w, ml_conv_b, ml_wq, ml_wk, ml_wv, ml_w_if,
           ml_b_if, ml_skip, ml_norm_g, ml_w_out, moe_router, moe_w_gate, moe_w_up, moe_w_down):
    B, T, D = x.shape
    N = B * T
    depth = mod_w.shape[0]
    c_pad = jnp.pad(c, ((0, 8 - B), (0, 0)))
    mix_g = norm_mix_g.reshape(depth, 1, D)
    ffn_g = norm_ffn_g.reshape(depth, 1, D)

    def modulation(i):
        mod = _matmul(c_pad, mod_w, i, tn=1024, tm=8, out_dtype=F32, a_silu=True, bias=mod_b,
                      name="adaln_modulation")
        return mod[:B].reshape(B, 6, 1, D)

    mod = modulation(0)
    h = _normmod(x, mix_g, 0, mod, 1, 0, tt=512, out_dtype=BF16, name="norm_mix0")
    na_dh = D // NA_HEADS
    qk_gain = jnp.concatenate([jnp.tile(na_q_norm_g[0] * na_dh ** -0.5, NA_HEADS), jnp.tile(na_k_norm_g[0], NA_HEADS),
                               jnp.ones((D,), F32)]).reshape(1, 3 * D)
    qkv = _matmul(h.reshape(N, D), na_w_in, 0, tn=1024, tm=MM_TM, out_dtype=BF16, head_gain=qk_gain,
                  norm_cols=2 * D, head_dim=na_dh, name="na_qkv")
    att = _neighborhood_attention(qkv.reshape(B, T, 3 * D), na_rpb[0])
    x = _matmul(att.reshape(N, D), na_w_out, 0, tn=1024, tm=MM_TM, out_dtype=F32, res=x.reshape(N, D),
                gate=mod, gate_idx=2, rows_per_batch=T, name="na_out").reshape(B, T, D)

    h = _normmod(x, ffn_g, 0, mod, 4, 3, tt=512, out_dtype=BF16, name="norm_ffn0")
    dense_e, dense_start, dense_nsub = _dense_tiles(N)
    y = _grouped_swiglu(dense_e, dense_start, dense_nsub, jnp.full((1,), N // FFN_SUB, jnp.int32),
                        h.reshape(N, D), ffn_w_gate, ffn_w_up, ffn_w_down)

    mod0, mod = mod, modulation(1)
    x, h = _residual_normmod(x, y.reshape(B, T, D), mod0, 5, mix_g, 1, mod, 1, 0, tt=512,
                             name="ffn0_residual_norm_mix1")
    inner = ml_w_in.shape[2] // 2
    xz = _matmul(h.reshape(N, D), ml_w_in, 0, tn=1024, tm=MM_TM, out_dtype=BF16,
                 name="mlstm_in").reshape(B, T, 2 * inner)
    q, kt, v, xc, pre = _mlstm_pre(xz, ml_conv_w[0], ml_conv_b[0], ml_wq[0], ml_wk[0], ml_wv[0],
                                  ml_w_if[0], ml_b_if[0], tt=256)
    hf, hb = _mlstm(q, kt, v, pre, L=ML_CHUNK)
    u = _mlstm_post(hf, hb, xc, xz, ml_norm_g[0], ml_skip[0], tt=256)
    x = _matmul(u.reshape(N, inner), ml_w_out, 0, tn=512, tm=MM_TM, out_dtype=F32, res=x.reshape(N, D),
                gate=mod, gate_idx=2, rows_per_batch=T, name="mlstm_out").reshape(B, T, D)

    h2, top_idx, top_w = _router(x, ffn_g, 1, mod, moe_router[0], tt=256)
    n_slots = N * TOP_K + N_EXPERTS * FFN_SUB
    n_tiles = -(-N * TOP_K // FFN_ROWS) + N_EXPERTS
    pos, src, tile_e, tile_start, tile_nsub, n_used_sub = _route_tables(
        top_idx.reshape(N, LANES)[:, :TOP_K], N_EXPERTS, n_tiles, n_slots)
    xs = _dispatch(src, h2.reshape(N, D // 2), n_slots, chunk=FFN_SUB)
    ys = _grouped_swiglu(tile_e, tile_start, tile_nsub, n_used_sub, xs, moe_w_gate[0], moe_w_up[0], moe_w_down[0])
    return _combine(pos, ys, x.reshape(N, D), top_w.reshape(N, LANES), mod, tt=256).reshape(B, T, D)
```

```python
import functools

import jax
import jax.numpy as jnp
from jax import lax
from jax.experimental import pallas as pl
from jax.experimental.pallas import tpu as pltpu

F32 = jnp.float32
BF16 = jnp.bfloat16

GRID_W = 64
NA_HEADS = 16
NA_WIN_ROWS_MAX = 8
NA_WIN_COLS = 16
ML_HEADS = 8
ML_QKV_BLOCK = 4
ML_CONV_K = 5
N_EXPERTS = 8
TOP_K = 2
RMS_EPS = 1e-6
LN_EPS = 1e-5

V7X_VMEM_LIMIT_BYTES = 56 * 1024 * 1024
LANES = 128
BF16_SUBLANES = 16

MM_TM = 1024
ML_CHUNK = 256
ML_HEADS_PER_STEP = 2
FFN_SUB = 256
FFN_ROWS = 3072
FFN_BLOCK_SUBS = (4, 2, 1)
FFN_TF = 256
MASK_NEG = -1e30
NA_ROW_GROUP = 8


def _params(sem):
    return pltpu.CompilerParams(dimension_semantics=sem, vmem_limit_bytes=V7X_VMEM_LIMIT_BYTES)


def _mm_body(*refs, a_silu, has_bias, has_res, norm_blocks, head_dim):
    a_ref, w_ref = refs[0], refs[1]
    k = 2
    if has_bias:
        b_ref = refs[k]
        k += 1
    if has_res:
        r_ref, g_ref = refs[k], refs[k + 1]
        k += 2
    if norm_blocks:
        hg_ref = refs[k]
        k += 1
    o_ref, wb_ref = refs[k], refs[k + 1]
    j = pl.program_id(0)
    i = pl.program_id(1)

    def finish(w):
        a = a_ref[...]
        if a_silu:
            a = a * jax.nn.sigmoid(a)
        y = jnp.dot(a.astype(BF16), w, preferred_element_type=F32)
        if has_bias:
            y = y + b_ref[...]
        if has_res:
            y = r_ref[...] + g_ref[...] * y
        if not norm_blocks:
            o_ref[...] = y.astype(o_ref.dtype)
            return

        normed = j < norm_blocks
        for h in range(y.shape[1] // head_dim):
            sl = slice(h * head_dim, (h + 1) * head_dim)
            yh = y[:, sl]
            inv = lax.rsqrt(jnp.mean(yh * yh, axis=-1, keepdims=True) + RMS_EPS)
            o_ref[:, sl] = (yh * (jnp.where(normed, inv, 1.0) * hg_ref[:, sl])).astype(o_ref.dtype)

    @pl.when(i == 0)
    def _():
        w = w_ref[...].astype(BF16)
        wb_ref[...] = w
        finish(w)

    @pl.when(i > 0)
    def _():
        finish(wb_ref[...])


def _matmul(a, w, layer, *, tn, tm, out_dtype, a_silu=False, bias=None, res=None, gate=None,
            gate_idx=0, rows_per_batch=None, head_gain=None, norm_cols=0, head_dim=None, name="matmul"):
    M, K = a.shape
    N = w.shape[2]
    grid = (N // tn, M // tm)
    in_specs = [pl.BlockSpec((tm, K), lambda j, i: (i, 0)),
                pl.BlockSpec((None, K, tn), lambda j, i: (layer, 0, j))]
    args = [a, w]
    if bias is not None:
        in_specs.append(pl.BlockSpec((None, 1, tn), lambda j, i: (layer, 0, j)))
        args.append(bias.reshape(bias.shape[0], 1, N))
    if res is not None:
        tiles_per_batch = rows_per_batch // tm
        in_specs.append(pl.BlockSpec((tm, tn), lambda j, i: (i, j)))
        in_specs.append(pl.BlockSpec((None, None, 1, tn),
                                     lambda j, i: (i // tiles_per_batch, gate_idx, 0, j)))
        args += [res, gate]
    if head_gain is not None:
        in_specs.append(pl.BlockSpec((1, tn), lambda j, i: (0, j)))
        args.append(head_gain)
    body = functools.partial(_mm_body, a_silu=a_silu, has_bias=bias is not None,
                             has_res=res is not None, norm_blocks=norm_cols // tn, head_dim=head_dim)
    return pl.pallas_call(
        body,
        out_shape=jax.ShapeDtypeStruct((M, N), out_dtype),
        grid=grid,
        in_specs=in_specs,
        out_specs=pl.BlockSpec((tm, tn), lambda j, i: (i, j)),
        scratch_shapes=[pltpu.VMEM((K, tn), BF16)],
        compiler_params=_params(("parallel", "arbitrary")),
        name=name,
    )(*args)


def _norm_mod(x, g, sc, sh):
    y = x * lax.rsqrt(jnp.mean(x * x, axis=-1, keepdims=True) + RMS_EPS) * g
    return y * (1.0 + sc) + sh


def _normmod_body(x_ref, g_ref, sc_ref, sh_ref, o_ref):
    o_ref[...] = _norm_mod(x_ref[...], g_ref[...], sc_ref[...], sh_ref[...]).astype(o_ref.dtype)


def _mod_spec(k, D):
    return pl.BlockSpec((None, None, 1, D), lambda b, t: (b, k, 0, 0))


def _normmod(x, g, layer, mod, k_scale, k_shift, *, tt, out_dtype, name):
    B, T, D = x.shape
    return pl.pallas_call(
        _normmod_body,
        out_shape=jax.ShapeDtypeStruct((B, T, D), out_dtype),
        grid=(B, T // tt),
        in_specs=[pl.BlockSpec((None, tt, D), lambda b, t: (b, t, 0)),
                  pl.BlockSpec((None, 1, D), lambda b, t: (layer, 0, 0)),
                  _mod_spec(k_scale, D), _mod_spec(k_shift, D)],
        out_specs=pl.BlockSpec((None, tt, D), lambda b, t: (b, t, 0)),
        compiler_params=_params(("parallel", "parallel")),
        name=name,
    )(x, g, mod, mod)


def _na_body(q_ref, k_ref, v_ref, bias_ref, o_ref, *, rows, width, kh):
    def row_group(gi, carry):
        offs, scores, probs = [], [], []
        for u in range(NA_ROW_GROUP):
            r = gi * NA_ROW_GROUP + u
            r0 = jnp.clip(r - kh // 2, 0, rows - kh)
            q_off = pl.multiple_of(r * width, width)
            b_off = pl.multiple_of(r0 * width, width)
            q_r = q_ref[pl.ds(q_off, width), :]
            k_band = k_ref[pl.ds(b_off, kh * width), :]
            s = lax.dot_general(q_r, k_band, (((1,), (1,)), ((), ())), preferred_element_type=F32)
            scores.append(s + bias_ref[r - r0])
            offs.append((q_off, b_off))
        for s in scores:
            e = jnp.exp(s - jnp.max(s, axis=-1, keepdims=True))
            probs.append((e.astype(BF16), jnp.sum(e, axis=-1, keepdims=True)))
        for (q_off, b_off), (e, l) in zip(offs, probs):
            v_band = v_ref[pl.ds(b_off, kh * width), :]
            o = jnp.dot(e, v_band, preferred_element_type=F32) / l
            o_ref[pl.ds(q_off, width), :] = o.astype(o_ref.dtype)
        return carry

    lax.fori_loop(0, rows // NA_ROW_GROUP, row_group, 0)


def _na_bias_table(rpb, kh):
    col = jnp.arange(GRID_W)
    col_start = jnp.clip(col - NA_WIN_COLS // 2, 0, GRID_W - NA_WIN_COLS)
    col_in = (col[None, :] >= col_start[:, None]) & (col[None, :] < col_start[:, None] + NA_WIN_COLS)
    H, n_di, n_dj = rpb.shape
    lo = GRID_W - NA_WIN_COLS
    ext = jnp.concatenate([jnp.repeat(rpb[..., :1], lo, axis=-1), rpb.astype(F32),
                           jnp.repeat(rpb[..., -1:], 2 * GRID_W - 1 - lo - n_dj, axis=-1)], axis=-1)
    skew = jnp.tile(ext, (1, 1, GRID_W + 1))[..., :GRID_W * 2 * GRID_W].reshape(H, n_di, GRID_W, 2 * GRID_W)
    toep = skew[:, :, ::-1, :GRID_W]
    rpb_cols = jnp.where(col_in[None, None], toep, MASK_NEG)
    top = NA_WIN_ROWS_MAX - 1
    tab = jnp.stack([rpb_cols[:, top - off:top - off + kh] for off in range(kh)], axis=1)
    H = rpb.shape[0]
    return jnp.transpose(tab, (0, 1, 3, 2, 4)).reshape(H, kh, GRID_W, kh * GRID_W)


def _neighborhood_attention(qkv, rpb):
    B, T, D3 = qkv.shape
    D = D3 // 3
    H = NA_HEADS
    dh = D // H
    rows = T // GRID_W
    kh = min(NA_WIN_ROWS_MAX, rows)
    bias = _na_bias_table(rpb, kh)
    body = functools.partial(_na_body, rows=rows, width=GRID_W, kh=kh)
    return pl.pallas_call(
        body,
        out_shape=jax.ShapeDtypeStruct((B, T, D), BF16),
        grid=(B, H),
        in_specs=[pl.BlockSpec((None, T, dh), lambda b, h: (b, 0, h)),
                  pl.BlockSpec((None, T, dh), lambda b, h: (b, 0, H + h)),
                  pl.BlockSpec((None, T, dh), lambda b, h: (b, 0, 2 * H + h)),
                  pl.BlockSpec((None, kh, GRID_W, kh * GRID_W), lambda b, h: (h, 0, 0, 0))],
        out_specs=pl.BlockSpec((None, T, dh), lambda b, h: (b, 0, h)),
        compiler_params=_params(("parallel", "parallel")),
        name="neighborhood_attention",
    )(qkv, qkv, qkv, bias)


def _mlstm_pre_body(xm_ref, xp_ref, xn_ref, cw_ref, cb_ref, wq_ref, wk_ref, wv_ref, wif_ref, bif_ref,
                    q_ref, kt_ref, v_ref, xc_ref, pre_ref, *, tt, inner, halo):
    t = pl.program_id(1)
    first = (t > 0).astype(F32)
    last = (t < pl.num_programs(1) - 1).astype(F32)
    pad = ML_CONV_K // 2
    n = tt + 2 * halo
    pre = jnp.zeros((tt, LANES), F32)
    for c in range(inner // LANES):
        sl = slice(c * LANES, (c + 1) * LANES)
        cur_b = xm_ref[:, sl]
        cat = jnp.concatenate([xp_ref[:, sl].astype(F32) * first, cur_b.astype(F32),
                               xn_ref[:, sl].astype(F32) * last], axis=0)
        xc = jnp.zeros((tt, LANES), F32) + cb_ref[:, sl]
        for j in range(ML_CONV_K):
            sh = pltpu.roll(cat, (pad - j) % n, axis=0) if j != pad else cat
            xc = xc + sh[halo:halo + tt] * cw_ref[j:j + 1, sl]
        xc = xc * jax.nn.sigmoid(xc)
        xc_b = xc.astype(BF16)
        q = jnp.dot(xc_b, wq_ref[c], preferred_element_type=F32).astype(BF16)
        k = jnp.dot(xc_b, wk_ref[c], preferred_element_type=F32).astype(BF16)
        v = jnp.dot(cur_b, wv_ref[c], preferred_element_type=F32).astype(BF16)
        pre = pre + jnp.dot(q, wif_ref[0, sl, :], preferred_element_type=F32)
        pre = pre + jnp.dot(k, wif_ref[1, sl, :], preferred_element_type=F32)
        pre = pre + jnp.dot(v, wif_ref[2, sl, :], preferred_element_type=F32)
        q_ref[:, sl] = q
        kt_ref[sl, :] = k.astype(F32).T.astype(BF16)
        v_ref[:, sl] = v
        xc_ref[:, sl] = xc_b
    pre_ref[...] = pre + bif_ref[...]


def _block_diag_dense(w):
    nb = w.shape[0]
    per = LANES // ML_QKV_BLOCK
    wr = w.reshape(nb // per, per, ML_QKV_BLOCK, ML_QKV_BLOCK)
    eye = jnp.eye(per, dtype=w.dtype)
    dense = jnp.einsum('gnio,nm->gnimo', wr, eye)
    return dense.reshape(nb // per, LANES, LANES).astype(BF16)


def _mlstm_pre(xz, conv_w, conv_b, wq, wk, wv, w_if, b_if, *, tt):
    B, T, inner2 = xz.shape
    inner = inner2 // 2
    halo = BF16_SUBLANES
    n_gate = w_if.shape[1] * w_if.shape[2] * w_if.shape[3]
    wif = jnp.pad(w_if.reshape(3, inner, n_gate), ((0, 0), (0, 0), (0, LANES - n_gate))).astype(BF16)
    bif = jnp.pad(b_if.reshape(1, n_gate), ((0, 0), (0, LANES - n_gate)))
    nblk = inner // LANES
    hb = tt // halo
    nh = T // halo
    body = functools.partial(_mlstm_pre_body, tt=tt, inner=inner, halo=halo)
    act = jax.ShapeDtypeStruct((B, T, inner), BF16)
    act_spec = pl.BlockSpec((None, tt, inner), lambda b, t: (b, t, 0))
    const3 = lambda b, t: (0, 0, 0)
    return pl.pallas_call(
        body,
        out_shape=(act, jax.ShapeDtypeStruct((B, inner, T), BF16), act, act,
                   jax.ShapeDtypeStruct((B, T, LANES), F32)),
        grid=(B, T // tt),
        in_specs=[act_spec,
                  pl.BlockSpec((None, halo, inner), lambda b, t: (b, jnp.maximum(t * hb - 1, 0), 0)),
                  pl.BlockSpec((None, halo, inner), lambda b, t: (b, jnp.minimum((t + 1) * hb, nh - 1), 0)),
                  pl.BlockSpec((ML_CONV_K, inner), lambda b, t: (0, 0)),
                  pl.BlockSpec((1, inner), lambda b, t: (0, 0)),
                  pl.BlockSpec((nblk, LANES, LANES), const3),
                  pl.BlockSpec((nblk, LANES, LANES), const3),
                  pl.BlockSpec((nblk, LANES, LANES), const3),
                  pl.BlockSpec((3, inner, LANES), const3),
                  pl.BlockSpec((1, LANES), lambda b, t: (0, 0))],
        out_specs=(act_spec, pl.BlockSpec((None, inner, tt), lambda b, t: (b, 0, t)), act_spec, act_spec,
                   pl.BlockSpec((None, tt, LANES), lambda b, t: (b, t, 0))),
        compiler_params=_params(("parallel", "parallel")),
        name="mlstm_pre",
    )(xz, xz, xz, conv_w, conv_b.reshape(1, inner), _block_diag_dense(wq), _block_diag_dense(wk),
      _block_diag_dense(wv), wif, bif)


def _log_sigmoid(x):
    return jnp.minimum(x, 0.0) - jnp.log(1.0 + jnp.exp(-jnp.abs(x)))


def _mlstm_direction(q, kt, v, li_r, lf_r, lf_c, S, m_ref, o_ref, *, reverse, L, dq):
    scale = dq ** -0.5
    lf_r = _log_sigmoid(lf_r)
    lf_c = _log_sigmoid(lf_c)
    row = lax.broadcasted_iota(jnp.int32, (L, L), 0)
    col = lax.broadcasted_iota(jnp.int32, (L, L), 1)
    vis = (col >= row) if reverse else (col <= row)
    vis_t = (row >= col) if reverse else (row <= col)
    b_c = jnp.sum(jnp.where(vis, lf_r, 0.0), axis=1, keepdims=True)
    b_r = jnp.sum(jnp.where(vis_t, lf_c, 0.0), axis=0, keepdims=True)
    g = jnp.sum(lf_r, axis=1, keepdims=True)
    m = m_ref[...]
    ones_col = (lax.broadcasted_iota(jnp.int32, (L, LANES), 1) == 0).astype(BF16)
    v_aug = jnp.concatenate([v, ones_col], axis=1)

    dlog = jnp.where(vis, b_c - b_r + li_r, -jnp.inf)
    m_inter = b_c + m
    m_t = jnp.maximum(m_inter, jnp.max(dlog, axis=1, keepdims=True))
    s = jnp.dot(q, kt, preferred_element_type=F32) * (jnp.exp(dlog - m_t) * scale)
    inter = jnp.exp(m_inter - m_t)
    out = jnp.dot(s.astype(BF16), v_aug, preferred_element_type=F32)
    out = out + jnp.dot(q, S[...].astype(BF16), preferred_element_type=F32) * inter
    den = out[:, dq:dq + 1]
    o_ref[...] = (out[:, :dq] / jnp.maximum(jnp.abs(den), jnp.exp(-m_t))).astype(o_ref.dtype)

    a = g - b_r + li_r
    m_new = jnp.maximum(g + m, jnp.max(a, axis=1, keepdims=True))
    w = jnp.exp(a - m_new) * scale
    decay = jnp.exp(g + m - m_new)
    S[...] = decay * S[...] + jnp.dot(kt * w.astype(BF16), v_aug, preferred_element_type=F32)
    m_ref[...] = m_new


def _mlstm_body(qf_ref, kf_ref, vf_ref, qb_ref, kb_ref, vb_ref, gr_f_ref, gr_b_ref, gc_f_ref, gc_b_ref,
                of_ref, ob_ref, S, m_ref, *, L, dq, heads):
    h0 = pl.program_id(1) * ML_HEADS_PER_STEP

    @pl.when(pl.program_id(2) == 0)
    def _():
        S[...] = jnp.zeros_like(S)
        m_ref[...] = jnp.zeros_like(m_ref)

    lane = lax.broadcasted_iota(jnp.int32, (L, LANES), 1)
    streams = ((qf_ref, kf_ref, vf_ref, gr_f_ref, gc_f_ref, of_ref),
               (qb_ref, kb_ref, vb_ref, gr_b_ref, gc_b_ref, ob_ref))
    for d, (q_ref, kt_ref, v_ref, gr_ref, gc_ref, o_ref) in enumerate(streams):
        for hh in range(ML_HEADS_PER_STEP):
            sl = slice(hh * dq, (hh + 1) * dq)
            i_idx = d * 2 * heads + h0 + hh
            f_idx = i_idx + heads
            lf_c = jnp.sum(jnp.where(lane == f_idx, gc_ref[...], 0.0), axis=1, keepdims=True)
            _mlstm_direction(q_ref[:, sl], kt_ref[sl, :], v_ref[:, sl],
                             gr_ref[pl.ds(i_idx, 1), :], gr_ref[pl.ds(f_idx, 1), :], lf_c,
                             S.at[d, hh], m_ref.at[d, hh], o_ref.at[:, sl], reverse=bool(d), L=L, dq=dq)


def _mlstm(q, kt, v, pre, *, L):
    B, T, inner = q.shape
    H = ML_HEADS
    dh = inner // H
    n = T // L
    n_gate = 4 * H
    gate_rows = jnp.transpose(pre[:, :, :n_gate], (0, 2, 1))

    fwd = lambda c: c
    bwd = lambda c: n - 1 - c
    hps = ML_HEADS_PER_STEP
    qv_spec = lambda at: pl.BlockSpec((None, L, hps * dh), lambda b, h, c: (b, at(c), h))
    kt_spec = lambda at: pl.BlockSpec((None, hps * dh, L), lambda b, h, c: (b, h, at(c)))
    row_spec = lambda at: pl.BlockSpec((None, n_gate, L), lambda b, h, c: (b, 0, at(c)))
    col_spec = lambda at: pl.BlockSpec((None, L, LANES), lambda b, h, c: (b, at(c), 0))
    out = jax.ShapeDtypeStruct((B, T, inner), BF16)
    body = functools.partial(_mlstm_body, L=L, dq=dh, heads=H)
    return pl.pallas_call(
        body,
        out_shape=(out, out),
        grid=(B, H // hps, n),
        in_specs=[qv_spec(fwd), kt_spec(fwd), qv_spec(fwd), qv_spec(bwd), kt_spec(bwd), qv_spec(bwd),
                  row_spec(fwd), row_spec(bwd), col_spec(fwd), col_spec(bwd)],
        out_specs=(qv_spec(fwd), qv_spec(bwd)),
        scratch_shapes=[pltpu.VMEM((2, hps, dh, dh + LANES), F32), pltpu.VMEM((2, hps, 1, 1), F32)],
        compiler_params=_params(("parallel", "parallel", "arbitrary")),
        name="mlstm_chunkwise",
    )(q, kt, v, q, kt, v, gate_rows, gate_rows, pre, pre)


def _mlstm_post_body(hf_ref, hb_ref, xc_ref, z_ref, ng_ref, skip_ref, o_ref, *, dh, inner):
    for h in range(inner // dh):
        sl = slice(h * dh, (h + 1) * dh)
        ht = hf_ref[:, sl].astype(F32) + hb_ref[:, sl].astype(F32)
        mu = jnp.mean(ht, axis=-1, keepdims=True)
        var = jnp.mean(jnp.square(ht - mu), axis=-1, keepdims=True)
        hn = (ht - mu) * lax.rsqrt(var + LN_EPS) * ng_ref[:, sl]
        out = (hn + skip_ref[:, sl] * xc_ref[:, sl].astype(F32)) * jax.nn.sigmoid(z_ref[:, sl].astype(F32))
        o_ref[:, sl] = out.astype(o_ref.dtype)


def _mlstm_post(hf, hb, xc, xz, norm_g, skip, *, tt):
    B, T, inner = hf.shape
    dh = inner // ML_HEADS
    body = functools.partial(_mlstm_post_body, dh=dh, inner=inner)
    vec = pl.BlockSpec((1, inner), lambda b, t: (0, 0))
    return pl.pallas_call(
        body,
        out_shape=jax.ShapeDtypeStruct((B, T, inner), BF16),
        grid=(B, T // tt),
        in_specs=[pl.BlockSpec((None, tt, inner), lambda b, t: (b, t, 0)),
                  pl.BlockSpec((None, tt, inner), lambda b, t: (b, t, 0)),
                  pl.BlockSpec((None, tt, inner), lambda b, t: (b, t, 0)),
                  pl.BlockSpec((None, tt, inner), lambda b, t: (b, t, 1)),
                  vec, vec],
        out_specs=pl.BlockSpec((None, tt, inner), lambda b, t: (b, t, 0)),
        compiler_params=_params(("parallel", "parallel")),
        name="mlstm_post",
    )(hf, hb, xc, xz, norm_g.reshape(1, inner), skip.reshape(1, inner))


def _pack_bf16_pairs(x):
    half = x.shape[1] // 2
    bits = pltpu.bitcast(x.astype(BF16).astype(F32), jnp.uint32)
    return (bits[:, :half] >> 16) | bits[:, half:]


def _unpack_bf16_pairs(w):
    lo = pltpu.bitcast(w << 16, F32).astype(BF16)
    hi = pltpu.bitcast(w & jnp.uint32(0xFFFF0000), F32).astype(BF16)
    return lo, hi


def _router_body(x_ref, g_ref, sc_ref, sh_ref, r_ref, h_ref, idx_ref, wt_ref, *, n_experts):
    h = _norm_mod(x_ref[...], g_ref[...], sc_ref[...], sh_ref[...])
    packed = _pack_bf16_pairs(h)
    n_lane_blocks = packed.shape[1] // LANES
    for c in range(n_lane_blocks):
        h_ref[pl.ds(c, packed.shape[0], stride=n_lane_blocks), :] = packed[:, c * LANES:(c + 1) * LANES]
    r = r_ref[...]
    h_hi, r_hi = h.astype(BF16), r.astype(BF16)
    h_lo = (h - h_hi.astype(F32)).astype(BF16)
    r_lo = (r - r_hi.astype(F32)).astype(BF16)
    logits = (jnp.dot(h_hi, r_hi, preferred_element_type=F32) + jnp.dot(h_lo, r_hi, preferred_element_type=F32)
              + jnp.dot(h_hi, r_lo, preferred_element_type=F32))
    lane = lax.broadcasted_iota(jnp.int32, logits.shape, 1)
    lg = jnp.where(lane < n_experts, logits, -jnp.inf)
    m1 = jnp.max(lg, axis=1, keepdims=True)
    i1 = jnp.min(jnp.where(lg == m1, lane, LANES), axis=1, keepdims=True)
    lg2 = jnp.where(lane == i1, -jnp.inf, lg)
    m2 = jnp.max(lg2, axis=1, keepdims=True)
    i2 = jnp.min(jnp.where(lg2 == m2, lane, LANES), axis=1, keepdims=True)
    e2 = jnp.exp(m2 - m1)
    w1 = 1.0 / (1.0 + e2)
    w2 = e2 * w1
    choices = jnp.where(lane == 0, i1, jnp.where(lane == 1, i2, 0)).astype(F32)
    idx_ref[...] = jnp.transpose(choices)[:idx_ref.shape[0], :].astype(jnp.int32)
    wt_ref[...] = jnp.where(lane == 0, w1, jnp.where(lane == 1, w2, 0.0))


def _router(x, g, layer, mod, router, *, tt):
    B, T, D = x.shape
    E = router.shape[1]
    r_pad = jnp.pad(router, ((0, 0), (0, LANES - E)))
    body = functools.partial(_router_body, n_experts=E)
    small = pl.BlockSpec((None, tt, LANES), lambda b, t: (b, t, 0))
    return pl.pallas_call(
        body,
        out_shape=(jax.ShapeDtypeStruct((B, T * (D // 2 // LANES), LANES), jnp.uint32),
                   jax.ShapeDtypeStruct((B, T // tt, 8, tt), jnp.int32),
                   jax.ShapeDtypeStruct((B, T, LANES), F32)),
        grid=(B, T // tt),
        in_specs=[pl.BlockSpec((None, tt, D), lambda b, t: (b, t, 0)),
                  pl.BlockSpec((None, 1, D), lambda b, t: (layer, 0, 0)),
                  _mod_spec(4, D), _mod_spec(3, D),
                  pl.BlockSpec((D, LANES), lambda b, t: (0, 0))],
        out_specs=(pl.BlockSpec((None, tt * (D // 2 // LANES), LANES), lambda b, t: (b, t, 0)),
                   pl.BlockSpec((None, None, 8, tt), lambda b, t: (b, t, 0, 0)), small),
        compiler_params=_params(("parallel", "parallel")),
        name="moe_router",
    )(x, g, mod, mod, r_pad)


def _route_tables(top_i, n_experts, n_tiles, n_slots):
    n_pairs = top_i.size
    e_flat = top_i.reshape(n_pairs)
    onehot = (e_flat[:, None] == jnp.arange(n_experts, dtype=jnp.int32)[None, :]).astype(jnp.int32)
    csum = jnp.cumsum(onehot, axis=0)
    counts = csum[-1]
    rank = jnp.sum(csum * onehot, axis=1) - 1
    nsub = (counts + FFN_SUB - 1) // FFN_SUB
    goff = jnp.cumsum(nsub) - nsub
    pos = jnp.sum(onehot * goff[None, :], axis=1) * FFN_SUB + rank
    src = jnp.zeros((n_slots,), jnp.int32).at[pos].set(jnp.arange(n_pairs, dtype=jnp.int32) // TOP_K)

    spr = FFN_ROWS // FFN_SUB
    ntile = (nsub + spr - 1) // spr
    per_tile = (nsub + jnp.maximum(ntile, 1) - 1) // jnp.maximum(ntile, 1)
    tcum = jnp.cumsum(ntile)
    toff = tcum - ntile
    ids = jnp.arange(n_tiles, dtype=jnp.int32)
    te = jnp.minimum(jnp.sum((ids[:, None] >= tcum[None, :]).astype(jnp.int32), axis=1), n_experts - 1)
    valid = ids < tcum[-1]
    j = ids - toff[te]
    t_start = goff[te] + j * per_tile[te]
    t_nsub = jnp.minimum(per_tile[te], nsub[te] - j * per_tile[te])
    last_e = te[jnp.maximum(tcum[-1] - 1, 0)]
    tile_e = jnp.where(valid, te, last_e).astype(jnp.int32)
    tile_start = jnp.where(valid, t_start, 0).astype(jnp.int32)
    tile_nsub = jnp.where(valid, t_nsub, 0).astype(jnp.int32)
    n_used_sub = jnp.sum(nsub).astype(jnp.int32).reshape(1)
    return pos.astype(jnp.int32), src, tile_e, tile_start, tile_nsub, n_used_sub


def _dense_tiles(n_rows):
    nsub = n_rows // FFN_SUB
    spr = FFN_ROWS // FFN_SUB
    ntile = -(-nsub // spr)
    per_tile = -(-nsub // ntile)
    starts = [j * per_tile for j in range(ntile)]
    counts = [min(per_tile, nsub - st) for st in starts]
    return (jnp.zeros((ntile,), jnp.int32), jnp.array(starts, jnp.int32), jnp.array(counts, jnp.int32))


def _dispatch_body(src_ref, h_ref, o_ref, buf, sem, *, chunk, slab):
    i = pl.program_id(0)

    def gather(step, slot):
        base = step * chunk

        def issue(r, carry):
            src_row = pl.multiple_of(src_ref[base + r] * slab, slab)
            dst_row = pl.multiple_of(r * slab, slab)
            pltpu.make_async_copy(h_ref.at[pl.ds(src_row, slab)], buf.at[slot, pl.ds(dst_row, slab)],
                                  sem.at[slot]).start()
            return carry

        lax.fori_loop(0, chunk, issue, 0, unroll=8)

    @pl.when(i == 0)
    def _():
        gather(0, 0)

    @pl.when(i + 1 < pl.num_programs(0))
    def _():
        gather(i + 1, (i + 1) % 2)

    slot = i % 2
    pltpu.make_async_copy(h_ref.at[pl.ds(0, chunk * slab)], buf.at[slot], sem.at[slot]).wait()
    half = slab * LANES
    for c in range(slab):
        lo, hi = _unpack_bf16_pairs(buf[slot, pl.ds(c, chunk, stride=slab), :])
        o_ref[:, c * LANES:(c + 1) * LANES] = lo
        o_ref[:, half + c * LANES:half + (c + 1) * LANES] = hi


def _dispatch(src, h, n_slots, *, chunk, slab):
    D = 2 * slab * LANES
    body = functools.partial(_dispatch_body, chunk=chunk, slab=slab)
    return pl.pallas_call(
        body,
        out_shape=jax.ShapeDtypeStruct((n_slots, D), BF16),
        grid_spec=pltpu.PrefetchScalarGridSpec(
            num_scalar_prefetch=1,
            grid=(n_slots // chunk,),
            in_specs=[pl.BlockSpec(memory_space=pl.ANY)],
            out_specs=pl.BlockSpec((chunk, D), lambda i, src: (i, 0)),
            scratch_shapes=[pltpu.VMEM((2, chunk * slab, LANES), jnp.uint32), pltpu.SemaphoreType.DMA((2,))]),
        compiler_params=_params(("arbitrary",)),
        name="moe_dispatch",
    )(src, h)


def _expert_body(te_ref, ts_ref, tn_ref, used_ref, xs_ref, wg_ref, wu_ref, wd_ref, y_ref,
                 xb, acc, wgb, wub, wdb, sem_in, sem_out, *, nf):
    s = pl.program_id(0)
    f = pl.program_id(1)
    nsub = tn_ref[s]
    start = ts_ref[s]
    d_model = acc.shape[1]

    def local_rows(j, n=1):
        return pl.ds(pl.multiple_of(j * FFN_SUB, FFN_SUB), n * FFN_SUB)

    def hbm_rows(j):
        return pl.ds(pl.multiple_of((start + j) * FFN_SUB, FFN_SUB), FFN_SUB)

    def in_copy(j):
        return pltpu.make_async_copy(xs_ref.at[hbm_rows(j)], xb.at[local_rows(j)], sem_in)

    def out_copy(j):
        return pltpu.make_async_copy(acc.at[local_rows(j)], y_ref.at[hbm_rows(j)], sem_out)

    def for_each_sub(lo, hi, fn):
        def step(j, carry):
            fn(j)
            return carry

        lax.fori_loop(lo, hi, step, 0)

    def zero_acc(j):
        acc[local_rows(j), :] = jnp.zeros((FFN_SUB, d_model), F32)

    @pl.when((s == 0) & (f == 0))
    def _():
        def fill_copy(j):
            row = pl.multiple_of(j * FFN_SUB, FFN_SUB)
            return pltpu.make_async_copy(acc.at[local_rows(0)], y_ref.at[pl.ds(row, FFN_SUB)], sem_out)

        n_sub_total = y_ref.shape[0] // FFN_SUB
        zero_acc(0)
        for_each_sub(used_ref[0], n_sub_total, lambda j: fill_copy(j).start())
        for_each_sub(used_ref[0], n_sub_total, lambda j: fill_copy(j).wait())

    @pl.when(nsub > 0)
    def _():
        @pl.when(f == 0)
        def _():
            for_each_sub(0, nsub, lambda j: in_copy(j).start())
            for_each_sub(0, nsub, zero_acc)
            for_each_sub(0, nsub, lambda j: in_copy(j).wait())

        def block(j0, n, cast_weights=False):
            if cast_weights:
                wg, wu, wd = (r[...].astype(BF16) for r in (wg_ref, wu_ref, wd_ref))
                wgb[...], wub[...], wdb[...] = wg, wu, wd
            else:
                wg, wu, wd = wgb[...], wub[...], wdb[...]
            rows = local_rows(j0, n)
            x = xb[rows, :]
            hg = jnp.dot(x, wg, preferred_element_type=F32)
            hu = jnp.dot(x, wu, preferred_element_type=F32)
            hid = (hg * jax.nn.sigmoid(hg) * hu).astype(BF16)
            acc[rows, :] += jnp.dot(hid, wd, preferred_element_type=F32)

            @pl.when(f == nf - 1)
            def _():
                for u in range(n):
                    out_copy(j0 + u).start()

        big = FFN_BLOCK_SUBS[0]
        n_big = nsub // big

        @pl.when(n_big > 0)
        def _():
            block(0, big, cast_weights=True)

        for_each_sub(1, n_big, lambda p: block(p * big, big))
        off = n_big * big
        started = n_big > 0
        for size in FFN_BLOCK_SUBS[1:]:
            has = ((nsub - off) // size) > 0

            @pl.when(has & jnp.logical_not(started))
            def _():
                block(0, size, cast_weights=True)

            @pl.when(has & started)
            def _():
                block(off, size)

            off = off + jnp.where(has, size, 0)
            started = started | has

        @pl.when(f == nf - 1)
        def _():
            for_each_sub(0, nsub, lambda j: out_copy(j).wait())


def _grouped_swiglu(tile_e, tile_start, tile_nsub, n_used_sub, xs, w_gate, w_up, w_down):
    P, D = xs.shape
    F = w_gate.shape[2]
    nf = F // FFN_TF
    n_tiles = tile_e.shape[0]

    def f_eff(s, f, tn):
        return jnp.where(tn[s] > 0, f, nf - 1)

    body = functools.partial(_expert_body, nf=nf)
    return pl.pallas_call(
        body,
        out_shape=jax.ShapeDtypeStruct((P, D), F32),
        grid_spec=pltpu.PrefetchScalarGridSpec(
            num_scalar_prefetch=4,
            grid=(n_tiles, nf),
            in_specs=[pl.BlockSpec(memory_space=pl.ANY),
                      pl.BlockSpec((None, D, FFN_TF), lambda s, f, te, ts, tn, used: (te[s], 0, f_eff(s, f, tn))),
                      pl.BlockSpec((None, D, FFN_TF), lambda s, f, te, ts, tn, used: (te[s], 0, f_eff(s, f, tn))),
                      pl.BlockSpec((None, FFN_TF, D), lambda s, f, te, ts, tn, used: (te[s], f_eff(s, f, tn), 0))],
            out_specs=pl.BlockSpec(memory_space=pl.ANY),
            scratch_shapes=[pltpu.VMEM((FFN_ROWS, D), BF16),
                            pltpu.VMEM((FFN_ROWS, D), F32),
                            pltpu.VMEM((D, FFN_TF), BF16),
                            pltpu.VMEM((D, FFN_TF), BF16),
                            pltpu.VMEM((FFN_TF, D), BF16),
                            pltpu.SemaphoreType.DMA,
                            pltpu.SemaphoreType.DMA]),
        compiler_params=_params(("arbitrary", "arbitrary")),
        name="grouped_swiglu",
    )(tile_e, tile_start, tile_nsub, n_used_sub, xs, w_gate, w_up, w_down)


def _combine_body(pos_ref, y_ref, x_ref, w_ref, g_ref, o_ref, ybuf, sem, *, tt):
    i = pl.program_id(0)

    def gather(step, slot):
        base = step * (tt * TOP_K)

        def issue(r, carry):
            for k in range(TOP_K):
                pltpu.make_async_copy(y_ref.at[pl.ds(pos_ref[base + r * TOP_K + k], 1)],
                                      ybuf.at[slot, k, pl.ds(r, 1)], sem.at[slot]).start()
            return carry

        lax.fori_loop(0, tt, issue, 0, unroll=4)

    @pl.when(i == 0)
    def _():
        gather(0, 0)

    @pl.when(i + 1 < pl.num_programs(0))
    def _():
        gather(i + 1, (i + 1) % 2)

    slot = i % 2
    for k in range(TOP_K):
        pltpu.make_async_copy(y_ref.at[pl.ds(0, tt)], ybuf.at[slot, k], sem.at[slot]).wait()
    w = w_ref[...]
    y = w[:, 0:1] * ybuf[slot, 0]
    for k in range(1, TOP_K):
        y = y + w[:, k:k + 1] * ybuf[slot, k]
    o_ref[...] = x_ref[...] + g_ref[...] * y


def _combine(pos, y, x, wts, mod, *, tt):
    N, D = x.shape
    tiles_per_batch = N // mod.shape[0] // tt
    body = functools.partial(_combine_body, tt=tt)
    return pl.pallas_call(
        body,
        out_shape=jax.ShapeDtypeStruct((N, D), F32),
        grid_spec=pltpu.PrefetchScalarGridSpec(
            num_scalar_prefetch=1,
            grid=(N // tt,),
            in_specs=[pl.BlockSpec(memory_space=pl.ANY),
                      pl.BlockSpec((tt, D), lambda i, p: (i, 0)),
                      pl.BlockSpec((tt, LANES), lambda i, p: (i, 0)),
                      pl.BlockSpec((None, None, 1, D), lambda i, p: (i // tiles_per_batch, 5, 0, 0))],
            out_specs=pl.BlockSpec((tt, D), lambda i, p: (i, 0)),
            scratch_shapes=[pltpu.VMEM((2, TOP_K, tt, D), F32), pltpu.SemaphoreType.DMA((2,))]),
        compiler_params=_params(("arbitrary",)),
        name="moe_combine",
    )(pos, y, x, wts, mod)


def _residual_normmod_body(x_ref, y_ref, gate_ref, g_ref, sc_ref, sh_ref, x_out_ref, h_ref):
    x_new = x_ref[...] + gate_ref[...] * y_ref[...]
    x_out_ref[...] = x_new
    h_ref[...] = _norm_mod(x_new, g_ref[...], sc_ref[...], sh_ref[...]).astype(h_ref.dtype)


def _residual_normmod(x, y, mod_prev, k_gate, g, layer, mod_next, k_scale, k_shift, *, tt, name):
    B, T, D = x.shape
    spec = pl.BlockSpec((None, tt, D), lambda b, t: (b, t, 0))
    return pl.pallas_call(
        _residual_normmod_body,
        out_shape=(jax.ShapeDtypeStruct((B, T, D), F32), jax.ShapeDtypeStruct((B, T, D), BF16)),
        grid=(B, T // tt),
        in_specs=[spec, spec, _mod_spec(k_gate, D),
                  pl.BlockSpec((None, 1, D), lambda b, t: (layer, 0, 0)),
                  _mod_spec(k_scale, D), _mod_spec(k_shift, D)],
        out_specs=(spec, spec),
        compiler_params=_params(("parallel", "parallel")),
        name=name,
    )(x, y, mod_prev, g, mod_next, mod_next)


def kernel(x, c, mod_w, mod_b, norm_mix_g, norm_ffn_g, na_w_in, na_q_norm_g, na_k_norm_g, na_rpb, na_w_out,
           ffn_w_gate, ffn_w_up, ffn_w_down, ml_w_in, ml_conv_w, ml_conv_b, ml_wq, ml_wk, ml_wv, ml_w_if,
           ml_b_if, ml_skip, ml_norm_g, ml_w_out, moe_router, moe_w_gate, moe_w_up, moe_w_down):
    B, T, D = x.shape
    N = B * T
    depth = mod_w.shape[0]
    c_pad = jnp.pad(c, ((0, 8 - B), (0, 0)))
    mix_g = norm_mix_g.reshape(depth, 1, D)
    ffn_g = norm_ffn_g.reshape(depth, 1, D)

    def modulation(i):
        mod = _matmul(c_pad, mod_w, i, tn=1024, tm=8, out_dtype=F32, a_silu=True, bias=mod_b,
                      name="adaln_modulation")
        return mod[:B].reshape(B, 6, 1, D)

    mod = modulation(0)
    h = _normmod(x, mix_g, 0, mod, 1, 0, tt=512, out_dtype=BF16, name="norm_mix0")
    na_dh = D // NA_HEADS
    qk_gain = jnp.concatenate([jnp.tile(na_q_norm_g[0] * na_dh ** -0.5, NA_HEADS), jnp.tile(na_k_norm_g[0], NA_HEADS),
                               jnp.ones((D,), F32)]).reshape(1, 3 * D)
    qkv = _matmul(h.reshape(N, D), na_w_in, 0, tn=1024, tm=MM_TM, out_dtype=BF16, head_gain=qk_gain,
                  norm_cols=2 * D, head_dim=na_dh, name="na_qkv")
    att = _neighborhood_attention(qkv.reshape(B, T, 3 * D), na_rpb[0])
    x = _matmul(att.reshape(N, D), na_w_out, 0, tn=1024, tm=MM_TM, out_dtype=F32, res=x.reshape(N, D),
                gate=mod, gate_idx=2, rows_per_batch=T, name="na_out").reshape(B, T, D)

    h = _normmod(x, ffn_g, 0, mod, 4, 3, tt=512, out_dtype=BF16, name="norm_ffn0")
    dense_e, dense_start, dense_nsub = _dense_tiles(N)
    y = _grouped_swiglu(dense_e, dense_start, dense_nsub, jnp.full((1,), N // FFN_SUB, jnp.int32),
                        h.reshape(N, D), ffn_w_gate, ffn_w_up, ffn_w_down)

    mod0, mod = mod, modulation(1)
    x, h = _residual_normmod(x, y.reshape(B, T, D), mod0, 5, mix_g, 1, mod, 1, 0, tt=512,
                             name="ffn0_residual_norm_mix1")
    inner = ml_w_in.shape[2] // 2
    xz = _matmul(h.reshape(N, D), ml_w_in, 0, tn=1024, tm=MM_TM, out_dtype=BF16,
                 name="mlstm_in").reshape(B, T, 2 * inner)
    q, kt, v, xc, pre = _mlstm_pre(xz, ml_conv_w[0], ml_conv_b[0], ml_wq[0], ml_wk[0], ml_wv[0],
                                  ml_w_if[0], ml_b_if[0], tt=256)
    hf, hb = _mlstm(q, kt, v, pre, L=ML_CHUNK)
    u = _mlstm_post(hf, hb, xc, xz, ml_norm_g[0], ml_skip[0], tt=256)
    x = _matmul(u.reshape(N, inner), ml_w_out, 0, tn=512, tm=MM_TM, out_dtype=F32, res=x.reshape(N, D),
                gate=mod, gate_idx=2, rows_per_batch=T, name="mlstm_out").reshape(B, T, D)

    h2, top_idx, top_w = _router(x, ffn_g, 1, mod, moe_router[0], tt=256)
    n_slots = N * TOP_K + N_EXPERTS * FFN_SUB
    n_tiles = -(-N * TOP_K // FFN_ROWS) + N_EXPERTS
    pos, src, tile_e, tile_start, tile_nsub, n_used_sub = _route_tables(
        jnp.moveaxis(top_idx[:, :, :TOP_K, :], 2, 3).reshape(N, TOP_K), N_EXPERTS, n_tiles, n_slots)
    slab = D // 2 // LANES
    xs = _dispatch(src, h2.reshape(N * slab, LANES), n_slots, chunk=FFN_SUB, slab=slab)
    ys = _grouped_swiglu(tile_e, tile_start, tile_nsub, n_used_sub, xs, moe_w_gate[0], moe_w_up[0], moe_w_down[0])
    return _combine(pos, ys, x.reshape(N, D), top_w.reshape(N, LANES), mod, tt=256).reshape(B, T, D)
```

```python
import functools

import jax
import jax.numpy as jnp
from jax import lax
from jax.experimental import pallas as pl
from jax.experimental.pallas import tpu as pltpu

F32 = jnp.float32
BF16 = jnp.bfloat16

GRID_W = 64
NA_HEADS = 16
NA_WIN_ROWS_MAX = 8
NA_WIN_COLS = 16
ML_HEADS = 8
ML_QKV_BLOCK = 4
ML_CONV_K = 5
N_EXPERTS = 8
TOP_K = 2
RMS_EPS = 1e-6
LN_EPS = 1e-5

V7X_VMEM_LIMIT_BYTES = 56 * 1024 * 1024
LANES = 128
BF16_SUBLANES = 16

MM_TM = 1024
ML_CHUNK = 256
ML_HEADS_PER_STEP = 2
FFN_SUB = 128
FFN_ROWS = 3072
FFN_BLOCK_SUBS = (8, 4, 2, 1)
DISPATCH_CHUNK = 256
FFN_TF = 256
MASK_NEG = -1e30
NA_ROW_GROUP = 8


def _params(sem):
    return pltpu.CompilerParams(dimension_semantics=sem, vmem_limit_bytes=V7X_VMEM_LIMIT_BYTES)


def _mm_body(*refs, a_silu, has_bias, has_res, norm_blocks, head_dim):
    a_ref, w_ref = refs[0], refs[1]
    k = 2
    if has_bias:
        b_ref = refs[k]
        k += 1
    if has_res:
        r_ref, g_ref = refs[k], refs[k + 1]
        k += 2
    if norm_blocks:
        hg_ref = refs[k]
        k += 1
    o_ref, wb_ref = refs[k], refs[k + 1]
    j = pl.program_id(0)
    i = pl.program_id(1)

    def finish(w):
        a = a_ref[...]
        if a_silu:
            a = a * jax.nn.sigmoid(a)
        y = jnp.dot(a.astype(BF16), w, preferred_element_type=F32)
        if has_bias:
            y = y + b_ref[...]
        if has_res:
            y = r_ref[...] + g_ref[...] * y
        if not norm_blocks:
            o_ref[...] = y.astype(o_ref.dtype)
            return

        normed = j < norm_blocks
        for h in range(y.shape[1] // head_dim):
            sl = slice(h * head_dim, (h + 1) * head_dim)
            yh = y[:, sl]
            inv = lax.rsqrt(jnp.mean(yh * yh, axis=-1, keepdims=True) + RMS_EPS)
            o_ref[:, sl] = (yh * (jnp.where(normed, inv, 1.0) * hg_ref[:, sl])).astype(o_ref.dtype)

    @pl.when(i == 0)
    def _():
        w = w_ref[...].astype(BF16)
        wb_ref[...] = w
        finish(w)

    @pl.when(i > 0)
    def _():
        finish(wb_ref[...])


def _matmul(a, w, layer, *, tn, tm, out_dtype, a_silu=False, bias=None, res=None, gate=None,
            gate_idx=0, rows_per_batch=None, head_gain=None, norm_cols=0, head_dim=None, name="matmul"):
    M, K = a.shape
    N = w.shape[2]
    grid = (N // tn, M // tm)
    in_specs = [pl.BlockSpec((tm, K), lambda j, i: (i, 0)),
                pl.BlockSpec((None, K, tn), lambda j, i: (layer, 0, j))]
    args = [a, w]
    if bias is not None:
        in_specs.append(pl.BlockSpec((None, 1, tn), lambda j, i: (layer, 0, j)))
        args.append(bias.reshape(bias.shape[0], 1, N))
    if res is not None:
        tiles_per_batch = rows_per_batch // tm
        in_specs.append(pl.BlockSpec((tm, tn), lambda j, i: (i, j)))
        in_specs.append(pl.BlockSpec((None, None, 1, tn),
                                     lambda j, i: (i // tiles_per_batch, gate_idx, 0, j)))
        args += [res, gate]
    if head_gain is not None:
        in_specs.append(pl.BlockSpec((1, tn), lambda j, i: (0, j)))
        args.append(head_gain)
    body = functools.partial(_mm_body, a_silu=a_silu, has_bias=bias is not None,
                             has_res=res is not None, norm_blocks=norm_cols // tn, head_dim=head_dim)
    return pl.pallas_call(
        body,
        out_shape=jax.ShapeDtypeStruct((M, N), out_dtype),
        grid=grid,
        in_specs=in_specs,
        out_specs=pl.BlockSpec((tm, tn), lambda j, i: (i, j)),
        scratch_shapes=[pltpu.VMEM((K, tn), BF16)],
        compiler_params=_params(("parallel", "arbitrary")),
        name=name,
    )(*args)


def _norm_mod(x, g, sc, sh):
    y = x * lax.rsqrt(jnp.mean(x * x, axis=-1, keepdims=True) + RMS_EPS) * g
    return y * (1.0 + sc) + sh


def _normmod_body(x_ref, g_ref, sc_ref, sh_ref, o_ref):
    o_ref[...] = _norm_mod(x_ref[...], g_ref[...], sc_ref[...], sh_ref[...]).astype(o_ref.dtype)


def _mod_spec(k, D):
    return pl.BlockSpec((None, None, 1, D), lambda b, t: (b, k, 0, 0))


def _normmod(x, g, layer, mod, k_scale, k_shift, *, tt, out_dtype, name):
    B, T, D = x.shape
    return pl.pallas_call(
        _normmod_body,
        out_shape=jax.ShapeDtypeStruct((B, T, D), out_dtype),
        grid=(B, T // tt),
        in_specs=[pl.BlockSpec((None, tt, D), lambda b, t: (b, t, 0)),
                  pl.BlockSpec((None, 1, D), lambda b, t: (layer, 0, 0)),
                  _mod_spec(k_scale, D), _mod_spec(k_shift, D)],
        out_specs=pl.BlockSpec((None, tt, D), lambda b, t: (b, t, 0)),
        compiler_params=_params(("parallel", "parallel")),
        name=name,
    )(x, g, mod, mod)


def _na_body(q_ref, k_ref, v_ref, bias_ref, o_ref, *, rows, width, kh):
    def row_group(gi, carry):
        offs, scores, probs = [], [], []
        for u in range(NA_ROW_GROUP):
            r = gi * NA_ROW_GROUP + u
            r0 = jnp.clip(r - kh // 2, 0, rows - kh)
            q_off = pl.multiple_of(r * width, width)
            b_off = pl.multiple_of(r0 * width, width)
            q_r = q_ref[pl.ds(q_off, width), :]
            k_band = k_ref[pl.ds(b_off, kh * width), :]
            s = lax.dot_general(q_r, k_band, (((1,), (1,)), ((), ())), preferred_element_type=F32)
            scores.append(s + bias_ref[r - r0])
            offs.append((q_off, b_off))
        for s in scores:
            e = jnp.exp(s - jnp.max(s, axis=-1, keepdims=True))
            probs.append((e.astype(BF16), jnp.sum(e, axis=-1, keepdims=True)))
        for (q_off, b_off), (e, l) in zip(offs, probs):
            v_band = v_ref[pl.ds(b_off, kh * width), :]
            o = jnp.dot(e, v_band, preferred_element_type=F32) / l
            o_ref[pl.ds(q_off, width), :] = o.astype(o_ref.dtype)
        return carry

    lax.fori_loop(0, rows // NA_ROW_GROUP, row_group, 0)


def _na_bias_table(rpb, kh):
    col = jnp.arange(GRID_W)
    col_start = jnp.clip(col - NA_WIN_COLS // 2, 0, GRID_W - NA_WIN_COLS)
    col_in = (col[None, :] >= col_start[:, None]) & (col[None, :] < col_start[:, None] + NA_WIN_COLS)
    H, n_di, n_dj = rpb.shape
    lo = GRID_W - NA_WIN_COLS
    ext = jnp.concatenate([jnp.repeat(rpb[..., :1], lo, axis=-1), rpb.astype(F32),
                           jnp.repeat(rpb[..., -1:], 2 * GRID_W - 1 - lo - n_dj, axis=-1)], axis=-1)
    skew = jnp.tile(ext, (1, 1, GRID_W + 1))[..., :GRID_W * 2 * GRID_W].reshape(H, n_di, GRID_W, 2 * GRID_W)
    toep = skew[:, :, ::-1, :GRID_W]
    rpb_cols = jnp.where(col_in[None, None], toep, MASK_NEG)
    top = NA_WIN_ROWS_MAX - 1
    tab = jnp.stack([rpb_cols[:, top - off:top - off + kh] for off in range(kh)], axis=1)
    H = rpb.shape[0]
    return jnp.transpose(tab, (0, 1, 3, 2, 4)).reshape(H, kh, GRID_W, kh * GRID_W)


def _neighborhood_attention(qkv, rpb):
    B, T, D3 = qkv.shape
    D = D3 // 3
    H = NA_HEADS
    dh = D // H
    rows = T // GRID_W
    kh = min(NA_WIN_ROWS_MAX, rows)
    bias = _na_bias_table(rpb, kh)
    body = functools.partial(_na_body, rows=rows, width=GRID_W, kh=kh)
    return pl.pallas_call(
        body,
        out_shape=jax.ShapeDtypeStruct((B, T, D), BF16),
        grid=(B, H),
        in_specs=[pl.BlockSpec((None, T, dh), lambda b, h: (b, 0, h)),
                  pl.BlockSpec((None, T, dh), lambda b, h: (b, 0, H + h)),
                  pl.BlockSpec((None, T, dh), lambda b, h: (b, 0, 2 * H + h)),
                  pl.BlockSpec((None, kh, GRID_W, kh * GRID_W), lambda b, h: (h, 0, 0, 0))],
        out_specs=pl.BlockSpec((None, T, dh), lambda b, h: (b, 0, h)),
        compiler_params=_params(("parallel", "parallel")),
        name="neighborhood_attention",
    )(qkv, qkv, qkv, bias)


def _mlstm_pre_body(xm_ref, xp_ref, xn_ref, cw_ref, cb_ref, wq_ref, wk_ref, wv_ref, wif_ref, bif_ref,
                    q_ref, kt_ref, v_ref, xc_ref, pre_ref, *, tt, inner, halo):
    t = pl.program_id(1)
    first = (t > 0).astype(F32)
    last = (t < pl.num_programs(1) - 1).astype(F32)
    pad = ML_CONV_K // 2
    n = tt + 2 * halo
    pre = jnp.zeros((tt, LANES), F32)
    for c in range(inner // LANES):
        sl = slice(c * LANES, (c + 1) * LANES)
        cur_b = xm_ref[:, sl]
        cat = jnp.concatenate([xp_ref[:, sl].astype(F32) * first, cur_b.astype(F32),
                               xn_ref[:, sl].astype(F32) * last], axis=0)
        xc = jnp.zeros((tt, LANES), F32) + cb_ref[:, sl]
        for j in range(ML_CONV_K):
            sh = pltpu.roll(cat, (pad - j) % n, axis=0) if j != pad else cat
            xc = xc + sh[halo:halo + tt] * cw_ref[j:j + 1, sl]
        xc = xc * jax.nn.sigmoid(xc)
        xc_b = xc.astype(BF16)
        q = jnp.dot(xc_b, wq_ref[c], preferred_element_type=F32).astype(BF16)
        k = jnp.dot(xc_b, wk_ref[c], preferred_element_type=F32).astype(BF16)
        v = jnp.dot(cur_b, wv_ref[c], preferred_element_type=F32).astype(BF16)
        pre = pre + jnp.dot(q, wif_ref[0, sl, :], preferred_element_type=F32)
        pre = pre + jnp.dot(k, wif_ref[1, sl, :], preferred_element_type=F32)
        pre = pre + jnp.dot(v, wif_ref[2, sl, :], preferred_element_type=F32)
        q_ref[:, sl] = q
        kt_ref[sl, :] = k.astype(F32).T.astype(BF16)
        v_ref[:, sl] = v
        xc_ref[:, sl] = xc_b
    pre_ref[...] = pre + bif_ref[...]


def _block_diag_dense(w):
    nb = w.shape[0]
    per = LANES // ML_QKV_BLOCK
    wr = w.reshape(nb // per, per, ML_QKV_BLOCK, ML_QKV_BLOCK)
    eye = jnp.eye(per, dtype=w.dtype)
    dense = jnp.einsum('gnio,nm->gnimo', wr, eye)
    return dense.reshape(nb // per, LANES, LANES).astype(BF16)


def _mlstm_pre(xz, conv_w, conv_b, wq, wk, wv, w_if, b_if, *, tt):
    B, T, inner2 = xz.shape
    inner = inner2 // 2
    halo = BF16_SUBLANES
    n_gate = w_if.shape[1] * w_if.shape[2] * w_if.shape[3]
    wif = jnp.pad(w_if.reshape(3, inner, n_gate), ((0, 0), (0, 0), (0, LANES - n_gate))).astype(BF16)
    bif = jnp.pad(b_if.reshape(1, n_gate), ((0, 0), (0, LANES - n_gate)))
    nblk = inner // LANES
    hb = tt // halo
    nh = T // halo
    body = functools.partial(_mlstm_pre_body, tt=tt, inner=inner, halo=halo)
    act = jax.ShapeDtypeStruct((B, T, inner), BF16)
    act_spec = pl.BlockSpec((None, tt, inner), lambda b, t: (b, t, 0))
    const3 = lambda b, t: (0, 0, 0)
    return pl.pallas_call(
        body,
        out_shape=(act, jax.ShapeDtypeStruct((B, inner, T), BF16), act, act,
                   jax.ShapeDtypeStruct((B, T, LANES), F32)),
        grid=(B, T // tt),
        in_specs=[act_spec,
                  pl.BlockSpec((None, halo, inner), lambda b, t: (b, jnp.maximum(t * hb - 1, 0), 0)),
                  pl.BlockSpec((None, halo, inner), lambda b, t: (b, jnp.minimum((t + 1) * hb, nh - 1), 0)),
                  pl.BlockSpec((ML_CONV_K, inner), lambda b, t: (0, 0)),
                  pl.BlockSpec((1, inner), lambda b, t: (0, 0)),
                  pl.BlockSpec((nblk, LANES, LANES), const3),
                  pl.BlockSpec((nblk, LANES, LANES), const3),
                  pl.BlockSpec((nblk, LANES, LANES), const3),
                  pl.BlockSpec((3, inner, LANES), const3),
                  pl.BlockSpec((1, LANES), lambda b, t: (0, 0))],
        out_specs=(act_spec, pl.BlockSpec((None, inner, tt), lambda b, t: (b, 0, t)), act_spec, act_spec,
                   pl.BlockSpec((None, tt, LANES), lambda b, t: (b, t, 0))),
        compiler_params=_params(("parallel", "parallel")),
        name="mlstm_pre",
    )(xz, xz, xz, conv_w, conv_b.reshape(1, inner), _block_diag_dense(wq), _block_diag_dense(wk),
      _block_diag_dense(wv), wif, bif)


def _log_sigmoid(x):
    return jnp.minimum(x, 0.0) - jnp.log(1.0 + jnp.exp(-jnp.abs(x)))


def _mlstm_direction(q, kt, v, li_r, lf_r, lf_c, S, m_ref, o_ref, *, reverse, L, dq):
    scale = dq ** -0.5
    lf_r = _log_sigmoid(lf_r)
    lf_c = _log_sigmoid(lf_c)
    row = lax.broadcasted_iota(jnp.int32, (L, L), 0)
    col = lax.broadcasted_iota(jnp.int32, (L, L), 1)
    vis = (col >= row) if reverse else (col <= row)
    vis_t = (row >= col) if reverse else (row <= col)
    b_c = jnp.sum(jnp.where(vis, lf_r, 0.0), axis=1, keepdims=True)
    b_r = jnp.sum(jnp.where(vis_t, lf_c, 0.0), axis=0, keepdims=True)
    g = jnp.sum(lf_r, axis=1, keepdims=True)
    m = m_ref[...]
    ones_col = (lax.broadcasted_iota(jnp.int32, (L, LANES), 1) == 0).astype(BF16)
    v_aug = jnp.concatenate([v, ones_col], axis=1)

    dlog = jnp.where(vis, b_c - b_r + li_r, -jnp.inf)
    m_inter = b_c + m
    m_t = jnp.maximum(m_inter, jnp.max(dlog, axis=1, keepdims=True))
    s = jnp.dot(q, kt, preferred_element_type=F32) * (jnp.exp(dlog - m_t) * scale)
    inter = jnp.exp(m_inter - m_t)
    out = jnp.dot(s.astype(BF16), v_aug, preferred_element_type=F32)
    out = out + jnp.dot(q, S[...].astype(BF16), preferred_element_type=F32) * inter
    den = out[:, dq:dq + 1]
    o_ref[...] = (out[:, :dq] / jnp.maximum(jnp.abs(den), jnp.exp(-m_t))).astype(o_ref.dtype)

    a = g - b_r + li_r
    m_new = jnp.maximum(g + m, jnp.max(a, axis=1, keepdims=True))
    w = jnp.exp(a - m_new) * scale
    decay = jnp.exp(g + m - m_new)
    S[...] = decay * S[...] + jnp.dot(kt * w.astype(BF16), v_aug, preferred_element_type=F32)
    m_ref[...] = m_new


def _mlstm_body(qf_ref, kf_ref, vf_ref, qb_ref, kb_ref, vb_ref, gr_f_ref, gr_b_ref, gc_f_ref, gc_b_ref,
                of_ref, ob_ref, S, m_ref, *, L, dq, heads):
    h0 = pl.program_id(1) * ML_HEADS_PER_STEP

    @pl.when(pl.program_id(2) == 0)
    def _():
        S[...] = jnp.zeros_like(S)
        m_ref[...] = jnp.zeros_like(m_ref)

    lane = lax.broadcasted_iota(jnp.int32, (L, LANES), 1)
    streams = ((qf_ref, kf_ref, vf_ref, gr_f_ref, gc_f_ref, of_ref),
               (qb_ref, kb_ref, vb_ref, gr_b_ref, gc_b_ref, ob_ref))
    for d, (q_ref, kt_ref, v_ref, gr_ref, gc_ref, o_ref) in enumerate(streams):
        for hh in range(ML_HEADS_PER_STEP):
            sl = slice(hh * dq, (hh + 1) * dq)
            i_idx = d * 2 * heads + h0 + hh
            f_idx = i_idx + heads
            lf_c = jnp.sum(jnp.where(lane == f_idx, gc_ref[...], 0.0), axis=1, keepdims=True)
            _mlstm_direction(q_ref[:, sl], kt_ref[sl, :], v_ref[:, sl],
                             gr_ref[pl.ds(i_idx, 1), :], gr_ref[pl.ds(f_idx, 1), :], lf_c,
                             S.at[d, hh], m_ref.at[d, hh], o_ref.at[:, sl], reverse=bool(d), L=L, dq=dq)


def _mlstm(q, kt, v, pre, *, L):
    B, T, inner = q.shape
    H = ML_HEADS
    dh = inner // H
    n = T // L
    n_gate = 4 * H
    gate_rows = jnp.transpose(pre[:, :, :n_gate], (0, 2, 1))

    fwd = lambda c: c
    bwd = lambda c: n - 1 - c
    hps = ML_HEADS_PER_STEP
    qv_spec = lambda at: pl.BlockSpec((None, L, hps * dh), lambda b, h, c: (b, at(c), h))
    kt_spec = lambda at: pl.BlockSpec((None, hps * dh, L), lambda b, h, c: (b, h, at(c)))
    row_spec = lambda at: pl.BlockSpec((None, n_gate, L), lambda b, h, c: (b, 0, at(c)))
    col_spec = lambda at: pl.BlockSpec((None, L, LANES), lambda b, h, c: (b, at(c), 0))
    out = jax.ShapeDtypeStruct((B, T, inner), BF16)
    body = functools.partial(_mlstm_body, L=L, dq=dh, heads=H)
    return pl.pallas_call(
        body,
        out_shape=(out, out),
        grid=(B, H // hps, n),
        in_specs=[qv_spec(fwd), kt_spec(fwd), qv_spec(fwd), qv_spec(bwd), kt_spec(bwd), qv_spec(bwd),
                  row_spec(fwd), row_spec(bwd), col_spec(fwd), col_spec(bwd)],
        out_specs=(qv_spec(fwd), qv_spec(bwd)),
        scratch_shapes=[pltpu.VMEM((2, hps, dh, dh + LANES), F32), pltpu.VMEM((2, hps, 1, 1), F32)],
        compiler_params=_params(("parallel", "parallel", "arbitrary")),
        name="mlstm_chunkwise",
    )(q, kt, v, q, kt, v, gate_rows, gate_rows, pre, pre)


def _mlstm_post_body(hf_ref, hb_ref, xc_ref, z_ref, ng_ref, skip_ref, o_ref, *, dh, inner):
    for h in range(inner // dh):
        sl = slice(h * dh, (h + 1) * dh)
        ht = hf_ref[:, sl].astype(F32) + hb_ref[:, sl].astype(F32)
        mu = jnp.mean(ht, axis=-1, keepdims=True)
        var = jnp.mean(jnp.square(ht - mu), axis=-1, keepdims=True)
        hn = (ht - mu) * lax.rsqrt(var + LN_EPS) * ng_ref[:, sl]
        out = (hn + skip_ref[:, sl] * xc_ref[:, sl].astype(F32)) * jax.nn.sigmoid(z_ref[:, sl].astype(F32))
        o_ref[:, sl] = out.astype(o_ref.dtype)


def _mlstm_post(hf, hb, xc, xz, norm_g, skip, *, tt):
    B, T, inner = hf.shape
    dh = inner // ML_HEADS
    body = functools.partial(_mlstm_post_body, dh=dh, inner=inner)
    vec = pl.BlockSpec((1, inner), lambda b, t: (0, 0))
    return pl.pallas_call(
        body,
        out_shape=jax.ShapeDtypeStruct((B, T, inner), BF16),
        grid=(B, T // tt),
        in_specs=[pl.BlockSpec((None, tt, inner), lambda b, t: (b, t, 0)),
                  pl.BlockSpec((None, tt, inner), lambda b, t: (b, t, 0)),
                  pl.BlockSpec((None, tt, inner), lambda b, t: (b, t, 0)),
                  pl.BlockSpec((None, tt, inner), lambda b, t: (b, t, 1)),
                  vec, vec],
        out_specs=pl.BlockSpec((None, tt, inner), lambda b, t: (b, t, 0)),
        compiler_params=_params(("parallel", "parallel")),
        name="mlstm_post",
    )(hf, hb, xc, xz, norm_g.reshape(1, inner), skip.reshape(1, inner))


def _pack_bf16_pairs(x):
    half = x.shape[1] // 2
    bits = pltpu.bitcast(x.astype(BF16).astype(F32), jnp.uint32)
    return (bits[:, :half] >> 16) | bits[:, half:]


def _unpack_bf16_pairs(w):
    lo = pltpu.bitcast(w << 16, F32).astype(BF16)
    hi = pltpu.bitcast(w & jnp.uint32(0xFFFF0000), F32).astype(BF16)
    return lo, hi


def _router_body(x_ref, g_ref, sc_ref, sh_ref, r_ref, h_ref, idx_ref, wt_ref, *, n_experts):
    h = _norm_mod(x_ref[...], g_ref[...], sc_ref[...], sh_ref[...])
    packed = _pack_bf16_pairs(h)
    n_lane_blocks = packed.shape[1] // LANES
    for c in range(n_lane_blocks):
        h_ref[pl.ds(c, packed.shape[0], stride=n_lane_blocks), :] = packed[:, c * LANES:(c + 1) * LANES]
    r = r_ref[...]
    h_hi, r_hi = h.astype(BF16), r.astype(BF16)
    h_lo = (h - h_hi.astype(F32)).astype(BF16)
    r_lo = (r - r_hi.astype(F32)).astype(BF16)
    logits = (jnp.dot(h_hi, r_hi, preferred_element_type=F32) + jnp.dot(h_lo, r_hi, preferred_element_type=F32)
              + jnp.dot(h_hi, r_lo, preferred_element_type=F32))
    lane = lax.broadcasted_iota(jnp.int32, logits.shape, 1)
    lg = jnp.where(lane < n_experts, logits, -jnp.inf)
    m1 = jnp.max(lg, axis=1, keepdims=True)
    i1 = jnp.min(jnp.where(lg == m1, lane, LANES), axis=1, keepdims=True)
    lg2 = jnp.where(lane == i1, -jnp.inf, lg)
    m2 = jnp.max(lg2, axis=1, keepdims=True)
    i2 = jnp.min(jnp.where(lg2 == m2, lane, LANES), axis=1, keepdims=True)
    e2 = jnp.exp(m2 - m1)
    w1 = 1.0 / (1.0 + e2)
    w2 = e2 * w1
    choices = jnp.where(lane == 0, i1, jnp.where(lane == 1, i2, 0)).astype(F32)
    idx_ref[...] = jnp.transpose(choices)[:idx_ref.shape[0], :].astype(jnp.int32)
    wt_ref[...] = jnp.where(lane == 0, w1, jnp.where(lane == 1, w2, 0.0))


def _router(x, g, layer, mod, router, *, tt):
    B, T, D = x.shape
    E = router.shape[1]
    r_pad = jnp.pad(router, ((0, 0), (0, LANES - E)))
    body = functools.partial(_router_body, n_experts=E)
    small = pl.BlockSpec((None, tt, LANES), lambda b, t: (b, t, 0))
    return pl.pallas_call(
        body,
        out_shape=(jax.ShapeDtypeStruct((B, T * (D // 2 // LANES), LANES), jnp.uint32),
                   jax.ShapeDtypeStruct((B, T // tt, 8, tt), jnp.int32),
                   jax.ShapeDtypeStruct((B, T, LANES), F32)),
        grid=(B, T // tt),
        in_specs=[pl.BlockSpec((None, tt, D), lambda b, t: (b, t, 0)),
                  pl.BlockSpec((None, 1, D), lambda b, t: (layer, 0, 0)),
                  _mod_spec(4, D), _mod_spec(3, D),
                  pl.BlockSpec((D, LANES), lambda b, t: (0, 0))],
        out_specs=(pl.BlockSpec((None, tt * (D // 2 // LANES), LANES), lambda b, t: (b, t, 0)),
                   pl.BlockSpec((None, None, 8, tt), lambda b, t: (b, t, 0, 0)), small),
        compiler_params=_params(("parallel", "parallel")),
        name="moe_router",
    )(x, g, mod, mod, r_pad)


def _route_tables(top_i, n_experts, n_tiles, n_slots):
    n_pairs = top_i.size
    e_flat = top_i.reshape(n_pairs)
    onehot = (e_flat[:, None] == jnp.arange(n_experts, dtype=jnp.int32)[None, :]).astype(jnp.int32)
    csum = jnp.cumsum(onehot, axis=0)
    counts = csum[-1]
    rank = jnp.sum(csum * onehot, axis=1) - 1
    nsub = (counts + FFN_SUB - 1) // FFN_SUB
    goff = jnp.cumsum(nsub) - nsub
    pos = jnp.sum(onehot * goff[None, :], axis=1) * FFN_SUB + rank
    src = jnp.zeros((n_slots,), jnp.int32).at[pos].set(jnp.arange(n_pairs, dtype=jnp.int32) // TOP_K)

    spr = FFN_ROWS // FFN_SUB
    ntile = (nsub + spr - 1) // spr
    per_tile = (nsub + jnp.maximum(ntile, 1) - 1) // jnp.maximum(ntile, 1)
    tcum = jnp.cumsum(ntile)
    toff = tcum - ntile
    ids = jnp.arange(n_tiles, dtype=jnp.int32)
    te = jnp.minimum(jnp.sum((ids[:, None] >= tcum[None, :]).astype(jnp.int32), axis=1), n_experts - 1)
    valid = ids < tcum[-1]
    j = ids - toff[te]
    t_start = goff[te] + j * per_tile[te]
    t_nsub = jnp.minimum(per_tile[te], nsub[te] - j * per_tile[te])
    last_e = te[jnp.maximum(tcum[-1] - 1, 0)]
    tile_e = jnp.where(valid, te, last_e).astype(jnp.int32)
    tile_start = jnp.where(valid, t_start, 0).astype(jnp.int32)
    tile_nsub = jnp.where(valid, t_nsub, 0).astype(jnp.int32)
    n_used_sub = jnp.sum(nsub).astype(jnp.int32).reshape(1)
    return pos.astype(jnp.int32), src, tile_e, tile_start, tile_nsub, n_used_sub


def _dense_tiles(n_rows):
    nsub = n_rows // FFN_SUB
    spr = FFN_ROWS // FFN_SUB
    ntile = -(-nsub // spr)
    per_tile = -(-nsub // ntile)
    starts = [j * per_tile for j in range(ntile)]
    counts = [min(per_tile, nsub - st) for st in starts]
    return (jnp.zeros((ntile,), jnp.int32), jnp.array(starts, jnp.int32), jnp.array(counts, jnp.int32))


def _dispatch_body(src_ref, h_ref, o_ref, buf, sem, *, chunk, slab):
    i = pl.program_id(0)

    def gather(step, slot):
        base = step * chunk

        def issue(r, carry):
            src_row = pl.multiple_of(src_ref[base + r] * slab, slab)
            dst_row = pl.multiple_of(r * slab, slab)
            pltpu.make_async_copy(h_ref.at[pl.ds(src_row, slab)], buf.at[slot, pl.ds(dst_row, slab)],
                                  sem.at[slot]).start()
            return carry

        lax.fori_loop(0, chunk, issue, 0, unroll=8)

    @pl.when(i == 0)
    def _():
        gather(0, 0)

    @pl.when(i + 1 < pl.num_programs(0))
    def _():
        gather(i + 1, (i + 1) % 2)

    slot = i % 2
    pltpu.make_async_copy(h_ref.at[pl.ds(0, chunk * slab)], buf.at[slot], sem.at[slot]).wait()
    half = slab * LANES
    for c in range(slab):
        lo, hi = _unpack_bf16_pairs(buf[slot, pl.ds(c, chunk, stride=slab), :])
        o_ref[:, c * LANES:(c + 1) * LANES] = lo
        o_ref[:, half + c * LANES:half + (c + 1) * LANES] = hi


def _dispatch(src, h, n_slots, *, chunk, slab):
    D = 2 * slab * LANES
    body = functools.partial(_dispatch_body, chunk=chunk, slab=slab)
    return pl.pallas_call(
        body,
        out_shape=jax.ShapeDtypeStruct((n_slots, D), BF16),
        grid_spec=pltpu.PrefetchScalarGridSpec(
            num_scalar_prefetch=1,
            grid=(n_slots // chunk,),
            in_specs=[pl.BlockSpec(memory_space=pl.ANY)],
            out_specs=pl.BlockSpec((chunk, D), lambda i, src: (i, 0)),
            scratch_shapes=[pltpu.VMEM((2, chunk * slab, LANES), jnp.uint32), pltpu.SemaphoreType.DMA((2,))]),
        compiler_params=_params(("arbitrary",)),
        name="moe_dispatch",
    )(src, h)


def _expert_body(te_ref, ts_ref, tn_ref, used_ref, xs_ref, wg_ref, wu_ref, wd_ref, y_ref,
                 xb, acc, wgb, wub, wdb, sem_in, sem_out, *, nf):
    s = pl.program_id(0)
    f = pl.program_id(1)
    nsub = tn_ref[s]
    start = ts_ref[s]
    d_model = acc.shape[1]

    def local_rows(j, n=1):
        return pl.ds(pl.multiple_of(j * FFN_SUB, FFN_SUB), n * FFN_SUB)

    def hbm_rows(j):
        return pl.ds(pl.multiple_of((start + j) * FFN_SUB, FFN_SUB), FFN_SUB)

    def in_copy(j):
        return pltpu.make_async_copy(xs_ref.at[hbm_rows(j)], xb.at[local_rows(j)], sem_in)

    def out_copy(j):
        return pltpu.make_async_copy(acc.at[local_rows(j)], y_ref.at[hbm_rows(j)], sem_out)

    def for_each_sub(lo, hi, fn):
        def step(j, carry):
            fn(j)
            return carry

        lax.fori_loop(lo, hi, step, 0)

    def zero_acc(j):
        acc[local_rows(j), :] = jnp.zeros((FFN_SUB, d_model), F32)

    @pl.when((s == 0) & (f == 0))
    def _():
        def fill_copy(j):
            row = pl.multiple_of(j * FFN_SUB, FFN_SUB)
            return pltpu.make_async_copy(acc.at[local_rows(0)], y_ref.at[pl.ds(row, FFN_SUB)], sem_out)

        n_sub_total = y_ref.shape[0] // FFN_SUB
        zero_acc(0)
        for_each_sub(used_ref[0], n_sub_total, lambda j: fill_copy(j).start())
        for_each_sub(used_ref[0], n_sub_total, lambda j: fill_copy(j).wait())

    @pl.when(nsub > 0)
    def _():
        @pl.when(f == 0)
        def _():
            for_each_sub(0, nsub, lambda j: in_copy(j).start())
            for_each_sub(0, nsub, zero_acc)
            for_each_sub(0, nsub, lambda j: in_copy(j).wait())

        def block(j0, n, cast_weights=False):
            if cast_weights:
                wg, wu, wd = (r[...].astype(BF16) for r in (wg_ref, wu_ref, wd_ref))
                wgb[...], wub[...], wdb[...] = wg, wu, wd
            else:
                wg, wu, wd = wgb[...], wub[...], wdb[...]
            rows = local_rows(j0, n)
            x = xb[rows, :]
            hg = jnp.dot(x, wg, preferred_element_type=F32)
            hu = jnp.dot(x, wu, preferred_element_type=F32)
            hid = (hg * jax.nn.sigmoid(hg) * hu).astype(BF16)
            acc[rows, :] += jnp.dot(hid, wd, preferred_element_type=F32)

            @pl.when(f == nf - 1)
            def _():
                for u in range(n):
                    out_copy(j0 + u).start()

        big = FFN_BLOCK_SUBS[0]
        n_big = nsub // big

        @pl.when(n_big > 0)
        def _():
            block(0, big, cast_weights=True)

        for_each_sub(1, n_big, lambda p: block(p * big, big))
        off = n_big * big
        started = n_big > 0
        for size in FFN_BLOCK_SUBS[1:]:
            has = ((nsub - off) // size) > 0

            @pl.when(has & jnp.logical_not(started))
            def _():
                block(0, size, cast_weights=True)

            @pl.when(has & started)
            def _():
                block(off, size)

            off = off + jnp.where(has, size, 0)
            started = started | has

        @pl.when(f == nf - 1)
        def _():
            for_each_sub(0, nsub, lambda j: out_copy(j).wait())


def _grouped_swiglu(tile_e, tile_start, tile_nsub, n_used_sub, xs, w_gate, w_up, w_down):
    P, D = xs.shape
    F = w_gate.shape[2]
    nf = F // FFN_TF
    n_tiles = tile_e.shape[0]

    def f_eff(s, f, tn):
        return jnp.where(tn[s] > 0, f, nf - 1)

    body = functools.partial(_expert_body, nf=nf)
    return pl.pallas_call(
        body,
        out_shape=jax.ShapeDtypeStruct((P, D), F32),
        grid_spec=pltpu.PrefetchScalarGridSpec(
            num_scalar_prefetch=4,
            grid=(n_tiles, nf),
            in_specs=[pl.BlockSpec(memory_space=pl.ANY),
                      pl.BlockSpec((None, D, FFN_TF), lambda s, f, te, ts, tn, used: (te[s], 0, f_eff(s, f, tn))),
                      pl.BlockSpec((None, D, FFN_TF), lambda s, f, te, ts, tn, used: (te[s], 0, f_eff(s, f, tn))),
                      pl.BlockSpec((None, FFN_TF, D), lambda s, f, te, ts, tn, used: (te[s], f_eff(s, f, tn), 0))],
            out_specs=pl.BlockSpec(memory_space=pl.ANY),
            scratch_shapes=[pltpu.VMEM((FFN_ROWS, D), BF16),
                            pltpu.VMEM((FFN_ROWS, D), F32),
                            pltpu.VMEM((D, FFN_TF), BF16),
                            pltpu.VMEM((D, FFN_TF), BF16),
                            pltpu.VMEM((FFN_TF, D), BF16),
                            pltpu.SemaphoreType.DMA,
                            pltpu.SemaphoreType.DMA]),
        compiler_params=_params(("arbitrary", "arbitrary")),
        name="grouped_swiglu",
    )(tile_e, tile_start, tile_nsub, n_used_sub, xs, w_gate, w_up, w_down)


def _combine_body(pos_ref, y_ref, x_ref, w_ref, g_ref, o_ref, ybuf, sem, *, tt):
    i = pl.program_id(0)

    def gather(step, slot):
        base = step * (tt * TOP_K)

        def issue(r, carry):
            for k in range(TOP_K):
                pltpu.make_async_copy(y_ref.at[pl.ds(pos_ref[base + r * TOP_K + k], 1)],
                                      ybuf.at[slot, k, pl.ds(r, 1)], sem.at[slot]).start()
            return carry

        lax.fori_loop(0, tt, issue, 0, unroll=4)

    @pl.when(i == 0)
    def _():
        gather(0, 0)

    @pl.when(i + 1 < pl.num_programs(0))
    def _():
        gather(i + 1, (i + 1) % 2)

    slot = i % 2
    for k in range(TOP_K):
        pltpu.make_async_copy(y_ref.at[pl.ds(0, tt)], ybuf.at[slot, k], sem.at[slot]).wait()
    w = w_ref[...]
    y = w[:, 0:1] * ybuf[slot, 0]
    for k in range(1, TOP_K):
        y = y + w[:, k:k + 1] * ybuf[slot, k]
    o_ref[...] = x_ref[...] + g_ref[...] * y


def _combine(pos, y, x, wts, mod, *, tt):
    N, D = x.shape
    tiles_per_batch = N // mod.shape[0] // tt
    body = functools.partial(_combine_body, tt=tt)
    return pl.pallas_call(
        body,
        out_shape=jax.ShapeDtypeStruct((N, D), F32),
        grid_spec=pltpu.PrefetchScalarGridSpec(
            num_scalar_prefetch=1,
            grid=(N // tt,),
            in_specs=[pl.BlockSpec(memory_space=pl.ANY),
                      pl.BlockSpec((tt, D), lambda i, p: (i, 0)),
                      pl.BlockSpec((tt, LANES), lambda i, p: (i, 0)),
                      pl.BlockSpec((None, None, 1, D), lambda i, p: (i // tiles_per_batch, 5, 0, 0))],
            out_specs=pl.BlockSpec((tt, D), lambda i, p: (i, 0)),
            scratch_shapes=[pltpu.VMEM((2, TOP_K, tt, D), F32), pltpu.SemaphoreType.DMA((2,))]),
        compiler_params=_params(("arbitrary",)),
        name="moe_combine",
    )(pos, y, x, wts, mod)


def _residual_normmod_body(x_ref, y_ref, gate_ref, g_ref, sc_ref, sh_ref, x_out_ref, h_ref):
    x_new = x_ref[...] + gate_ref[...] * y_ref[...]
    x_out_ref[...] = x_new
    h_ref[...] = _norm_mod(x_new, g_ref[...], sc_ref[...], sh_ref[...]).astype(h_ref.dtype)


def _residual_normmod(x, y, mod_prev, k_gate, g, layer, mod_next, k_scale, k_shift, *, tt, name):
    B, T, D = x.shape
    spec = pl.BlockSpec((None, tt, D), lambda b, t: (b, t, 0))
    return pl.pallas_call(
        _residual_normmod_body,
        out_shape=(jax.ShapeDtypeStruct((B, T, D), F32), jax.ShapeDtypeStruct((B, T, D), BF16)),
        grid=(B, T // tt),
        in_specs=[spec, spec, _mod_spec(k_gate, D),
                  pl.BlockSpec((None, 1, D), lambda b, t: (layer, 0, 0)),
                  _mod_spec(k_scale, D), _mod_spec(k_shift, D)],
        out_specs=(spec, spec),
        compiler_params=_params(("parallel", "parallel")),
        name=name,
    )(x, y, mod_prev, g, mod_next, mod_next)


def kernel(x, c, mod_w, mod_b, norm_mix_g, norm_ffn_g, na_w_in, na_q_norm_g, na_k_norm_g, na_rpb, na_w_out,
           ffn_w_gate, ffn_w_up, ffn_w_down, ml_w_in, ml_conv_w, ml_conv_b, ml_wq, ml_wk, ml_wv, ml_w_if,
           ml_b_if, ml_skip, ml_norm_g, ml_w_out, moe_router, moe_w_gate, moe_w_up, moe_w_down):
    B, T, D = x.shape
    N = B * T
    depth = mod_w.shape[0]
    c_pad = jnp.pad(c, ((0, 8 - B), (0, 0)))
    mix_g = norm_mix_g.reshape(depth, 1, D)
    ffn_g = norm_ffn_g.reshape(depth, 1, D)

    def modulation(i):
        mod = _matmul(c_pad, mod_w, i, tn=1024, tm=8, out_dtype=F32, a_silu=True, bias=mod_b,
                      name="adaln_modulation")
        return mod[:B].reshape(B, 6, 1, D)

    mod = modulation(0)
    h = _normmod(x, mix_g, 0, mod, 1, 0, tt=512, out_dtype=BF16, name="norm_mix0")
    na_dh = D // NA_HEADS
    qk_gain = jnp.concatenate([jnp.tile(na_q_norm_g[0] * na_dh ** -0.5, NA_HEADS), jnp.tile(na_k_norm_g[0], NA_HEADS),
                               jnp.ones((D,), F32)]).reshape(1, 3 * D)
    qkv = _matmul(h.reshape(N, D), na_w_in, 0, tn=1024, tm=MM_TM, out_dtype=BF16, head_gain=qk_gain,
                  norm_cols=2 * D, head_dim=na_dh, name="na_qkv")
    att = _neighborhood_attention(qkv.reshape(B, T, 3 * D), na_rpb[0])
    x = _matmul(att.reshape(N, D), na_w_out, 0, tn=1024, tm=MM_TM, out_dtype=F32, res=x.reshape(N, D),
                gate=mod, gate_idx=2, rows_per_batch=T, name="na_out").reshape(B, T, D)

    h = _normmod(x, ffn_g, 0, mod, 4, 3, tt=512, out_dtype=BF16, name="norm_ffn0")
    dense_e, dense_start, dense_nsub = _dense_tiles(N)
    y = _grouped_swiglu(dense_e, dense_start, dense_nsub, jnp.full((1,), N // FFN_SUB, jnp.int32),
                        h.reshape(N, D), ffn_w_gate, ffn_w_up, ffn_w_down)

    mod0, mod = mod, modulation(1)
    x, h = _residual_normmod(x, y.reshape(B, T, D), mod0, 5, mix_g, 1, mod, 1, 0, tt=512,
                             name="ffn0_residual_norm_mix1")
    inner = ml_w_in.shape[2] // 2
    xz = _matmul(h.reshape(N, D), ml_w_in, 0, tn=1024, tm=MM_TM, out_dtype=BF16,
                 name="mlstm_in").reshape(B, T, 2 * inner)
    q, kt, v, xc, pre = _mlstm_pre(xz, ml_conv_w[0], ml_conv_b[0], ml_wq[0], ml_wk[0], ml_wv[0],
                                  ml_w_if[0], ml_b_if[0], tt=256)
    hf, hb = _mlstm(q, kt, v, pre, L=ML_CHUNK)
    u = _mlstm_post(hf, hb, xc, xz, ml_norm_g[0], ml_skip[0], tt=256)
    x = _matmul(u.reshape(N, inner), ml_w_out, 0, tn=512, tm=MM_TM, out_dtype=F32, res=x.reshape(N, D),
                gate=mod, gate_idx=2, rows_per_batch=T, name="mlstm_out").reshape(B, T, D)

    h2, top_idx, top_w = _router(x, ffn_g, 1, mod, moe_router[0], tt=256)
    n_slots = N * TOP_K + N_EXPERTS * FFN_SUB
    n_tiles = -(-N * TOP_K // FFN_ROWS) + N_EXPERTS
    pos, src, tile_e, tile_start, tile_nsub, n_used_sub = _route_tables(
        jnp.moveaxis(top_idx[:, :, :TOP_K, :], 2, 3).reshape(N, TOP_K), N_EXPERTS, n_tiles, n_slots)
    slab = D // 2 // LANES
    xs = _dispatch(src, h2.reshape(N * slab, LANES), n_slots, chunk=DISPATCH_CHUNK, slab=slab)
    ys = _grouped_swiglu(tile_e, tile_start, tile_nsub, n_used_sub, xs, moe_w_gate[0], moe_w_up[0], moe_w_down[0])
    return _combine(pos, ys, x.reshape(N, D), top_w.reshape(N, LANES), mod, tt=256).reshape(B, T, D)
```

```python
import functools

import jax
import jax.numpy as jnp
from jax import lax
from jax.experimental import pallas as pl
from jax.experimental.pallas import tpu as pltpu

F32 = jnp.float32
BF16 = jnp.bfloat16

GRID_W = 64
NA_HEADS = 16
NA_WIN_ROWS_MAX = 8
NA_WIN_COLS = 16
ML_HEADS = 8
ML_QKV_BLOCK = 4
ML_CONV_K = 5
N_EXPERTS = 8
TOP_K = 2
RMS_EPS = 1e-6
LN_EPS = 1e-5

V7X_VMEM_LIMIT_BYTES = 56 * 1024 * 1024
LANES = 128
BF16_SUBLANES = 16

MM_TM = 1024
ML_CHUNK = 256
ML_HEADS_PER_STEP = 2
ML_LANE_BLOCK = 128
FFN_SUB = 128
FFN_ROWS = 3072
FFN_BLOCK_SUBS = (8, 4, 2, 1)
DISPATCH_CHUNK = 256
FFN_TF = 256
MASK_NEG = -1e30
NA_ROW_GROUP = 8


def _params(sem):
    return pltpu.CompilerParams(dimension_semantics=sem, vmem_limit_bytes=V7X_VMEM_LIMIT_BYTES)


def _mm_body(*refs, a_silu, has_bias, has_res, norm_blocks, head_dim):
    a_ref, w_ref = refs[0], refs[1]
    k = 2
    if has_bias:
        b_ref = refs[k]
        k += 1
    if has_res:
        r_ref, g_ref = refs[k], refs[k + 1]
        k += 2
    if norm_blocks:
        hg_ref = refs[k]
        k += 1
    o_ref, wb_ref = refs[k], refs[k + 1]
    j = pl.program_id(0)
    i = pl.program_id(1)

    def finish(w):
        a = a_ref[...]
        if a_silu:
            a = a * jax.nn.sigmoid(a)
        y = jnp.dot(a.astype(BF16), w, preferred_element_type=F32)
        if has_bias:
            y = y + b_ref[...]
        if has_res:
            y = r_ref[...] + g_ref[...] * y
        if not norm_blocks:
            o_ref[...] = y.astype(o_ref.dtype)
            return

        normed = j < norm_blocks
        for h in range(y.shape[1] // head_dim):
            sl = slice(h * head_dim, (h + 1) * head_dim)
            yh = y[:, sl]
            inv = lax.rsqrt(jnp.mean(yh * yh, axis=-1, keepdims=True) + RMS_EPS)
            o_ref[:, sl] = (yh * (jnp.where(normed, inv, 1.0) * hg_ref[:, sl])).astype(o_ref.dtype)

    @pl.when(i == 0)
    def _():
        w = w_ref[...].astype(BF16)
        wb_ref[...] = w
        finish(w)

    @pl.when(i > 0)
    def _():
        finish(wb_ref[...])


def _matmul(a, w, layer, *, tn, tm, out_dtype, a_silu=False, bias=None, res=None, gate=None,
            gate_idx=0, rows_per_batch=None, head_gain=None, norm_cols=0, head_dim=None, name="matmul"):
    M, K = a.shape
    N = w.shape[2]
    grid = (N // tn, M // tm)
    in_specs = [pl.BlockSpec((tm, K), lambda j, i: (i, 0)),
                pl.BlockSpec((None, K, tn), lambda j, i: (layer, 0, j))]
    args = [a, w]
    if bias is not None:
        in_specs.append(pl.BlockSpec((None, 1, tn), lambda j, i: (layer, 0, j)))
        args.append(bias.reshape(bias.shape[0], 1, N))
    if res is not None:
        tiles_per_batch = rows_per_batch // tm
        in_specs.append(pl.BlockSpec((tm, tn), lambda j, i: (i, j)))
        in_specs.append(pl.BlockSpec((None, None, 1, tn),
                                     lambda j, i: (i // tiles_per_batch, gate_idx, 0, j)))
        args += [res, gate]
    if head_gain is not None:
        in_specs.append(pl.BlockSpec((1, tn), lambda j, i: (0, j)))
        args.append(head_gain)
    body = functools.partial(_mm_body, a_silu=a_silu, has_bias=bias is not None,
                             has_res=res is not None, norm_blocks=norm_cols // tn, head_dim=head_dim)
    return pl.pallas_call(
        body,
        out_shape=jax.ShapeDtypeStruct((M, N), out_dtype),
        grid=grid,
        in_specs=in_specs,
        out_specs=pl.BlockSpec((tm, tn), lambda j, i: (i, j)),
        scratch_shapes=[pltpu.VMEM((K, tn), BF16)],
        compiler_params=_params(("parallel", "arbitrary")),
        name=name,
    )(*args)


def _norm_mod(x, g, sc, sh):
    y = x * lax.rsqrt(jnp.mean(x * x, axis=-1, keepdims=True) + RMS_EPS) * g
    return y * (1.0 + sc) + sh


def _normmod_body(x_ref, g_ref, sc_ref, sh_ref, o_ref):
    o_ref[...] = _norm_mod(x_ref[...], g_ref[...], sc_ref[...], sh_ref[...]).astype(o_ref.dtype)


def _mod_spec(k, D):
    return pl.BlockSpec((None, None, 1, D), lambda b, t: (b, k, 0, 0))


def _normmod(x, g, layer, mod, k_scale, k_shift, *, tt, out_dtype, name):
    B, T, D = x.shape
    return pl.pallas_call(
        _normmod_body,
        out_shape=jax.ShapeDtypeStruct((B, T, D), out_dtype),
        grid=(B, T // tt),
        in_specs=[pl.BlockSpec((None, tt, D), lambda b, t: (b, t, 0)),
                  pl.BlockSpec((None, 1, D), lambda b, t: (layer, 0, 0)),
                  _mod_spec(k_scale, D), _mod_spec(k_shift, D)],
        out_specs=pl.BlockSpec((None, tt, D), lambda b, t: (b, t, 0)),
        compiler_params=_params(("parallel", "parallel")),
        name=name,
    )(x, g, mod, mod)


def _na_body(q_ref, k_ref, v_ref, bias_ref, o_ref, *, rows, width, kh):
    def row_group(gi, carry):
        offs, scores, probs = [], [], []
        for u in range(NA_ROW_GROUP):
            r = gi * NA_ROW_GROUP + u
            r0 = jnp.clip(r - kh // 2, 0, rows - kh)
            q_off = pl.multiple_of(r * width, width)
            b_off = pl.multiple_of(r0 * width, width)
            q_r = q_ref[pl.ds(q_off, width), :]
            k_band = k_ref[pl.ds(b_off, kh * width), :]
            s = lax.dot_general(q_r, k_band, (((1,), (1,)), ((), ())), preferred_element_type=F32)
            rel = r0 - r + NA_WIN_ROWS_MAX - 1
            bias = jnp.concatenate([bias_ref[rel + i, 0] + bias_ref[rel + i + 1, 1] for i in range(0, kh, 2)],
                                   axis=1)
            scores.append(s + bias)
            offs.append((q_off, b_off))
        for s in scores:
            e = jnp.exp(s - jnp.max(s, axis=-1, keepdims=True))
            probs.append((e.astype(BF16), jnp.sum(e, axis=-1, keepdims=True)))
        for (q_off, b_off), (e, l) in zip(offs, probs):
            v_band = v_ref[pl.ds(b_off, kh * width), :]
            o = jnp.dot(e, v_band, preferred_element_type=F32) / l
            o_ref[pl.ds(q_off, width), :] = o.astype(o_ref.dtype)
        return carry

    lax.fori_loop(0, rows // NA_ROW_GROUP, row_group, 0)


def _na_bias_table(rpb):
    col = jnp.arange(GRID_W)
    col_start = jnp.clip(col - NA_WIN_COLS // 2, 0, GRID_W - NA_WIN_COLS)
    col_in = (col[None, :] >= col_start[:, None]) & (col[None, :] < col_start[:, None] + NA_WIN_COLS)
    H, n_di, n_dj = rpb.shape
    far = GRID_W - NA_WIN_COLS
    rpb = rpb.astype(F32)
    circle = jnp.concatenate([rpb[..., NA_WIN_COLS - 1:], jnp.repeat(rpb[..., -1:], far, axis=-1),
                              jnp.repeat(rpb[..., :1], far, axis=-1), rpb[..., :NA_WIN_COLS - 1]], axis=-1)
    pitch = 2 * GRID_W - 2
    toep = jnp.tile(circle, (1, 1, GRID_W))[..., :GRID_W * pitch].reshape(H, n_di, GRID_W, pitch)[..., :GRID_W]
    rpb_cols = jnp.where(col_in[None, None], toep, MASK_NEG)
    zeros = jnp.zeros_like(rpb_cols)
    return jnp.stack([jnp.concatenate([rpb_cols, zeros], axis=-1),
                      jnp.concatenate([zeros, rpb_cols], axis=-1)], axis=2)


def _neighborhood_attention(qkv, rpb):
    B, T, D3 = qkv.shape
    D = D3 // 3
    H = NA_HEADS
    dh = D // H
    rows = T // GRID_W
    kh = min(NA_WIN_ROWS_MAX, rows)
    bias = _na_bias_table(rpb)
    body = functools.partial(_na_body, rows=rows, width=GRID_W, kh=kh)
    return pl.pallas_call(
        body,
        out_shape=jax.ShapeDtypeStruct((B, T, D), BF16),
        grid=(B, H),
        in_specs=[pl.BlockSpec((None, T, dh), lambda b, h: (b, 0, h)),
                  pl.BlockSpec((None, T, dh), lambda b, h: (b, 0, H + h)),
                  pl.BlockSpec((None, T, dh), lambda b, h: (b, 0, 2 * H + h)),
                  pl.BlockSpec((None,) + bias.shape[1:], lambda b, h: (h, 0, 0, 0, 0))],
        out_specs=pl.BlockSpec((None, T, dh), lambda b, h: (b, 0, h)),
        compiler_params=_params(("parallel", "parallel")),
        name="neighborhood_attention",
    )(qkv, qkv, qkv, bias)


def _mlstm_pre_body(xm_ref, xp_ref, xn_ref, cw_ref, cb_ref, wq_ref, wk_ref, wv_ref, wif_ref, bif_ref,
                    q_ref, kt_ref, v_ref, xc_ref, pre_ref, cat_ref, *, tt, inner, halo):
    t = pl.program_id(1)
    first = (t > 0).astype(F32)
    last = (t < pl.num_programs(1) - 1).astype(F32)
    pad = ML_CONV_K // 2
    pre = jnp.zeros((tt, LANES), F32)
    width = ML_LANE_BLOCK
    for c in range(inner // width):
        sl = slice(c * width, (c + 1) * width)
        cur_b = xm_ref[:, sl]
        cat = jnp.concatenate([xp_ref[:, sl].astype(F32) * first, cur_b.astype(F32),
                               xn_ref[:, sl].astype(F32) * last], axis=0)
        cat_ref[...] = cat
        xc = jnp.zeros((tt, width), F32) + cb_ref[:, sl]
        for j in range(ML_CONV_K):
            xc = xc + cat_ref[halo - pad + j:halo - pad + j + tt, :] * cw_ref[j:j + 1, sl]
        xc = xc * jax.nn.sigmoid(xc)
        xc_b = xc.astype(BF16)
        q = jnp.dot(xc_b, wq_ref[c], preferred_element_type=F32).astype(BF16)
        k = jnp.dot(xc_b, wk_ref[c], preferred_element_type=F32).astype(BF16)
        v = jnp.dot(cur_b, wv_ref[c], preferred_element_type=F32).astype(BF16)
        pre = pre + jnp.dot(q, wif_ref[0, sl, :], preferred_element_type=F32)
        pre = pre + jnp.dot(k, wif_ref[1, sl, :], preferred_element_type=F32)
        pre = pre + jnp.dot(v, wif_ref[2, sl, :], preferred_element_type=F32)
        q_ref[:, sl] = q
        kt_ref[sl, :] = k.astype(F32).T.astype(BF16)
        v_ref[:, sl] = v
        xc_ref[:, sl] = xc_b
    pre_ref[...] = pre + bif_ref[...]


def _block_diag_dense(w):
    nb = w.shape[0]
    per = ML_LANE_BLOCK // ML_QKV_BLOCK
    wr = w.reshape(nb // per, per, ML_QKV_BLOCK, ML_QKV_BLOCK)
    eye = jnp.eye(per, dtype=w.dtype)
    dense = jnp.einsum('gnio,nm->gnimo', wr, eye)
    return dense.reshape(nb // per, ML_LANE_BLOCK, ML_LANE_BLOCK).astype(BF16)


def _mlstm_pre(xz, conv_w, conv_b, wq, wk, wv, w_if, b_if, *, tt):
    B, T, inner2 = xz.shape
    inner = inner2 // 2
    halo = BF16_SUBLANES
    n_gate = w_if.shape[1] * w_if.shape[2] * w_if.shape[3]
    wif = jnp.pad(w_if.reshape(3, inner, n_gate), ((0, 0), (0, 0), (0, LANES - n_gate))).astype(BF16)
    bif = jnp.pad(b_if.reshape(1, n_gate), ((0, 0), (0, LANES - n_gate)))
    nblk = inner // ML_LANE_BLOCK
    hb = tt // halo
    nh = T // halo
    body = functools.partial(_mlstm_pre_body, tt=tt, inner=inner, halo=halo)
    act = jax.ShapeDtypeStruct((B, T, inner), BF16)
    act_spec = pl.BlockSpec((None, tt, inner), lambda b, t: (b, t, 0))
    const3 = lambda b, t: (0, 0, 0)
    return pl.pallas_call(
        body,
        out_shape=(act, jax.ShapeDtypeStruct((B, inner, T), BF16), act, act,
                   jax.ShapeDtypeStruct((B, T, LANES), F32)),
        grid=(B, T // tt),
        in_specs=[act_spec,
                  pl.BlockSpec((None, halo, inner), lambda b, t: (b, jnp.maximum(t * hb - 1, 0), 0)),
                  pl.BlockSpec((None, halo, inner), lambda b, t: (b, jnp.minimum((t + 1) * hb, nh - 1), 0)),
                  pl.BlockSpec((ML_CONV_K, inner), lambda b, t: (0, 0)),
                  pl.BlockSpec((1, inner), lambda b, t: (0, 0)),
                  pl.BlockSpec((nblk, ML_LANE_BLOCK, ML_LANE_BLOCK), const3),
                  pl.BlockSpec((nblk, ML_LANE_BLOCK, ML_LANE_BLOCK), const3),
                  pl.BlockSpec((nblk, ML_LANE_BLOCK, ML_LANE_BLOCK), const3),
                  pl.BlockSpec((3, inner, LANES), const3),
                  pl.BlockSpec((1, LANES), lambda b, t: (0, 0))],
        out_specs=(act_spec, pl.BlockSpec((None, inner, tt), lambda b, t: (b, 0, t)), act_spec, act_spec,
                   pl.BlockSpec((None, tt, LANES), lambda b, t: (b, t, 0))),
        scratch_shapes=[pltpu.VMEM((tt + 2 * halo, ML_LANE_BLOCK), F32)],
        compiler_params=_params(("parallel", "parallel")),
        name="mlstm_pre",
    )(xz, xz, xz, conv_w, conv_b.reshape(1, inner), _block_diag_dense(wq), _block_diag_dense(wk),
      _block_diag_dense(wv), wif, bif)


def _log_sigmoid(x):
    return jnp.minimum(x, 0.0) - jnp.log(1.0 + jnp.exp(-jnp.abs(x)))


def _mlstm_direction(q, kt, v, li_r, lf_r, lf_c, S, m_ref, o_ref, *, reverse, L, dq):
    scale = dq ** -0.5
    lf_r = _log_sigmoid(lf_r)
    lf_c = _log_sigmoid(lf_c)
    row = lax.broadcasted_iota(jnp.int32, (L, L), 0)
    col = lax.broadcasted_iota(jnp.int32, (L, L), 1)
    vis = (col >= row) if reverse else (col <= row)
    vis_t = (row >= col) if reverse else (row <= col)
    b_c = jnp.sum(jnp.where(vis, lf_r, 0.0), axis=1, keepdims=True)
    b_r = jnp.sum(jnp.where(vis_t, lf_c, 0.0), axis=0, keepdims=True)
    g = jnp.sum(lf_r, axis=1, keepdims=True)
    m = m_ref[...]
    ones_col = (lax.broadcasted_iota(jnp.int32, (L, LANES), 1) == 0).astype(BF16)
    v_aug = jnp.concatenate([v, ones_col], axis=1)

    dlog = jnp.where(vis, b_c - b_r + li_r, -jnp.inf)
    m_inter = b_c + m
    m_t = jnp.maximum(m_inter, jnp.max(dlog, axis=1, keepdims=True))
    s = jnp.dot(q, kt, preferred_element_type=F32) * (jnp.exp(dlog - m_t) * scale)
    inter = jnp.exp(m_inter - m_t)
    out = jnp.dot(s.astype(BF16), v_aug, preferred_element_type=F32)
    out = out + jnp.dot(q, S[...].astype(BF16), preferred_element_type=F32) * inter
    den = out[:, dq:dq + 1]
    o_ref[...] = (out[:, :dq] / jnp.maximum(jnp.abs(den), jnp.exp(-m_t))).astype(o_ref.dtype)

    a = g - b_r + li_r
    m_new = jnp.maximum(g + m, jnp.max(a, axis=1, keepdims=True))
    w = jnp.exp(a - m_new) * scale
    decay = jnp.exp(g + m - m_new)
    S[...] = decay * S[...] + jnp.dot(kt * w.astype(BF16), v_aug, preferred_element_type=F32)
    m_ref[...] = m_new


def _mlstm_body(qf_ref, kf_ref, vf_ref, qb_ref, kb_ref, vb_ref, gr_f_ref, gr_b_ref, gc_f_ref, gc_b_ref,
                of_ref, ob_ref, S, m_ref, *, L, dq, heads):
    h0 = pl.program_id(1) * ML_HEADS_PER_STEP

    @pl.when(pl.program_id(2) == 0)
    def _():
        S[...] = jnp.zeros_like(S)
        m_ref[...] = jnp.zeros_like(m_ref)

    lane = lax.broadcasted_iota(jnp.int32, (L, LANES), 1)
    streams = ((qf_ref, kf_ref, vf_ref, gr_f_ref, gc_f_ref, of_ref),
               (qb_ref, kb_ref, vb_ref, gr_b_ref, gc_b_ref, ob_ref))
    for d, (q_ref, kt_ref, v_ref, gr_ref, gc_ref, o_ref) in enumerate(streams):
        for hh in range(ML_HEADS_PER_STEP):
            sl = slice(hh * dq, (hh + 1) * dq)
            i_idx = d * 2 * heads + h0 + hh
            f_idx = i_idx + heads
            lf_c = jnp.sum(jnp.where(lane == f_idx, gc_ref[...], 0.0), axis=1, keepdims=True)
            _mlstm_direction(q_ref[:, sl], kt_ref[sl, :], v_ref[:, sl],
                             gr_ref[pl.ds(i_idx, 1), :], gr_ref[pl.ds(f_idx, 1), :], lf_c,
                             S.at[d, hh], m_ref.at[d, hh], o_ref.at[:, sl], reverse=bool(d), L=L, dq=dq)


def _mlstm(q, kt, v, pre, *, L):
    B, T, inner = q.shape
    H = ML_HEADS
    dh = inner // H
    n = T // L
    n_gate = 4 * H
    gate_rows = jnp.transpose(pre[:, :, :n_gate], (0, 2, 1))

    fwd = lambda c: c
    bwd = lambda c: n - 1 - c
    hps = ML_HEADS_PER_STEP
    qv_spec = lambda at: pl.BlockSpec((None, L, hps * dh), lambda b, h, c: (b, at(c), h))
    kt_spec = lambda at: pl.BlockSpec((None, hps * dh, L), lambda b, h, c: (b, h, at(c)))
    row_spec = lambda at: pl.BlockSpec((None, n_gate, L), lambda b, h, c: (b, 0, at(c)))
    col_spec = lambda at: pl.BlockSpec((None, L, LANES), lambda b, h, c: (b, at(c), 0))
    out = jax.ShapeDtypeStruct((B, T, inner), BF16)
    body = functools.partial(_mlstm_body, L=L, dq=dh, heads=H)
    return pl.pallas_call(
        body,
        out_shape=(out, out),
        grid=(B, H // hps, n),
        in_specs=[qv_spec(fwd), kt_spec(fwd), qv_spec(fwd), qv_spec(bwd), kt_spec(bwd), qv_spec(bwd),
                  row_spec(fwd), row_spec(bwd), col_spec(fwd), col_spec(bwd)],
        out_specs=(qv_spec(fwd), qv_spec(bwd)),
        scratch_shapes=[pltpu.VMEM((2, hps, dh, dh + LANES), F32), pltpu.VMEM((2, hps, 1, 1), F32)],
        compiler_params=_params(("parallel", "parallel", "arbitrary")),
        name="mlstm_chunkwise",
    )(q, kt, v, q, kt, v, gate_rows, gate_rows, pre, pre)


def _mlstm_post_body(hf_ref, hb_ref, xc_ref, z_ref, ng_ref, skip_ref, o_ref, *, dh, inner):
    for h in range(inner // dh):
        sl = slice(h * dh, (h + 1) * dh)
        ht = hf_ref[:, sl].astype(F32) + hb_ref[:, sl].astype(F32)
        mu = jnp.mean(ht, axis=-1, keepdims=True)
        var = jnp.mean(jnp.square(ht - mu), axis=-1, keepdims=True)
        hn = (ht - mu) * lax.rsqrt(var + LN_EPS) * ng_ref[:, sl]
        out = (hn + skip_ref[:, sl] * xc_ref[:, sl].astype(F32)) * jax.nn.sigmoid(z_ref[:, sl].astype(F32))
        o_ref[:, sl] = out.astype(o_ref.dtype)


def _mlstm_post(hf, hb, xc, xz, norm_g, skip, *, tt):
    B, T, inner = hf.shape
    dh = inner // ML_HEADS
    body = functools.partial(_mlstm_post_body, dh=dh, inner=inner)
    vec = pl.BlockSpec((1, inner), lambda b, t: (0, 0))
    return pl.pallas_call(
        body,
        out_shape=jax.ShapeDtypeStruct((B, T, inner), BF16),
        grid=(B, T // tt),
        in_specs=[pl.BlockSpec((None, tt, inner), lambda b, t: (b, t, 0)),
                  pl.BlockSpec((None, tt, inner), lambda b, t: (b, t, 0)),
                  pl.BlockSpec((None, tt, inner), lambda b, t: (b, t, 0)),
                  pl.BlockSpec((None, tt, inner), lambda b, t: (b, t, 1)),
                  vec, vec],
        out_specs=pl.BlockSpec((None, tt, inner), lambda b, t: (b, t, 0)),
        compiler_params=_params(("parallel", "parallel")),
        name="mlstm_post",
    )(hf, hb, xc, xz, norm_g.reshape(1, inner), skip.reshape(1, inner))


def _pack_bf16_pairs(x):
    half = x.shape[1] // 2
    bits = pltpu.bitcast(x.astype(BF16).astype(F32), jnp.uint32)
    return (bits[:, :half] >> 16) | bits[:, half:]


def _unpack_bf16_pairs(w):
    lo = pltpu.bitcast(w << 16, F32).astype(BF16)
    hi = pltpu.bitcast(w & jnp.uint32(0xFFFF0000), F32).astype(BF16)
    return lo, hi


def _router_body(x_ref, g_ref, sc_ref, sh_ref, r_ref, h_ref, idx_ref, wt_ref, *, n_experts):
    h = _norm_mod(x_ref[...], g_ref[...], sc_ref[...], sh_ref[...])
    packed = _pack_bf16_pairs(h)
    n_lane_blocks = packed.shape[1] // LANES
    for c in range(n_lane_blocks):
        h_ref[pl.ds(c, packed.shape[0], stride=n_lane_blocks), :] = packed[:, c * LANES:(c + 1) * LANES]
    r = r_ref[...]
    h_hi, r_hi = h.astype(BF16), r.astype(BF16)
    h_lo = (h - h_hi.astype(F32)).astype(BF16)
    r_lo = (r - r_hi.astype(F32)).astype(BF16)
    logits = (jnp.dot(h_hi, r_hi, preferred_element_type=F32) + jnp.dot(h_lo, r_hi, preferred_element_type=F32)
              + jnp.dot(h_hi, r_lo, preferred_element_type=F32))
    lane = lax.broadcasted_iota(jnp.int32, logits.shape, 1)
    lg = jnp.where(lane < n_experts, logits, -jnp.inf)
    m1 = jnp.max(lg, axis=1, keepdims=True)
    i1 = jnp.min(jnp.where(lg == m1, lane, LANES), axis=1, keepdims=True)
    lg2 = jnp.where(lane == i1, -jnp.inf, lg)
    m2 = jnp.max(lg2, axis=1, keepdims=True)
    i2 = jnp.min(jnp.where(lg2 == m2, lane, LANES), axis=1, keepdims=True)
    e2 = jnp.exp(m2 - m1)
    w1 = 1.0 / (1.0 + e2)
    w2 = e2 * w1
    choices = jnp.where(lane == 0, i1, jnp.where(lane == 1, i2, 0)).astype(F32)
    idx_ref[...] = jnp.transpose(choices)[:idx_ref.shape[0], :].astype(jnp.int32)
    wt_ref[...] = jnp.where(lane == 0, w1, jnp.where(lane == 1, w2, 0.0))


def _router(x, g, layer, mod, router, *, tt):
    B, T, D = x.shape
    E = router.shape[1]
    r_pad = jnp.pad(router, ((0, 0), (0, LANES - E)))
    body = functools.partial(_router_body, n_experts=E)
    small = pl.BlockSpec((None, tt, LANES), lambda b, t: (b, t, 0))
    return pl.pallas_call(
        body,
        out_shape=(jax.ShapeDtypeStruct((B, T * (D // 2 // LANES), LANES), jnp.uint32),
                   jax.ShapeDtypeStruct((B, T // tt, 8, tt), jnp.int32),
                   jax.ShapeDtypeStruct((B, T, LANES), F32)),
        grid=(B, T // tt),
        in_specs=[pl.BlockSpec((None, tt, D), lambda b, t: (b, t, 0)),
                  pl.BlockSpec((None, 1, D), lambda b, t: (layer, 0, 0)),
                  _mod_spec(4, D), _mod_spec(3, D),
                  pl.BlockSpec((D, LANES), lambda b, t: (0, 0))],
        out_specs=(pl.BlockSpec((None, tt * (D // 2 // LANES), LANES), lambda b, t: (b, t, 0)),
                   pl.BlockSpec((None, None, 8, tt), lambda b, t: (b, t, 0, 0)), small),
        compiler_params=_params(("parallel", "parallel")),
        name="moe_router",
    )(x, g, mod, mod, r_pad)


def _route_tables(top_i, n_experts, n_tiles, n_slots):
    n_pairs = top_i.size
    e_flat = top_i.reshape(n_pairs)
    onehot = (e_flat[:, None] == jnp.arange(n_experts, dtype=jnp.int32)[None, :]).astype(jnp.int32)
    csum = jnp.cumsum(onehot, axis=0)
    counts = csum[-1]
    rank = jnp.sum(csum * onehot, axis=1) - 1
    nsub = (counts + FFN_SUB - 1) // FFN_SUB
    goff = jnp.cumsum(nsub) - nsub
    pos = jnp.sum(onehot * goff[None, :], axis=1) * FFN_SUB + rank
    src = jnp.zeros((n_slots,), jnp.int32).at[pos].set(jnp.arange(n_pairs, dtype=jnp.int32) // TOP_K)

    spr = FFN_ROWS // FFN_SUB
    ntile = (nsub + spr - 1) // spr
    per_tile = (nsub + jnp.maximum(ntile, 1) - 1) // jnp.maximum(ntile, 1)
    tcum = jnp.cumsum(ntile)
    toff = tcum - ntile
    ids = jnp.arange(n_tiles, dtype=jnp.int32)
    te = jnp.minimum(jnp.sum((ids[:, None] >= tcum[None, :]).astype(jnp.int32), axis=1), n_experts - 1)
    valid = ids < tcum[-1]
    j = ids - toff[te]
    t_start = goff[te] + j * per_tile[te]
    t_nsub = jnp.minimum(per_tile[te], nsub[te] - j * per_tile[te])
    last_e = te[jnp.maximum(tcum[-1] - 1, 0)]
    tile_e = jnp.where(valid, te, last_e).astype(jnp.int32)
    tile_start = jnp.where(valid, t_start, 0).astype(jnp.int32)
    tile_nsub = jnp.where(valid, t_nsub, 0).astype(jnp.int32)
    n_used_sub = jnp.sum(nsub).astype(jnp.int32).reshape(1)
    return pos.astype(jnp.int32), src, tile_e, tile_start, tile_nsub, n_used_sub


def _dense_tiles(n_rows):
    nsub = n_rows // FFN_SUB
    spr = FFN_ROWS // FFN_SUB
    ntile = -(-nsub // spr)
    per_tile = -(-nsub // ntile)
    starts = [j * per_tile for j in range(ntile)]
    counts = [min(per_tile, nsub - st) for st in starts]
    return (jnp.zeros((ntile,), jnp.int32), jnp.array(starts, jnp.int32), jnp.array(counts, jnp.int32))


def _dispatch_body(src_ref, h_ref, o_ref, buf, sem, *, chunk, slab):
    i = pl.program_id(0)

    def gather(step, slot):
        base = step * chunk

        def issue(r, carry):
            src_row = pl.multiple_of(src_ref[base + r] * slab, slab)
            dst_row = pl.multiple_of(r * slab, slab)
            pltpu.make_async_copy(h_ref.at[pl.ds(src_row, slab)], buf.at[slot, pl.ds(dst_row, slab)],
                                  sem.at[slot]).start()
            return carry

        lax.fori_loop(0, chunk, issue, 0, unroll=8)

    @pl.when(i == 0)
    def _():
        gather(0, 0)

    @pl.when(i + 1 < pl.num_programs(0))
    def _():
        gather(i + 1, (i + 1) % 2)

    slot = i % 2
    pltpu.make_async_copy(h_ref.at[pl.ds(0, chunk * slab)], buf.at[slot], sem.at[slot]).wait()
    half = slab * LANES
    for c in range(slab):
        lo, hi = _unpack_bf16_pairs(buf[slot, pl.ds(c, chunk, stride=slab), :])
        o_ref[:, c * LANES:(c + 1) * LANES] = lo
        o_ref[:, half + c * LANES:half + (c + 1) * LANES] = hi


def _dispatch(src, h, n_slots, *, chunk, slab):
    D = 2 * slab * LANES
    body = functools.partial(_dispatch_body, chunk=chunk, slab=slab)
    return pl.pallas_call(
        body,
        out_shape=jax.ShapeDtypeStruct((n_slots, D), BF16),
        grid_spec=pltpu.PrefetchScalarGridSpec(
            num_scalar_prefetch=1,
            grid=(n_slots // chunk,),
            in_specs=[pl.BlockSpec(memory_space=pl.ANY)],
            out_specs=pl.BlockSpec((chunk, D), lambda i, src: (i, 0)),
            scratch_shapes=[pltpu.VMEM((2, chunk * slab, LANES), jnp.uint32), pltpu.SemaphoreType.DMA((2,))]),
        compiler_params=_params(("arbitrary",)),
        name="moe_dispatch",
    )(src, h)


def _expert_body(te_ref, ts_ref, tn_ref, used_ref, xs_ref, wg_ref, wu_ref, wd_ref, y_ref,
                 xb, acc, wgb, wub, wdb, sem_in, sem_out, *, nf):
    s = pl.program_id(0)
    f = pl.program_id(1)
    nsub = tn_ref[s]
    start = ts_ref[s]
    d_model = acc.shape[1]

    def local_rows(j, n=1):
        return pl.ds(pl.multiple_of(j * FFN_SUB, FFN_SUB), n * FFN_SUB)

    def hbm_rows(j):
        return pl.ds(pl.multiple_of((start + j) * FFN_SUB, FFN_SUB), FFN_SUB)

    def in_copy(j):
        return pltpu.make_async_copy(xs_ref.at[hbm_rows(j)], xb.at[local_rows(j)], sem_in)

    def out_copy(j):
        return pltpu.make_async_copy(acc.at[local_rows(j)], y_ref.at[hbm_rows(j)], sem_out)

    def for_each_sub(lo, hi, fn):
        def step(j, carry):
            fn(j)
            return carry

        lax.fori_loop(lo, hi, step, 0)

    def zero_acc(j):
        acc[local_rows(j), :] = jnp.zeros((FFN_SUB, d_model), F32)

    @pl.when((s == 0) & (f == 0))
    def _():
        def fill_copy(j):
            row = pl.multiple_of(j * FFN_SUB, FFN_SUB)
            return pltpu.make_async_copy(acc.at[local_rows(0)], y_ref.at[pl.ds(row, FFN_SUB)], sem_out)

        n_sub_total = y_ref.shape[0] // FFN_SUB
        zero_acc(0)
        for_each_sub(used_ref[0], n_sub_total, lambda j: fill_copy(j).start())
        for_each_sub(used_ref[0], n_sub_total, lambda j: fill_copy(j).wait())

    @pl.when(nsub > 0)
    def _():
        @pl.when(f == 0)
        def _():
            for_each_sub(0, nsub, lambda j: in_copy(j).start())
            for_each_sub(0, nsub, zero_acc)
            for_each_sub(0, nsub, lambda j: in_copy(j).wait())

        def block(j0, n, cast_weights=False):
            if cast_weights:
                wg, wu, wd = (r[...].astype(BF16) for r in (wg_ref, wu_ref, wd_ref))
                wgb[...], wub[...], wdb[...] = wg, wu, wd
            else:
                wg, wu, wd = wgb[...], wub[...], wdb[...]
            rows = local_rows(j0, n)
            x = xb[rows, :]
            hg = jnp.dot(x, wg, preferred_element_type=F32)
            hu = jnp.dot(x, wu, preferred_element_type=F32)
            hid = (hg * jax.nn.sigmoid(hg) * hu).astype(BF16)
            acc[rows, :] += jnp.dot(hid, wd, preferred_element_type=F32)

            @pl.when(f == nf - 1)
            def _():
                for u in range(n):
                    out_copy(j0 + u).start()

        big = FFN_BLOCK_SUBS[0]
        n_big = nsub // big

        @pl.when(n_big > 0)
        def _():
            block(0, big, cast_weights=True)

        for_each_sub(1, n_big, lambda p: block(p * big, big))
        off = n_big * big
        started = n_big > 0
        for size in FFN_BLOCK_SUBS[1:]:
            has = ((nsub - off) // size) > 0

            @pl.when(has & jnp.logical_not(started))
            def _():
                block(0, size, cast_weights=True)

            @pl.when(has & started)
            def _():
                block(off, size)

            off = off + jnp.where(has, size, 0)
            started = started | has

        @pl.when(f == nf - 1)
        def _():
            for_each_sub(0, nsub, lambda j: out_copy(j).wait())


def _grouped_swiglu(tile_e, tile_start, tile_nsub, n_used_sub, xs, w_gate, w_up, w_down):
    P, D = xs.shape
    F = w_gate.shape[2]
    nf = F // FFN_TF
    n_tiles = tile_e.shape[0]

    def f_eff(s, f, tn):
        return jnp.where(tn[s] > 0, f, nf - 1)

    body = functools.partial(_expert_body, nf=nf)
    return pl.pallas_call(
        body,
        out_shape=jax.ShapeDtypeStruct((P, D), F32),
        grid_spec=pltpu.PrefetchScalarGridSpec(
            num_scalar_prefetch=4,
            grid=(n_tiles, nf),
            in_specs=[pl.BlockSpec(memory_space=pl.ANY),
                      pl.BlockSpec((None, D, FFN_TF), lambda s, f, te, ts, tn, used: (te[s], 0, f_eff(s, f, tn))),
                      pl.BlockSpec((None, D, FFN_TF), lambda s, f, te, ts, tn, used: (te[s], 0, f_eff(s, f, tn))),
                      pl.BlockSpec((None, FFN_TF, D), lambda s, f, te, ts, tn, used: (te[s], f_eff(s, f, tn), 0))],
            out_specs=pl.BlockSpec(memory_space=pl.ANY),
            scratch_shapes=[pltpu.VMEM((FFN_ROWS, D), BF16),
                            pltpu.VMEM((FFN_ROWS, D), F32),
                            pltpu.VMEM((D, FFN_TF), BF16),
                            pltpu.VMEM((D, FFN_TF), BF16),
                            pltpu.VMEM((FFN_TF, D), BF16),
                            pltpu.SemaphoreType.DMA,
                            pltpu.SemaphoreType.DMA]),
        compiler_params=_params(("arbitrary", "arbitrary")),
        name="grouped_swiglu",
    )(tile_e, tile_start, tile_nsub, n_used_sub, xs, w_gate, w_up, w_down)


def _combine_body(pos_ref, y_ref, x_ref, w_ref, g_ref, o_ref, ybuf, sem, *, tt):
    i = pl.program_id(0)

    def gather(step, slot):
        base = step * (tt * TOP_K)

        def issue(r, carry):
            for k in range(TOP_K):
                pltpu.make_async_copy(y_ref.at[pl.ds(pos_ref[base + r * TOP_K + k], 1)],
                                      ybuf.at[slot, k, pl.ds(r, 1)], sem.at[slot]).start()
            return carry

        lax.fori_loop(0, tt, issue, 0, unroll=4)

    @pl.when(i == 0)
    def _():
        gather(0, 0)

    @pl.when(i + 1 < pl.num_programs(0))
    def _():
        gather(i + 1, (i + 1) % 2)

    slot = i % 2
    for k in range(TOP_K):
        pltpu.make_async_copy(y_ref.at[pl.ds(0, tt)], ybuf.at[slot, k], sem.at[slot]).wait()
    w = w_ref[...]
    y = w[:, 0:1] * ybuf[slot, 0]
    for k in range(1, TOP_K):
        y = y + w[:, k:k + 1] * ybuf[slot, k]
    o_ref[...] = x_ref[...] + g_ref[...] * y


def _combine(pos, y, x, wts, mod, *, tt):
    N, D = x.shape
    tiles_per_batch = N // mod.shape[0] // tt
    body = functools.partial(_combine_body, tt=tt)
    return pl.pallas_call(
        body,
        out_shape=jax.ShapeDtypeStruct((N, D), F32),
        grid_spec=pltpu.PrefetchScalarGridSpec(
            num_scalar_prefetch=1,
            grid=(N // tt,),
            in_specs=[pl.BlockSpec(memory_space=pl.ANY),
                      pl.BlockSpec((tt, D), lambda i, p: (i, 0)),
                      pl.BlockSpec((tt, LANES), lambda i, p: (i, 0)),
                      pl.BlockSpec((None, None, 1, D), lambda i, p: (i // tiles_per_batch, 5, 0, 0))],
            out_specs=pl.BlockSpec((tt, D), lambda i, p: (i, 0)),
            scratch_shapes=[pltpu.VMEM((2, TOP_K, tt, D), F32), pltpu.SemaphoreType.DMA((2,))]),
        compiler_params=_params(("arbitrary",)),
        name="moe_combine",
    )(pos, y, x, wts, mod)


def _residual_normmod_body(x_ref, y_ref, gate_ref, g_ref, sc_ref, sh_ref, x_out_ref, h_ref):
    x_new = x_ref[...] + gate_ref[...] * y_ref[...]
    x_out_ref[...] = x_new
    h_ref[...] = _norm_mod(x_new, g_ref[...], sc_ref[...], sh_ref[...]).astype(h_ref.dtype)


def _residual_normmod(x, y, mod_prev, k_gate, g, layer, mod_next, k_scale, k_shift, *, tt, name):
    B, T, D = x.shape
    spec = pl.BlockSpec((None, tt, D), lambda b, t: (b, t, 0))
    return pl.pallas_call(
        _residual_normmod_body,
        out_shape=(jax.ShapeDtypeStruct((B, T, D), F32), jax.ShapeDtypeStruct((B, T, D), BF16)),
        grid=(B, T // tt),
        in_specs=[spec, spec, _mod_spec(k_gate, D),
                  pl.BlockSpec((None, 1, D), lambda b, t: (layer, 0, 0)),
                  _mod_spec(k_scale, D), _mod_spec(k_shift, D)],
        out_specs=(spec, spec),
        compiler_params=_params(("parallel", "parallel")),
        name=name,
    )(x, y, mod_prev, g, mod_next, mod_next)


def kernel(x, c, mod_w, mod_b, norm_mix_g, norm_ffn_g, na_w_in, na_q_norm_g, na_k_norm_g, na_rpb, na_w_out,
           ffn_w_gate, ffn_w_up, ffn_w_down, ml_w_in, ml_conv_w, ml_conv_b, ml_wq, ml_wk, ml_wv, ml_w_if,
           ml_b_if, ml_skip, ml_norm_g, ml_w_out, moe_router, moe_w_gate, moe_w_up, moe_w_down):
    B, T, D = x.shape
    N = B * T
    depth = mod_w.shape[0]
    c_pad = jnp.pad(c, ((0, 8 - B), (0, 0)))
    mix_g = norm_mix_g.reshape(depth, 1, D)
    ffn_g = norm_ffn_g.reshape(depth, 1, D)

    def modulation(i):
        mod = _matmul(c_pad, mod_w, i, tn=1024, tm=8, out_dtype=F32, a_silu=True, bias=mod_b,
                      name="adaln_modulation")
        return mod[:B].reshape(B, 6, 1, D)

    mod = modulation(0)
    h = _normmod(x, mix_g, 0, mod, 1, 0, tt=512, out_dtype=BF16, name="norm_mix0")
    na_dh = D // NA_HEADS
    qk_gain = jnp.concatenate([jnp.tile(na_q_norm_g[0] * na_dh ** -0.5, NA_HEADS), jnp.tile(na_k_norm_g[0], NA_HEADS),
                               jnp.ones((D,), F32)]).reshape(1, 3 * D)
    qkv = _matmul(h.reshape(N, D), na_w_in, 0, tn=1024, tm=MM_TM, out_dtype=BF16, head_gain=qk_gain,
                  norm_cols=2 * D, head_dim=na_dh, name="na_qkv")
    att = _neighborhood_attention(qkv.reshape(B, T, 3 * D), na_rpb[0])
    x = _matmul(att.reshape(N, D), na_w_out, 0, tn=1024, tm=MM_TM, out_dtype=F32, res=x.reshape(N, D),
                gate=mod, gate_idx=2, rows_per_batch=T, name="na_out").reshape(B, T, D)

    h = _normmod(x, ffn_g, 0, mod, 4, 3, tt=512, out_dtype=BF16, name="norm_ffn0")
    dense_e, dense_start, dense_nsub = _dense_tiles(N)
    y = _grouped_swiglu(dense_e, dense_start, dense_nsub, jnp.full((1,), N // FFN_SUB, jnp.int32),
                        h.reshape(N, D), ffn_w_gate, ffn_w_up, ffn_w_down)

    mod0, mod = mod, modulation(1)
    x, h = _residual_normmod(x, y.reshape(B, T, D), mod0, 5, mix_g, 1, mod, 1, 0, tt=512,
                             name="ffn0_residual_norm_mix1")
    inner = ml_w_in.shape[2] // 2
    xz = _matmul(h.reshape(N, D), ml_w_in, 0, tn=1024, tm=MM_TM, out_dtype=BF16,
                 name="mlstm_in").reshape(B, T, 2 * inner)
    q, kt, v, xc, pre = _mlstm_pre(xz, ml_conv_w[0], ml_conv_b[0], ml_wq[0], ml_wk[0], ml_wv[0],
                                  ml_w_if[0], ml_b_if[0], tt=256)
    hf, hb = _mlstm(q, kt, v, pre, L=ML_CHUNK)
    u = _mlstm_post(hf, hb, xc, xz, ml_norm_g[0], ml_skip[0], tt=256)
    x = _matmul(u.reshape(N, inner), ml_w_out, 0, tn=512, tm=MM_TM, out_dtype=F32, res=x.reshape(N, D),
                gate=mod, gate_idx=2, rows_per_batch=T, name="mlstm_out").reshape(B, T, D)

    h2, top_idx, top_w = _router(x, ffn_g, 1, mod, moe_router[0], tt=256)
    n_slots = N * TOP_K + N_EXPERTS * FFN_SUB
    n_tiles = -(-N * TOP_K // FFN_ROWS) + N_EXPERTS
    pos, src, tile_e, tile_start, tile_nsub, n_used_sub = _route_tables(
        jnp.moveaxis(top_idx[:, :, :TOP_K, :], 2, 3).reshape(N, TOP_K), N_EXPERTS, n_tiles, n_slots)
    slab = D // 2 // LANES
    xs = _dispatch(src, h2.reshape(N * slab, LANES), n_slots, chunk=DISPATCH_CHUNK, slab=slab)
    ys = _grouped_swiglu(tile_e, tile_start, tile_nsub, n_used_sub, xs, moe_w_gate[0], moe_w_up[0], moe_w_down[0])
    return _combine(pos, ys, x.reshape(N, D), top_w.reshape(N, LANES), mod, tt=256).reshape(B, T, D)
```

```python
import functools

import jax
import jax.numpy as jnp
from jax import lax
from jax.experimental import pallas as pl
from jax.experimental.pallas import tpu as pltpu

F32 = jnp.float32
BF16 = jnp.bfloat16

GRID_W = 64
NA_HEADS = 16
NA_WIN_ROWS_MAX = 8
NA_WIN_COLS = 16
ML_HEADS = 8
ML_QKV_BLOCK = 4
ML_CONV_K = 5
N_EXPERTS = 8
TOP_K = 2
RMS_EPS = 1e-6
LN_EPS = 1e-5

V7X_VMEM_LIMIT_BYTES = 56 * 1024 * 1024
LANES = 128
BF16_SUBLANES = 16

MM_TM = 1024
ML_CHUNK = 256
ML_HEADS_PER_STEP = 4
ML_LANE_BLOCK = 128
FFN_SUB = 128
FFN_ROWS = 3072
FFN_BLOCK_SUBS = (8, 4, 2, 1)
DISPATCH_CHUNK = 512
COMBINE_TOKENS = 512
FFN_TF = 256
MASK_NEG = -1e30
NA_ROW_GROUP = 8


def _params(sem):
    return pltpu.CompilerParams(dimension_semantics=sem, vmem_limit_bytes=V7X_VMEM_LIMIT_BYTES)


def _mm_body(*refs, a_silu, has_bias, has_res, norm_blocks, head_dim):
    a_ref, w_ref = refs[0], refs[1]
    k = 2
    if has_bias:
        b_ref = refs[k]
        k += 1
    if has_res:
        r_ref, g_ref = refs[k], refs[k + 1]
        k += 2
    if norm_blocks:
        hg_ref = refs[k]
        k += 1
    o_ref, wb_ref = refs[k], refs[k + 1]
    j = pl.program_id(0)
    i = pl.program_id(1)

    def finish(w):
        a = a_ref[...]
        if a_silu:
            a = a * jax.nn.sigmoid(a)
        y = jnp.dot(a.astype(BF16), w, preferred_element_type=F32)
        if has_bias:
            y = y + b_ref[...]
        if has_res:
            y = r_ref[...] + g_ref[...] * y
        if not norm_blocks:
            o_ref[...] = y.astype(o_ref.dtype)
            return

        normed = j < norm_blocks
        for h in range(y.shape[1] // head_dim):
            sl = slice(h * head_dim, (h + 1) * head_dim)
            yh = y[:, sl]
            inv = lax.rsqrt(jnp.mean(yh * yh, axis=-1, keepdims=True) + RMS_EPS)
            o_ref[:, sl] = (yh * (jnp.where(normed, inv, 1.0) * hg_ref[:, sl])).astype(o_ref.dtype)

    @pl.when(i == 0)
    def _():
        w = w_ref[...].astype(BF16)
        wb_ref[...] = w
        finish(w)

    @pl.when(i > 0)
    def _():
        finish(wb_ref[...])


def _matmul(a, w, layer, *, tn, tm, out_dtype, a_silu=False, bias=None, res=None, gate=None,
            gate_idx=0, rows_per_batch=None, head_gain=None, norm_cols=0, head_dim=None, name="matmul"):
    M, K = a.shape
    N = w.shape[2]
    grid = (N // tn, M // tm)
    in_specs = [pl.BlockSpec((tm, K), lambda j, i: (i, 0)),
                pl.BlockSpec((None, K, tn), lambda j, i: (layer, 0, j))]
    args = [a, w]
    if bias is not None:
        in_specs.append(pl.BlockSpec((None, 1, tn), lambda j, i: (layer, 0, j)))
        args.append(bias.reshape(bias.shape[0], 1, N))
    if res is not None:
        tiles_per_batch = rows_per_batch // tm
        in_specs.append(pl.BlockSpec((tm, tn), lambda j, i: (i, j)))
        in_specs.append(pl.BlockSpec((None, None, 1, tn),
                                     lambda j, i: (i // tiles_per_batch, gate_idx, 0, j)))
        args += [res, gate]
    if head_gain is not None:
        in_specs.append(pl.BlockSpec((1, tn), lambda j, i: (0, j)))
        args.append(head_gain)
    body = functools.partial(_mm_body, a_silu=a_silu, has_bias=bias is not None,
                             has_res=res is not None, norm_blocks=norm_cols // tn, head_dim=head_dim)
    return pl.pallas_call(
        body,
        out_shape=jax.ShapeDtypeStruct((M, N), out_dtype),
        grid=grid,
        in_specs=in_specs,
        out_specs=pl.BlockSpec((tm, tn), lambda j, i: (i, j)),
        scratch_shapes=[pltpu.VMEM((K, tn), BF16)],
        compiler_params=_params(("parallel", "arbitrary")),
        name=name,
    )(*args)


def _norm_mod(x, g, sc, sh):
    y = x * lax.rsqrt(jnp.mean(x * x, axis=-1, keepdims=True) + RMS_EPS) * g
    return y * (1.0 + sc) + sh


def _normmod_body(x_ref, g_ref, sc_ref, sh_ref, o_ref):
    o_ref[...] = _norm_mod(x_ref[...], g_ref[...], sc_ref[...], sh_ref[...]).astype(o_ref.dtype)


def _mod_spec(k, D):
    return pl.BlockSpec((None, None, 1, D), lambda b, t: (b, k, 0, 0))


def _normmod(x, g, layer, mod, k_scale, k_shift, *, tt, out_dtype, name):
    B, T, D = x.shape
    return pl.pallas_call(
        _normmod_body,
        out_shape=jax.ShapeDtypeStruct((B, T, D), out_dtype),
        grid=(B, T // tt),
        in_specs=[pl.BlockSpec((None, tt, D), lambda b, t: (b, t, 0)),
                  pl.BlockSpec((None, 1, D), lambda b, t: (layer, 0, 0)),
                  _mod_spec(k_scale, D), _mod_spec(k_shift, D)],
        out_specs=pl.BlockSpec((None, tt, D), lambda b, t: (b, t, 0)),
        compiler_params=_params(("parallel", "parallel")),
        name=name,
    )(x, g, mod, mod)


def _na_body(q_ref, k_ref, v_ref, bias_ref, o_ref, *, rows, width, kh):
    def row_group(gi, carry):
        offs, scores, probs = [], [], []
        for u in range(NA_ROW_GROUP):
            r = gi * NA_ROW_GROUP + u
            r0 = jnp.clip(r - kh // 2, 0, rows - kh)
            q_off = pl.multiple_of(r * width, width)
            b_off = pl.multiple_of(r0 * width, width)
            q_r = q_ref[pl.ds(q_off, width), :]
            k_band = k_ref[pl.ds(b_off, kh * width), :]
            s = lax.dot_general(q_r, k_band, (((1,), (1,)), ((), ())), preferred_element_type=F32)
            rel = r0 - r + NA_WIN_ROWS_MAX - 1
            bias = jnp.concatenate([bias_ref[rel + i, 0] + bias_ref[rel + i + 1, 1] for i in range(0, kh, 2)],
                                   axis=1)
            scores.append(s + bias)
            offs.append((q_off, b_off))
        for s in scores:
            e = jnp.exp(s - jnp.max(s, axis=-1, keepdims=True))
            probs.append((e.astype(BF16), jnp.sum(e, axis=-1, keepdims=True)))
        for (q_off, b_off), (e, l) in zip(offs, probs):
            v_band = v_ref[pl.ds(b_off, kh * width), :]
            o = jnp.dot(e, v_band, preferred_element_type=F32) / l
            o_ref[pl.ds(q_off, width), :] = o.astype(o_ref.dtype)
        return carry

    lax.fori_loop(0, rows // NA_ROW_GROUP, row_group, 0)


def _na_bias_table(rpb):
    col = jnp.arange(GRID_W)
    col_start = jnp.clip(col - NA_WIN_COLS // 2, 0, GRID_W - NA_WIN_COLS)
    col_in = (col[None, :] >= col_start[:, None]) & (col[None, :] < col_start[:, None] + NA_WIN_COLS)
    H, n_di, n_dj = rpb.shape
    far = GRID_W - NA_WIN_COLS
    rpb = rpb.astype(F32)
    circle = jnp.concatenate([rpb[..., NA_WIN_COLS - 1:], jnp.repeat(rpb[..., -1:], far, axis=-1),
                              jnp.repeat(rpb[..., :1], far, axis=-1), rpb[..., :NA_WIN_COLS - 1]], axis=-1)
    pitch = 2 * GRID_W - 2
    toep = jnp.tile(circle, (1, 1, GRID_W))[..., :GRID_W * pitch].reshape(H, n_di, GRID_W, pitch)[..., :GRID_W]
    rpb_cols = jnp.where(col_in[None, None], toep, MASK_NEG)
    zeros = jnp.zeros_like(rpb_cols)
    return jnp.stack([jnp.concatenate([rpb_cols, zeros], axis=-1),
                      jnp.concatenate([zeros, rpb_cols], axis=-1)], axis=2)


def _neighborhood_attention(qkv, rpb):
    B, T, D3 = qkv.shape
    D = D3 // 3
    H = NA_HEADS
    dh = D // H
    rows = T // GRID_W
    kh = min(NA_WIN_ROWS_MAX, rows)
    bias = _na_bias_table(rpb)
    body = functools.partial(_na_body, rows=rows, width=GRID_W, kh=kh)
    return pl.pallas_call(
        body,
        out_shape=jax.ShapeDtypeStruct((B, T, D), BF16),
        grid=(B, H),
        in_specs=[pl.BlockSpec((None, T, dh), lambda b, h: (b, 0, h)),
                  pl.BlockSpec((None, T, dh), lambda b, h: (b, 0, H + h)),
                  pl.BlockSpec((None, T, dh), lambda b, h: (b, 0, 2 * H + h)),
                  pl.BlockSpec((None,) + bias.shape[1:], lambda b, h: (h, 0, 0, 0, 0))],
        out_specs=pl.BlockSpec((None, T, dh), lambda b, h: (b, 0, h)),
        compiler_params=_params(("parallel", "parallel")),
        name="neighborhood_attention",
    )(qkv, qkv, qkv, bias)


def _mlstm_pre_body(xm_ref, xp_ref, xn_ref, cw_ref, cb_ref, wq_ref, wk_ref, wv_ref, wif_ref, bif_ref,
                    q_ref, kt_ref, v_ref, xc_ref, pre_ref, cat_ref, *, tt, inner, halo):
    t = pl.program_id(1)
    first = (t > 0).astype(F32)
    last = (t < pl.num_programs(1) - 1).astype(F32)
    pad = ML_CONV_K // 2
    pre = jnp.zeros((tt, LANES), F32)
    width = ML_LANE_BLOCK
    for c in range(inner // width):
        sl = slice(c * width, (c + 1) * width)
        cur_b = xm_ref[:, sl]
        cat = jnp.concatenate([xp_ref[:, sl].astype(F32) * first, cur_b.astype(F32),
                               xn_ref[:, sl].astype(F32) * last], axis=0)
        cat_ref[...] = cat
        xc = jnp.zeros((tt, width), F32) + cb_ref[:, sl]
        for j in range(ML_CONV_K):
            xc = xc + cat_ref[halo - pad + j:halo - pad + j + tt, :] * cw_ref[j:j + 1, sl]
        xc = xc * jax.nn.sigmoid(xc)
        xc_b = xc.astype(BF16)
        q = jnp.dot(xc_b, wq_ref[c], preferred_element_type=F32).astype(BF16)
        k = jnp.dot(xc_b, wk_ref[c], preferred_element_type=F32).astype(BF16)
        v = jnp.dot(cur_b, wv_ref[c], preferred_element_type=F32).astype(BF16)
        pre = pre + jnp.dot(q, wif_ref[0, sl, :], preferred_element_type=F32)
        pre = pre + jnp.dot(k, wif_ref[1, sl, :], preferred_element_type=F32)
        pre = pre + jnp.dot(v, wif_ref[2, sl, :], preferred_element_type=F32)
        q_ref[:, sl] = q
        kt_ref[sl, :] = k.astype(F32).T.astype(BF16)
        v_ref[:, sl] = v
        xc_ref[:, sl] = xc_b
    pre_ref[...] = pre + bif_ref[...]


def _block_diag_dense(w):
    nb = w.shape[0]
    per = ML_LANE_BLOCK // ML_QKV_BLOCK
    wr = w.reshape(nb // per, per, ML_QKV_BLOCK, ML_QKV_BLOCK)
    eye = jnp.eye(per, dtype=w.dtype)
    dense = jnp.einsum('gnio,nm->gnimo', wr, eye)
    return dense.reshape(nb // per, ML_LANE_BLOCK, ML_LANE_BLOCK).astype(BF16)


def _mlstm_pre(xz, conv_w, conv_b, wq, wk, wv, w_if, b_if, *, tt):
    B, T, inner2 = xz.shape
    inner = inner2 // 2
    halo = BF16_SUBLANES
    n_gate = w_if.shape[1] * w_if.shape[2] * w_if.shape[3]
    wif = jnp.pad(w_if.reshape(3, inner, n_gate), ((0, 0), (0, 0), (0, LANES - n_gate))).astype(BF16)
    bif = jnp.pad(b_if.reshape(1, n_gate), ((0, 0), (0, LANES - n_gate)))
    nblk = inner // ML_LANE_BLOCK
    hb = tt // halo
    nh = T // halo
    body = functools.partial(_mlstm_pre_body, tt=tt, inner=inner, halo=halo)
    act = jax.ShapeDtypeStruct((B, T, inner), BF16)
    act_spec = pl.BlockSpec((None, tt, inner), lambda b, t: (b, t, 0))
    const3 = lambda b, t: (0, 0, 0)
    return pl.pallas_call(
        body,
        out_shape=(act, jax.ShapeDtypeStruct((B, inner, T), BF16), act, act,
                   jax.ShapeDtypeStruct((B, T, LANES), F32)),
        grid=(B, T // tt),
        in_specs=[act_spec,
                  pl.BlockSpec((None, halo, inner), lambda b, t: (b, jnp.maximum(t * hb - 1, 0), 0)),
                  pl.BlockSpec((None, halo, inner), lambda b, t: (b, jnp.minimum((t + 1) * hb, nh - 1), 0)),
                  pl.BlockSpec((ML_CONV_K, inner), lambda b, t: (0, 0)),
                  pl.BlockSpec((1, inner), lambda b, t: (0, 0)),
                  pl.BlockSpec((nblk, ML_LANE_BLOCK, ML_LANE_BLOCK), const3),
                  pl.BlockSpec((nblk, ML_LANE_BLOCK, ML_LANE_BLOCK), const3),
                  pl.BlockSpec((nblk, ML_LANE_BLOCK, ML_LANE_BLOCK), const3),
                  pl.BlockSpec((3, inner, LANES), const3),
                  pl.BlockSpec((1, LANES), lambda b, t: (0, 0))],
        out_specs=(act_spec, pl.BlockSpec((None, inner, tt), lambda b, t: (b, 0, t)), act_spec, act_spec,
                   pl.BlockSpec((None, tt, LANES), lambda b, t: (b, t, 0))),
        scratch_shapes=[pltpu.VMEM((tt + 2 * halo, ML_LANE_BLOCK), F32)],
        compiler_params=_params(("parallel", "parallel")),
        name="mlstm_pre",
    )(xz, xz, xz, conv_w, conv_b.reshape(1, inner), _block_diag_dense(wq), _block_diag_dense(wk),
      _block_diag_dense(wv), wif, bif)


def _log_sigmoid(x):
    return jnp.minimum(x, 0.0) - jnp.log(1.0 + jnp.exp(-jnp.abs(x)))


def _mlstm_direction(q, kt, v, li_r, lf_r, lf_c, S, m_ref, o_ref, *, reverse, L, dq):
    scale = dq ** -0.5
    lf_r = _log_sigmoid(lf_r)
    lf_c = _log_sigmoid(lf_c)
    row = lax.broadcasted_iota(jnp.int32, (L, L), 0)
    col = lax.broadcasted_iota(jnp.int32, (L, L), 1)
    vis = (col >= row) if reverse else (col <= row)
    vis_t = (row >= col) if reverse else (row <= col)
    b_c = jnp.sum(jnp.where(vis, lf_r, 0.0), axis=1, keepdims=True)
    b_r = jnp.sum(jnp.where(vis_t, lf_c, 0.0), axis=0, keepdims=True)
    g = jnp.sum(lf_r, axis=1, keepdims=True)
    m = m_ref[...]
    ones_col = (lax.broadcasted_iota(jnp.int32, (L, LANES), 1) == 0).astype(BF16)
    v_aug = jnp.concatenate([v, ones_col], axis=1)

    dlog = jnp.where(vis, b_c - b_r + li_r, -jnp.inf)
    m_inter = b_c + m
    m_t = jnp.maximum(m_inter, jnp.max(dlog, axis=1, keepdims=True))
    s = jnp.dot(q, kt, preferred_element_type=F32) * (jnp.exp(dlog - m_t) * scale)
    inter = jnp.exp(m_inter - m_t)
    out = jnp.dot(s.astype(BF16), v_aug, preferred_element_type=F32)
    out = out + jnp.dot(q, S[...].astype(BF16), preferred_element_type=F32) * inter
    den = out[:, dq:dq + 1]
    o_ref[...] = (out[:, :dq] / jnp.maximum(jnp.abs(den), jnp.exp(-m_t))).astype(o_ref.dtype)

    a = g - b_r + li_r
    m_new = jnp.maximum(g + m, jnp.max(a, axis=1, keepdims=True))
    w = jnp.exp(a - m_new) * scale
    decay = jnp.exp(g + m - m_new)
    S[...] = decay * S[...] + jnp.dot(kt * w.astype(BF16), v_aug, preferred_element_type=F32)
    m_ref[...] = m_new


def _mlstm_body(qf_ref, kf_ref, vf_ref, qb_ref, kb_ref, vb_ref, gr_f_ref, gr_b_ref, gc_f_ref, gc_b_ref,
                of_ref, ob_ref, S, m_ref, *, L, dq, heads):
    h0 = pl.program_id(1) * ML_HEADS_PER_STEP

    @pl.when(pl.program_id(2) == 0)
    def _():
        S[...] = jnp.zeros_like(S)
        m_ref[...] = jnp.zeros_like(m_ref)

    lane = lax.broadcasted_iota(jnp.int32, (L, LANES), 1)
    streams = ((qf_ref, kf_ref, vf_ref, gr_f_ref, gc_f_ref, of_ref),
               (qb_ref, kb_ref, vb_ref, gr_b_ref, gc_b_ref, ob_ref))
    for d, (q_ref, kt_ref, v_ref, gr_ref, gc_ref, o_ref) in enumerate(streams):
        for hh in range(ML_HEADS_PER_STEP):
            sl = slice(hh * dq, (hh + 1) * dq)
            i_idx = d * 2 * heads + h0 + hh
            f_idx = i_idx + heads
            lf_c = jnp.sum(jnp.where(lane == f_idx, gc_ref[...], 0.0), axis=1, keepdims=True)
            _mlstm_direction(q_ref[:, sl], kt_ref[sl, :], v_ref[:, sl],
                             gr_ref[pl.ds(i_idx, 1), :], gr_ref[pl.ds(f_idx, 1), :], lf_c,
                             S.at[d, hh], m_ref.at[d, hh], o_ref.at[:, sl], reverse=bool(d), L=L, dq=dq)


def _mlstm(q, kt, v, pre, *, L):
    B, T, inner = q.shape
    H = ML_HEADS
    dh = inner // H
    n = T // L
    n_gate = 4 * H
    gate_rows = jnp.transpose(pre[:, :, :n_gate], (0, 2, 1))

    fwd = lambda c: c
    bwd = lambda c: n - 1 - c
    hps = ML_HEADS_PER_STEP
    qv_spec = lambda at: pl.BlockSpec((None, L, hps * dh), lambda b, h, c: (b, at(c), h))
    kt_spec = lambda at: pl.BlockSpec((None, hps * dh, L), lambda b, h, c: (b, h, at(c)))
    row_spec = lambda at: pl.BlockSpec((None, n_gate, L), lambda b, h, c: (b, 0, at(c)))
    col_spec = lambda at: pl.BlockSpec((None, L, LANES), lambda b, h, c: (b, at(c), 0))
    out = jax.ShapeDtypeStruct((B, T, inner), BF16)
    body = functools.partial(_mlstm_body, L=L, dq=dh, heads=H)
    return pl.pallas_call(
        body,
        out_shape=(out, out),
        grid=(B, H // hps, n),
        in_specs=[qv_spec(fwd), kt_spec(fwd), qv_spec(fwd), qv_spec(bwd), kt_spec(bwd), qv_spec(bwd),
                  row_spec(fwd), row_spec(bwd), col_spec(fwd), col_spec(bwd)],
        out_specs=(qv_spec(fwd), qv_spec(bwd)),
        scratch_shapes=[pltpu.VMEM((2, hps, dh, dh + LANES), F32), pltpu.VMEM((2, hps, 1, 1), F32)],
        compiler_params=_params(("parallel", "parallel", "arbitrary")),
        name="mlstm_chunkwise",
    )(q, kt, v, q, kt, v, gate_rows, gate_rows, pre, pre)


def _mlstm_post_body(hf_ref, hb_ref, xc_ref, z_ref, ng_ref, skip_ref, o_ref, *, dh, inner):
    for h in range(inner // dh):
        sl = slice(h * dh, (h + 1) * dh)
        ht = hf_ref[:, sl].astype(F32) + hb_ref[:, sl].astype(F32)
        mu = jnp.mean(ht, axis=-1, keepdims=True)
        var = jnp.mean(jnp.square(ht - mu), axis=-1, keepdims=True)
        hn = (ht - mu) * lax.rsqrt(var + LN_EPS) * ng_ref[:, sl]
        out = (hn + skip_ref[:, sl] * xc_ref[:, sl].astype(F32)) * jax.nn.sigmoid(z_ref[:, sl].astype(F32))
        o_ref[:, sl] = out.astype(o_ref.dtype)


def _mlstm_post(hf, hb, xc, xz, norm_g, skip, *, tt):
    B, T, inner = hf.shape
    dh = inner // ML_HEADS
    body = functools.partial(_mlstm_post_body, dh=dh, inner=inner)
    vec = pl.BlockSpec((1, inner), lambda b, t: (0, 0))
    return pl.pallas_call(
        body,
        out_shape=jax.ShapeDtypeStruct((B, T, inner), BF16),
        grid=(B, T // tt),
        in_specs=[pl.BlockSpec((None, tt, inner), lambda b, t: (b, t, 0)),
                  pl.BlockSpec((None, tt, inner), lambda b, t: (b, t, 0)),
                  pl.BlockSpec((None, tt, inner), lambda b, t: (b, t, 0)),
                  pl.BlockSpec((None, tt, inner), lambda b, t: (b, t, 1)),
                  vec, vec],
        out_specs=pl.BlockSpec((None, tt, inner), lambda b, t: (b, t, 0)),
        compiler_params=_params(("parallel", "parallel")),
        name="mlstm_post",
    )(hf, hb, xc, xz, norm_g.reshape(1, inner), skip.reshape(1, inner))


def _pack_bf16_pairs(x):
    half = x.shape[1] // 2
    bits = pltpu.bitcast(x.astype(BF16).astype(F32), jnp.uint32)
    return (bits[:, :half] >> 16) | bits[:, half:]


def _unpack_bf16_pairs(w):
    lo = pltpu.bitcast(w << 16, F32).astype(BF16)
    hi = pltpu.bitcast(w & jnp.uint32(0xFFFF0000), F32).astype(BF16)
    return lo, hi


def _router_body(x_ref, g_ref, sc_ref, sh_ref, r_ref, h_ref, idx_ref, wt_ref, *, n_experts):
    h = _norm_mod(x_ref[...], g_ref[...], sc_ref[...], sh_ref[...])
    packed = _pack_bf16_pairs(h)
    n_lane_blocks = packed.shape[1] // LANES
    for c in range(n_lane_blocks):
        h_ref[pl.ds(c, packed.shape[0], stride=n_lane_blocks), :] = packed[:, c * LANES:(c + 1) * LANES]
    r = r_ref[...]
    h_hi, r_hi = h.astype(BF16), r.astype(BF16)
    h_lo = (h - h_hi.astype(F32)).astype(BF16)
    r_lo = (r - r_hi.astype(F32)).astype(BF16)
    logits = (jnp.dot(h_hi, r_hi, preferred_element_type=F32) + jnp.dot(h_lo, r_hi, preferred_element_type=F32)
              + jnp.dot(h_hi, r_lo, preferred_element_type=F32))
    lane = lax.broadcasted_iota(jnp.int32, logits.shape, 1)
    lg = jnp.where(lane < n_experts, logits, -jnp.inf)
    m1 = jnp.max(lg, axis=1, keepdims=True)
    i1 = jnp.min(jnp.where(lg == m1, lane, LANES), axis=1, keepdims=True)
    lg2 = jnp.where(lane == i1, -jnp.inf, lg)
    m2 = jnp.max(lg2, axis=1, keepdims=True)
    i2 = jnp.min(jnp.where(lg2 == m2, lane, LANES), axis=1, keepdims=True)
    e2 = jnp.exp(m2 - m1)
    w1 = 1.0 / (1.0 + e2)
    w2 = e2 * w1
    choices = jnp.where(lane == 0, i1, jnp.where(lane == 1, i2, 0)).astype(F32)
    idx_ref[...] = jnp.transpose(choices)[:idx_ref.shape[0], :].astype(jnp.int32)
    wt_ref[...] = jnp.where(lane == 0, w1, jnp.where(lane == 1, w2, 0.0))


def _router(x, g, layer, mod, router, *, tt):
    B, T, D = x.shape
    E = router.shape[1]
    r_pad = jnp.pad(router, ((0, 0), (0, LANES - E)))
    body = functools.partial(_router_body, n_experts=E)
    small = pl.BlockSpec((None, tt, LANES), lambda b, t: (b, t, 0))
    return pl.pallas_call(
        body,
        out_shape=(jax.ShapeDtypeStruct((B, T * (D // 2 // LANES), LANES), jnp.uint32),
                   jax.ShapeDtypeStruct((B, T // tt, 8, tt), jnp.int32),
                   jax.ShapeDtypeStruct((B, T, LANES), F32)),
        grid=(B, T // tt),
        in_specs=[pl.BlockSpec((None, tt, D), lambda b, t: (b, t, 0)),
                  pl.BlockSpec((None, 1, D), lambda b, t: (layer, 0, 0)),
                  _mod_spec(4, D), _mod_spec(3, D),
                  pl.BlockSpec((D, LANES), lambda b, t: (0, 0))],
        out_specs=(pl.BlockSpec((None, tt * (D // 2 // LANES), LANES), lambda b, t: (b, t, 0)),
                   pl.BlockSpec((None, None, 8, tt), lambda b, t: (b, t, 0, 0)), small),
        compiler_params=_params(("parallel", "parallel")),
        name="moe_router",
    )(x, g, mod, mod, r_pad)


def _route_tables(top_i, n_experts, n_tiles, n_slots):
    n_pairs = top_i.size
    e_flat = top_i.reshape(n_pairs)
    onehot = (e_flat[:, None] == jnp.arange(n_experts, dtype=jnp.int32)[None, :]).astype(jnp.int32)
    csum = jnp.cumsum(onehot, axis=0)
    counts = csum[-1]
    rank = jnp.sum(csum * onehot, axis=1) - 1
    nsub = (counts + FFN_SUB - 1) // FFN_SUB
    goff = jnp.cumsum(nsub) - nsub
    pos = jnp.sum(onehot * goff[None, :], axis=1) * FFN_SUB + rank
    src = jnp.zeros((n_slots,), jnp.int32).at[pos].set(jnp.arange(n_pairs, dtype=jnp.int32) // TOP_K)

    spr = FFN_ROWS // FFN_SUB
    ntile = (nsub + spr - 1) // spr
    per_tile = (nsub + jnp.maximum(ntile, 1) - 1) // jnp.maximum(ntile, 1)
    tcum = jnp.cumsum(ntile)
    toff = tcum - ntile
    ids = jnp.arange(n_tiles, dtype=jnp.int32)
    te = jnp.minimum(jnp.sum((ids[:, None] >= tcum[None, :]).astype(jnp.int32), axis=1), n_experts - 1)
    valid = ids < tcum[-1]
    j = ids - toff[te]
    t_start = goff[te] + j * per_tile[te]
    t_nsub = jnp.minimum(per_tile[te], nsub[te] - j * per_tile[te])
    last_e = te[jnp.maximum(tcum[-1] - 1, 0)]
    tile_e = jnp.where(valid, te, last_e).astype(jnp.int32)
    tile_start = jnp.where(valid, t_start, 0).astype(jnp.int32)
    tile_nsub = jnp.where(valid, t_nsub, 0).astype(jnp.int32)
    n_used_sub = jnp.sum(nsub).astype(jnp.int32).reshape(1)
    return pos.astype(jnp.int32), src, tile_e, tile_start, tile_nsub, n_used_sub


def _dense_tiles(n_rows):
    nsub = n_rows // FFN_SUB
    spr = FFN_ROWS // FFN_SUB
    ntile = -(-nsub // spr)
    per_tile = -(-nsub // ntile)
    starts = [j * per_tile for j in range(ntile)]
    counts = [min(per_tile, nsub - st) for st in starts]
    return (jnp.zeros((ntile,), jnp.int32), jnp.array(starts, jnp.int32), jnp.array(counts, jnp.int32))


def _dispatch_body(src_ref, h_ref, o_ref, buf, sem, *, chunk, slab):
    i = pl.program_id(0)

    def gather(step, slot):
        base = step * chunk

        def issue(r, carry):
            src_row = pl.multiple_of(src_ref[base + r] * slab, slab)
            dst_row = pl.multiple_of(r * slab, slab)
            pltpu.make_async_copy(h_ref.at[pl.ds(src_row, slab)], buf.at[slot, pl.ds(dst_row, slab)],
                                  sem.at[slot]).start()
            return carry

        lax.fori_loop(0, chunk, issue, 0, unroll=8)

    @pl.when(i == 0)
    def _():
        gather(0, 0)

    @pl.when(i + 1 < pl.num_programs(0))
    def _():
        gather(i + 1, (i + 1) % 2)

    slot = i % 2
    pltpu.make_async_copy(h_ref.at[pl.ds(0, chunk * slab)], buf.at[slot], sem.at[slot]).wait()
    half = slab * LANES
    for c in range(slab):
        lo, hi = _unpack_bf16_pairs(buf[slot, pl.ds(c, chunk, stride=slab), :])
        o_ref[:, c * LANES:(c + 1) * LANES] = lo
        o_ref[:, half + c * LANES:half + (c + 1) * LANES] = hi


def _dispatch(src, h, n_slots, *, chunk, slab):
    D = 2 * slab * LANES
    body = functools.partial(_dispatch_body, chunk=chunk, slab=slab)
    return pl.pallas_call(
        body,
        out_shape=jax.ShapeDtypeStruct((n_slots, D), BF16),
        grid_spec=pltpu.PrefetchScalarGridSpec(
            num_scalar_prefetch=1,
            grid=(n_slots // chunk,),
            in_specs=[pl.BlockSpec(memory_space=pl.ANY)],
            out_specs=pl.BlockSpec((chunk, D), lambda i, src: (i, 0)),
            scratch_shapes=[pltpu.VMEM((2, chunk * slab, LANES), jnp.uint32), pltpu.SemaphoreType.DMA((2,))]),
        compiler_params=_params(("arbitrary",)),
        name="moe_dispatch",
    )(src, h)


def _expert_body(te_ref, ts_ref, tn_ref, used_ref, xs_ref, wg_ref, wu_ref, wd_ref, y_ref,
                 xb, acc, wgb, wub, wdb, sem_in, sem_out, *, nf):
    s = pl.program_id(0)
    f = pl.program_id(1)
    nsub = tn_ref[s]
    start = ts_ref[s]
    d_model = acc.shape[1]

    def local_rows(j, n=1):
        return pl.ds(pl.multiple_of(j * FFN_SUB, FFN_SUB), n * FFN_SUB)

    def hbm_rows(j):
        return pl.ds(pl.multiple_of((start + j) * FFN_SUB, FFN_SUB), FFN_SUB)

    def in_copy(j):
        return pltpu.make_async_copy(xs_ref.at[hbm_rows(j)], xb.at[local_rows(j)], sem_in)

    def out_copy(j):
        return pltpu.make_async_copy(acc.at[local_rows(j)], y_ref.at[hbm_rows(j)], sem_out)

    def for_each_sub(lo, hi, fn):
        def step(j, carry):
            fn(j)
            return carry

        lax.fori_loop(lo, hi, step, 0)

    def zero_acc(j):
        acc[local_rows(j), :] = jnp.zeros((FFN_SUB, d_model), F32)

    @pl.when((s == 0) & (f == 0))
    def _():
        def fill_copy(j):
            row = pl.multiple_of(j * FFN_SUB, FFN_SUB)
            return pltpu.make_async_copy(acc.at[local_rows(0)], y_ref.at[pl.ds(row, FFN_SUB)], sem_out)

        n_sub_total = y_ref.shape[0] // FFN_SUB
        zero_acc(0)
        for_each_sub(used_ref[0], n_sub_total, lambda j: fill_copy(j).start())
        for_each_sub(used_ref[0], n_sub_total, lambda j: fill_copy(j).wait())

    @pl.when(nsub > 0)
    def _():
        @pl.when(f == 0)
        def _():
            for_each_sub(0, nsub, lambda j: in_copy(j).start())
            for_each_sub(0, nsub, zero_acc)
            for_each_sub(0, nsub, lambda j: in_copy(j).wait())

        def block(j0, n, cast_weights=False):
            if cast_weights:
                wg, wu, wd = (r[...].astype(BF16) for r in (wg_ref, wu_ref, wd_ref))
                wgb[...], wub[...], wdb[...] = wg, wu, wd
            else:
                wg, wu, wd = wgb[...], wub[...], wdb[...]
            rows = local_rows(j0, n)
            x = xb[rows, :]
            hg = jnp.dot(x, wg, preferred_element_type=F32)
            hu = jnp.dot(x, wu, preferred_element_type=F32)
            hid = (hg * jax.nn.sigmoid(hg) * hu).astype(BF16)
            acc[rows, :] += jnp.dot(hid, wd, preferred_element_type=F32)

            @pl.when(f == nf - 1)
            def _():
                for u in range(n):
                    out_copy(j0 + u).start()

        big = FFN_BLOCK_SUBS[0]
        n_big = nsub // big

        @pl.when(n_big > 0)
        def _():
            block(0, big, cast_weights=True)

        for_each_sub(1, n_big, lambda p: block(p * big, big))
        off = n_big * big
        started = n_big > 0
        for size in FFN_BLOCK_SUBS[1:]:
            has = ((nsub - off) // size) > 0

            @pl.when(has & jnp.logical_not(started))
            def _():
                block(0, size, cast_weights=True)

            @pl.when(has & started)
            def _():
                block(off, size)

            off = off + jnp.where(has, size, 0)
            started = started | has

        @pl.when(f == nf - 1)
        def _():
            for_each_sub(0, nsub, lambda j: out_copy(j).wait())


def _grouped_swiglu(tile_e, tile_start, tile_nsub, n_used_sub, xs, w_gate, w_up, w_down):
    P, D = xs.shape
    F = w_gate.shape[2]
    nf = F // FFN_TF
    n_tiles = tile_e.shape[0]

    def f_eff(s, f, tn):
        return jnp.where(tn[s] > 0, f, nf - 1)

    body = functools.partial(_expert_body, nf=nf)
    return pl.pallas_call(
        body,
        out_shape=jax.ShapeDtypeStruct((P, D), F32),
        grid_spec=pltpu.PrefetchScalarGridSpec(
            num_scalar_prefetch=4,
            grid=(n_tiles, nf),
            in_specs=[pl.BlockSpec(memory_space=pl.ANY),
                      pl.BlockSpec((None, D, FFN_TF), lambda s, f, te, ts, tn, used: (te[s], 0, f_eff(s, f, tn))),
                      pl.BlockSpec((None, D, FFN_TF), lambda s, f, te, ts, tn, used: (te[s], 0, f_eff(s, f, tn))),
                      pl.BlockSpec((None, FFN_TF, D), lambda s, f, te, ts, tn, used: (te[s], f_eff(s, f, tn), 0))],
            out_specs=pl.BlockSpec(memory_space=pl.ANY),
            scratch_shapes=[pltpu.VMEM((FFN_ROWS, D), BF16),
                            pltpu.VMEM((FFN_ROWS, D), F32),
                            pltpu.VMEM((D, FFN_TF), BF16),
                            pltpu.VMEM((D, FFN_TF), BF16),
                            pltpu.VMEM((FFN_TF, D), BF16),
                            pltpu.SemaphoreType.DMA,
                            pltpu.SemaphoreType.DMA]),
        compiler_params=_params(("arbitrary", "arbitrary")),
        name="grouped_swiglu",
    )(tile_e, tile_start, tile_nsub, n_used_sub, xs, w_gate, w_up, w_down)


def _combine_body(pos_ref, y_ref, x_ref, w_ref, g_ref, o_ref, ybuf, sem, *, tt):
    i = pl.program_id(0)

    def gather(step, slot):
        base = step * (tt * TOP_K)

        def issue(r, carry):
            for k in range(TOP_K):
                pltpu.make_async_copy(y_ref.at[pl.ds(pos_ref[base + r * TOP_K + k], 1)],
                                      ybuf.at[slot, k, pl.ds(r, 1)], sem.at[slot]).start()
            return carry

        lax.fori_loop(0, tt, issue, 0, unroll=4)

    @pl.when(i == 0)
    def _():
        gather(0, 0)

    @pl.when(i + 1 < pl.num_programs(0))
    def _():
        gather(i + 1, (i + 1) % 2)

    slot = i % 2
    for k in range(TOP_K):
        pltpu.make_async_copy(y_ref.at[pl.ds(0, tt)], ybuf.at[slot, k], sem.at[slot]).wait()
    w = w_ref[...]
    y = w[:, 0:1] * ybuf[slot, 0]
    for k in range(1, TOP_K):
        y = y + w[:, k:k + 1] * ybuf[slot, k]
    o_ref[...] = x_ref[...] + g_ref[...] * y


def _combine(pos, y, x, wts, mod, *, tt):
    N, D = x.shape
    tiles_per_batch = N // mod.shape[0] // tt
    body = functools.partial(_combine_body, tt=tt)
    return pl.pallas_call(
        body,
        out_shape=jax.ShapeDtypeStruct((N, D), F32),
        grid_spec=pltpu.PrefetchScalarGridSpec(
            num_scalar_prefetch=1,
            grid=(N // tt,),
            in_specs=[pl.BlockSpec(memory_space=pl.ANY),
                      pl.BlockSpec((tt, D), lambda i, p: (i, 0)),
                      pl.BlockSpec((tt, LANES), lambda i, p: (i, 0)),
                      pl.BlockSpec((None, None, 1, D), lambda i, p: (i // tiles_per_batch, 5, 0, 0))],
            out_specs=pl.BlockSpec((tt, D), lambda i, p: (i, 0)),
            scratch_shapes=[pltpu.VMEM((2, TOP_K, tt, D), F32), pltpu.SemaphoreType.DMA((2,))]),
        compiler_params=_params(("arbitrary",)),
        name="moe_combine",
    )(pos, y, x, wts, mod)


def _residual_normmod_body(x_ref, y_ref, gate_ref, g_ref, sc_ref, sh_ref, x_out_ref, h_ref):
    x_new = x_ref[...] + gate_ref[...] * y_ref[...]
    x_out_ref[...] = x_new
    h_ref[...] = _norm_mod(x_new, g_ref[...], sc_ref[...], sh_ref[...]).astype(h_ref.dtype)


def _residual_normmod(x, y, mod_prev, k_gate, g, layer, mod_next, k_scale, k_shift, *, tt, name):
    B, T, D = x.shape
    spec = pl.BlockSpec((None, tt, D), lambda b, t: (b, t, 0))
    return pl.pallas_call(
        _residual_normmod_body,
        out_shape=(jax.ShapeDtypeStruct((B, T, D), F32), jax.ShapeDtypeStruct((B, T, D), BF16)),
        grid=(B, T // tt),
        in_specs=[spec, spec, _mod_spec(k_gate, D),
                  pl.BlockSpec((None, 1, D), lambda b, t: (layer, 0, 0)),
                  _mod_spec(k_scale, D), _mod_spec(k_shift, D)],
        out_specs=(spec, spec),
        compiler_params=_params(("parallel", "parallel")),
        name=name,
    )(x, y, mod_prev, g, mod_next, mod_next)


def kernel(x, c, mod_w, mod_b, norm_mix_g, norm_ffn_g, na_w_in, na_q_norm_g, na_k_norm_g, na_rpb, na_w_out,
           ffn_w_gate, ffn_w_up, ffn_w_down, ml_w_in, ml_conv_w, ml_conv_b, ml_wq, ml_wk, ml_wv, ml_w_if,
           ml_b_if, ml_skip, ml_norm_g, ml_w_out, moe_router, moe_w_gate, moe_w_up, moe_w_down):
    B, T, D = x.shape
    N = B * T
    depth = mod_w.shape[0]
    c_pad = jnp.pad(c, ((0, 8 - B), (0, 0)))
    mix_g = norm_mix_g.reshape(depth, 1, D)
    ffn_g = norm_ffn_g.reshape(depth, 1, D)

    def modulation(i):
        mod = _matmul(c_pad, mod_w, i, tn=1024, tm=8, out_dtype=F32, a_silu=True, bias=mod_b,
                      name="adaln_modulation")
        return mod[:B].reshape(B, 6, 1, D)

    mod = modulation(0)
    h = _normmod(x, mix_g, 0, mod, 1, 0, tt=512, out_dtype=BF16, name="norm_mix0")
    na_dh = D // NA_HEADS
    qk_gain = jnp.concatenate([jnp.tile(na_q_norm_g[0] * na_dh ** -0.5, NA_HEADS), jnp.tile(na_k_norm_g[0], NA_HEADS),
                               jnp.ones((D,), F32)]).reshape(1, 3 * D)
    qkv = _matmul(h.reshape(N, D), na_w_in, 0, tn=1024, tm=MM_TM, out_dtype=BF16, head_gain=qk_gain,
                  norm_cols=2 * D, head_dim=na_dh, name="na_qkv")
    att = _neighborhood_attention(qkv.reshape(B, T, 3 * D), na_rpb[0])
    x = _matmul(att.reshape(N, D), na_w_out, 0, tn=1024, tm=MM_TM, out_dtype=F32, res=x.reshape(N, D),
                gate=mod, gate_idx=2, rows_per_batch=T, name="na_out").reshape(B, T, D)

    h = _normmod(x, ffn_g, 0, mod, 4, 3, tt=512, out_dtype=BF16, name="norm_ffn0")
    dense_e, dense_start, dense_nsub = _dense_tiles(N)
    y = _grouped_swiglu(dense_e, dense_start, dense_nsub, jnp.full((1,), N // FFN_SUB, jnp.int32),
                        h.reshape(N, D), ffn_w_gate, ffn_w_up, ffn_w_down)

    mod0, mod = mod, modulation(1)
    x, h = _residual_normmod(x, y.reshape(B, T, D), mod0, 5, mix_g, 1, mod, 1, 0, tt=512,
                             name="ffn0_residual_norm_mix1")
    inner = ml_w_in.shape[2] // 2
    xz = _matmul(h.reshape(N, D), ml_w_in, 0, tn=1024, tm=MM_TM, out_dtype=BF16,
                 name="mlstm_in").reshape(B, T, 2 * inner)
    q, kt, v, xc, pre = _mlstm_pre(xz, ml_conv_w[0], ml_conv_b[0], ml_wq[0], ml_wk[0], ml_wv[0],
                                  ml_w_if[0], ml_b_if[0], tt=256)
    hf, hb = _mlstm(q, kt, v, pre, L=ML_CHUNK)
    u = _mlstm_post(hf, hb, xc, xz, ml_norm_g[0], ml_skip[0], tt=256)
    x = _matmul(u.reshape(N, inner), ml_w_out, 0, tn=512, tm=MM_TM, out_dtype=F32, res=x.reshape(N, D),
                gate=mod, gate_idx=2, rows_per_batch=T, name="mlstm_out").reshape(B, T, D)

    h2, top_idx, top_w = _router(x, ffn_g, 1, mod, moe_router[0], tt=256)
    n_slots = N * TOP_K + N_EXPERTS * FFN_SUB
    n_tiles = -(-N * TOP_K // FFN_ROWS) + N_EXPERTS
    pos, src, tile_e, tile_start, tile_nsub, n_used_sub = _route_tables(
        jnp.moveaxis(top_idx[:, :, :TOP_K, :], 2, 3).reshape(N, TOP_K), N_EXPERTS, n_tiles, n_slots)
    slab = D // 2 // LANES
    xs = _dispatch(src, h2.reshape(N * slab, LANES), n_slots, chunk=DISPATCH_CHUNK, slab=slab)
    ys = _grouped_swiglu(tile_e, tile_start, tile_nsub, n_used_sub, xs, moe_w_gate[0], moe_w_up[0], moe_w_down[0])
    return _combine(pos, ys, x.reshape(N, D), top_w.reshape(N, LANES), mod, tt=COMBINE_TOKENS).reshape(B, T, D)
```

```python
import functools

import jax
import jax.numpy as jnp
from jax import lax
from jax.experimental import pallas as pl
from jax.experimental.pallas import tpu as pltpu

F32 = jnp.float32
BF16 = jnp.bfloat16

GRID_W = 64
NA_HEADS = 16
NA_WIN_ROWS_MAX = 8
NA_WIN_COLS = 16
ML_HEADS = 8
ML_QKV_BLOCK = 4
ML_CONV_K = 5
N_EXPERTS = 8
TOP_K = 2
RMS_EPS = 1e-6
LN_EPS = 1e-5

V7X_VMEM_LIMIT_BYTES = 56 * 1024 * 1024
LANES = 128
BF16_SUBLANES = 16

MM_TM = 1024
ML_CHUNK = 256
ML_HEADS_PER_STEP = 4
ML_LANE_BLOCK = 128
FFN_SUB = 128
FFN_ROWS = 3072
FFN_BLOCK_SUBS = (8, 4, 2, 1)
DISPATCH_CHUNK = 512
COMBINE_TOKENS = 512
FFN_TF = 256
MASK_NEG = -1e30
NA_ROW_GROUP = 16


def _params(sem):
    return pltpu.CompilerParams(dimension_semantics=sem, vmem_limit_bytes=V7X_VMEM_LIMIT_BYTES)


def _mm_body(*refs, a_silu, has_bias, has_res, norm_blocks, head_dim, next_norm):
    a_ref, w_ref = refs[0], refs[1]
    k = 2
    if has_bias:
        b_ref = refs[k]
        k += 1
    if has_res:
        r_ref, g_ref = refs[k], refs[k + 1]
        k += 2
    if norm_blocks:
        hg_ref = refs[k]
        k += 1
    if next_norm:
        ng_ref, nsc_ref, nsh_ref = refs[k:k + 3]
        k += 3
    o_ref = refs[k]
    k += 1
    if next_norm:
        h_ref = refs[k]
        k += 1
    wb_ref = refs[k]
    j = pl.program_id(0)
    i = pl.program_id(1)

    def finish(w):
        a = a_ref[...]
        if a_silu:
            a = a * jax.nn.sigmoid(a)
        y = jnp.dot(a.astype(BF16), w, preferred_element_type=F32)
        if has_bias:
            y = y + b_ref[...]
        if has_res:
            y = r_ref[...] + g_ref[...] * y
        if next_norm:
            h_ref[...] = _norm_mod(y, ng_ref[...], nsc_ref[...], nsh_ref[...]).astype(h_ref.dtype)
        if not norm_blocks:
            o_ref[...] = y.astype(o_ref.dtype)
            return

        normed = j < norm_blocks
        for h in range(y.shape[1] // head_dim):
            sl = slice(h * head_dim, (h + 1) * head_dim)
            yh = y[:, sl]
            inv = lax.rsqrt(jnp.mean(yh * yh, axis=-1, keepdims=True) + RMS_EPS)
            o_ref[:, sl] = (yh * (jnp.where(normed, inv, 1.0) * hg_ref[:, sl])).astype(o_ref.dtype)

    @pl.when(i == 0)
    def _():
        w = w_ref[...].astype(BF16)
        wb_ref[...] = w
        finish(w)

    @pl.when(i > 0)
    def _():
        finish(wb_ref[...])


def _matmul(a, w, layer, *, tn, tm, out_dtype, a_silu=False, bias=None, res=None, gate=None,
            gate_idx=0, rows_per_batch=None, head_gain=None, norm_cols=0, head_dim=None, next_norm=None,
            name="matmul"):
    M, K = a.shape
    N = w.shape[2]
    grid = (N // tn, M // tm)
    w_mode = pl.Buffered(1) if N == tn else None
    in_specs = [pl.BlockSpec((tm, K), lambda j, i: (i, 0)),
                pl.BlockSpec((None, K, tn), lambda j, i: (layer, 0, j), pipeline_mode=w_mode)]
    args = [a, w]
    if bias is not None:
        in_specs.append(pl.BlockSpec((None, 1, tn), lambda j, i: (layer, 0, j)))
        args.append(bias.reshape(bias.shape[0], 1, N))
    if res is not None:
        tiles_per_batch = rows_per_batch // tm
        in_specs.append(pl.BlockSpec((tm, tn), lambda j, i: (i, j)))
        in_specs.append(pl.BlockSpec((None, None, 1, tn),
                                     lambda j, i: (i // tiles_per_batch, gate_idx, 0, j)))
        args += [res, gate]
    if head_gain is not None:
        in_specs.append(pl.BlockSpec((1, tn), lambda j, i: (0, j)))
        args.append(head_gain)
    out_shape = jax.ShapeDtypeStruct((M, N), out_dtype)
    out_specs = pl.BlockSpec((tm, tn), lambda j, i: (i, j))
    if next_norm is not None:
        assert tn == N and res is not None
        n_gain, n_layer, k_scale, k_shift = next_norm
        in_specs.append(pl.BlockSpec((None, 1, N), lambda j, i: (n_layer, 0, 0)))
        in_specs.append(pl.BlockSpec((None, None, 1, N), lambda j, i: (i // tiles_per_batch, k_scale, 0, 0)))
        in_specs.append(pl.BlockSpec((None, None, 1, N), lambda j, i: (i // tiles_per_batch, k_shift, 0, 0)))
        args += [n_gain, gate, gate]
        out_shape = (out_shape, jax.ShapeDtypeStruct((M, N), BF16))
        out_specs = (out_specs, pl.BlockSpec((tm, tn), lambda j, i: (i, j)))
    body = functools.partial(_mm_body, a_silu=a_silu, has_bias=bias is not None,
                             has_res=res is not None, norm_blocks=norm_cols // tn, head_dim=head_dim,
                             next_norm=next_norm is not None)
    return pl.pallas_call(
        body,
        out_shape=out_shape,
        grid=grid,
        in_specs=in_specs,
        out_specs=out_specs,
        scratch_shapes=[pltpu.VMEM((K, tn), BF16)],
        compiler_params=_params(("parallel", "arbitrary")),
        name=name,
    )(*args)


def _norm_mod(x, g, sc, sh):
    y = x * lax.rsqrt(jnp.mean(x * x, axis=-1, keepdims=True) + RMS_EPS) * g
    return y * (1.0 + sc) + sh


def _normmod_body(x_ref, g_ref, sc_ref, sh_ref, o_ref):
    o_ref[...] = _norm_mod(x_ref[...], g_ref[...], sc_ref[...], sh_ref[...]).astype(o_ref.dtype)


def _mod_spec(k, D):
    return pl.BlockSpec((None, None, 1, D), lambda b, t: (b, k, 0, 0))


def _normmod(x, g, layer, mod, k_scale, k_shift, *, tt, out_dtype, name):
    B, T, D = x.shape
    return pl.pallas_call(
        _normmod_body,
        out_shape=jax.ShapeDtypeStruct((B, T, D), out_dtype),
        grid=(B, T // tt),
        in_specs=[pl.BlockSpec((None, tt, D), lambda b, t: (b, t, 0)),
                  pl.BlockSpec((None, 1, D), lambda b, t: (layer, 0, 0)),
                  _mod_spec(k_scale, D), _mod_spec(k_shift, D)],
        out_specs=pl.BlockSpec((None, tt, D), lambda b, t: (b, t, 0)),
        compiler_params=_params(("parallel", "parallel")),
        name=name,
    )(x, g, mod, mod)


def _na_body(q_ref, k_ref, v_ref, bias_ref, o_ref, *, rows, width, kh):
    def row_group(gi, carry):
        offs, scores, probs = [], [], []
        for u in range(NA_ROW_GROUP):
            r = gi * NA_ROW_GROUP + u
            r0 = jnp.clip(r - kh // 2, 0, rows - kh)
            q_off = pl.multiple_of(r * width, width)
            b_off = pl.multiple_of(r0 * width, width)
            q_r = q_ref[pl.ds(q_off, width), :]
            k_band = k_ref[pl.ds(b_off, kh * width), :]
            s = lax.dot_general(q_r, k_band, (((1,), (1,)), ((), ())), preferred_element_type=F32)
            rel = r0 - r + NA_WIN_ROWS_MAX - 1
            bias = jnp.concatenate([bias_ref[rel + i, 0] + bias_ref[rel + i + 1, 1] for i in range(0, kh, 2)],
                                   axis=1)
            scores.append(s + bias)
            offs.append((q_off, b_off))
        for s in scores:
            e = jnp.exp(s - jnp.max(s, axis=-1, keepdims=True))
            probs.append((e.astype(BF16), jnp.sum(e, axis=-1, keepdims=True)))
        for (q_off, b_off), (e, l) in zip(offs, probs):
            v_band = v_ref[pl.ds(b_off, kh * width), :]
            o = jnp.dot(e, v_band, preferred_element_type=F32) / l
            o_ref[pl.ds(q_off, width), :] = o.astype(o_ref.dtype)
        return carry

    lax.fori_loop(0, rows // NA_ROW_GROUP, row_group, 0)


def _na_bias_table(rpb):
    col = jnp.arange(GRID_W)
    col_start = jnp.clip(col - NA_WIN_COLS // 2, 0, GRID_W - NA_WIN_COLS)
    col_in = (col[None, :] >= col_start[:, None]) & (col[None, :] < col_start[:, None] + NA_WIN_COLS)
    H, n_di, n_dj = rpb.shape
    far = GRID_W - NA_WIN_COLS
    rpb = rpb.astype(F32)
    circle = jnp.concatenate([rpb[..., NA_WIN_COLS - 1:], jnp.repeat(rpb[..., -1:], far, axis=-1),
                              jnp.repeat(rpb[..., :1], far, axis=-1), rpb[..., :NA_WIN_COLS - 1]], axis=-1)
    pitch = 2 * GRID_W - 2
    toep = jnp.tile(circle, (1, 1, GRID_W))[..., :GRID_W * pitch].reshape(H, n_di, GRID_W, pitch)[..., :GRID_W]
    rpb_cols = jnp.where(col_in[None, None], toep, MASK_NEG)
    zeros = jnp.zeros_like(rpb_cols)
    return jnp.stack([jnp.concatenate([rpb_cols, zeros], axis=-1),
                      jnp.concatenate([zeros, rpb_cols], axis=-1)], axis=2)


def _neighborhood_attention(qkv, rpb):
    B, T, D3 = qkv.shape
    D = D3 // 3
    H = NA_HEADS
    dh = D // H
    rows = T // GRID_W
    kh = min(NA_WIN_ROWS_MAX, rows)
    bias = _na_bias_table(rpb)
    body = functools.partial(_na_body, rows=rows, width=GRID_W, kh=kh)
    return pl.pallas_call(
        body,
        out_shape=jax.ShapeDtypeStruct((B, T, D), BF16),
        grid=(B, H),
        in_specs=[pl.BlockSpec((None, T, dh), lambda b, h: (b, 0, h)),
                  pl.BlockSpec((None, T, dh), lambda b, h: (b, 0, H + h)),
                  pl.BlockSpec((None, T, dh), lambda b, h: (b, 0, 2 * H + h)),
                  pl.BlockSpec((None,) + bias.shape[1:], lambda b, h: (h, 0, 0, 0, 0))],
        out_specs=pl.BlockSpec((None, T, dh), lambda b, h: (b, 0, h)),
        compiler_params=_params(("parallel", "parallel")),
        name="neighborhood_attention",
    )(qkv, qkv, qkv, bias)


def _mlstm_pre_body(xm_ref, xp_ref, xn_ref, cw_ref, cb_ref, wq_ref, wk_ref, wv_ref, wif_ref, bif_ref,
                    q_ref, kt_ref, v_ref, xc_ref, pre_ref, cat_ref, *, tt, inner, halo):
    t = pl.program_id(1)
    first = (t > 0).astype(F32)
    last = (t < pl.num_programs(1) - 1).astype(F32)
    pad = ML_CONV_K // 2
    pre = jnp.zeros((tt, LANES), F32)
    width = ML_LANE_BLOCK
    for c in range(inner // width):
        sl = slice(c * width, (c + 1) * width)
        cur_b = xm_ref[:, sl]
        cat = jnp.concatenate([xp_ref[:, sl].astype(F32) * first, cur_b.astype(F32),
                               xn_ref[:, sl].astype(F32) * last], axis=0)
        cat_ref[...] = cat
        xc = jnp.zeros((tt, width), F32) + cb_ref[:, sl]
        for j in range(ML_CONV_K):
            xc = xc + cat_ref[halo - pad + j:halo - pad + j + tt, :] * cw_ref[j:j + 1, sl]
        xc = xc * jax.nn.sigmoid(xc)
        xc_b = xc.astype(BF16)
        q = jnp.dot(xc_b, wq_ref[c], preferred_element_type=F32).astype(BF16)
        k = jnp.dot(xc_b, wk_ref[c], preferred_element_type=F32).astype(BF16)
        v = jnp.dot(cur_b, wv_ref[c], preferred_element_type=F32).astype(BF16)
        pre = pre + jnp.dot(q, wif_ref[0, sl, :], preferred_element_type=F32)
        pre = pre + jnp.dot(k, wif_ref[1, sl, :], preferred_element_type=F32)
        pre = pre + jnp.dot(v, wif_ref[2, sl, :], preferred_element_type=F32)
        q_ref[:, sl] = q
        kt_ref[sl, :] = k.astype(F32).T.astype(BF16)
        v_ref[:, sl] = v
        xc_ref[:, sl] = xc_b
    pre_ref[...] = pre + bif_ref[...]


def _block_diag_dense(w):
    nb = w.shape[0]
    per = ML_LANE_BLOCK // ML_QKV_BLOCK
    wr = w.reshape(nb // per, per, ML_QKV_BLOCK, ML_QKV_BLOCK)
    eye = jnp.eye(per, dtype=w.dtype)
    dense = jnp.einsum('gnio,nm->gnimo', wr, eye)
    return dense.reshape(nb // per, ML_LANE_BLOCK, ML_LANE_BLOCK).astype(BF16)


def _mlstm_pre(xz, conv_w, conv_b, wq, wk, wv, w_if, b_if, *, tt):
    B, T, inner2 = xz.shape
    inner = inner2 // 2
    halo = BF16_SUBLANES
    n_gate = w_if.shape[1] * w_if.shape[2] * w_if.shape[3]
    wif = jnp.pad(w_if.reshape(3, inner, n_gate), ((0, 0), (0, 0), (0, LANES - n_gate))).astype(BF16)
    bif = jnp.pad(b_if.reshape(1, n_gate), ((0, 0), (0, LANES - n_gate)))
    nblk = inner // ML_LANE_BLOCK
    hb = tt // halo
    nh = T // halo
    body = functools.partial(_mlstm_pre_body, tt=tt, inner=inner, halo=halo)
    act = jax.ShapeDtypeStruct((B, T, inner), BF16)
    act_spec = pl.BlockSpec((None, tt, inner), lambda b, t: (b, t, 0))
    const3 = lambda b, t: (0, 0, 0)
    return pl.pallas_call(
        body,
        out_shape=(act, jax.ShapeDtypeStruct((B, inner, T), BF16), act, act,
                   jax.ShapeDtypeStruct((B, T, LANES), F32)),
        grid=(B, T // tt),
        in_specs=[act_spec,
                  pl.BlockSpec((None, halo, inner), lambda b, t: (b, jnp.maximum(t * hb - 1, 0), 0)),
                  pl.BlockSpec((None, halo, inner), lambda b, t: (b, jnp.minimum((t + 1) * hb, nh - 1), 0)),
                  pl.BlockSpec((ML_CONV_K, inner), lambda b, t: (0, 0)),
                  pl.BlockSpec((1, inner), lambda b, t: (0, 0)),
                  pl.BlockSpec((nblk, ML_LANE_BLOCK, ML_LANE_BLOCK), const3),
                  pl.BlockSpec((nblk, ML_LANE_BLOCK, ML_LANE_BLOCK), const3),
                  pl.BlockSpec((nblk, ML_LANE_BLOCK, ML_LANE_BLOCK), const3),
                  pl.BlockSpec((3, inner, LANES), const3),
                  pl.BlockSpec((1, LANES), lambda b, t: (0, 0))],
        out_specs=(act_spec, pl.BlockSpec((None, inner, tt), lambda b, t: (b, 0, t)), act_spec, act_spec,
                   pl.BlockSpec((None, tt, LANES), lambda b, t: (b, t, 0))),
        scratch_shapes=[pltpu.VMEM((tt + 2 * halo, ML_LANE_BLOCK), F32)],
        compiler_params=_params(("parallel", "parallel")),
        name="mlstm_pre",
    )(xz, xz, xz, conv_w, conv_b.reshape(1, inner), _block_diag_dense(wq), _block_diag_dense(wk),
      _block_diag_dense(wv), wif, bif)


def _log_sigmoid(x):
    return jnp.minimum(x, 0.0) - jnp.log(1.0 + jnp.exp(-jnp.abs(x)))


def _mlstm_direction(q, kt, v, li_r, lf_r, lf_c, S, m_ref, o_ref, *, reverse, L, dq):
    scale = dq ** -0.5
    lf_r = _log_sigmoid(lf_r)
    lf_c = _log_sigmoid(lf_c)
    row = lax.broadcasted_iota(jnp.int32, (L, L), 0)
    col = lax.broadcasted_iota(jnp.int32, (L, L), 1)
    vis = (col >= row) if reverse else (col <= row)
    vis_t = (row >= col) if reverse else (row <= col)
    b_c = jnp.sum(jnp.where(vis, lf_r, 0.0), axis=1, keepdims=True)
    b_r = jnp.sum(jnp.where(vis_t, lf_c, 0.0), axis=0, keepdims=True)
    g = jnp.sum(lf_r, axis=1, keepdims=True)
    m = m_ref[...]
    ones_col = (lax.broadcasted_iota(jnp.int32, (L, LANES), 1) == 0).astype(BF16)
    v_aug = jnp.concatenate([v, ones_col], axis=1)

    dlog = jnp.where(vis, b_c - b_r + li_r, -jnp.inf)
    m_inter = b_c + m
    m_t = jnp.maximum(m_inter, jnp.max(dlog, axis=1, keepdims=True))
    s = jnp.dot(q, kt, preferred_element_type=F32) * (jnp.exp(dlog - m_t) * scale)
    inter = jnp.exp(m_inter - m_t)
    out = jnp.dot(s.astype(BF16), v_aug, preferred_element_type=F32)
    out = out + jnp.dot(q, S[...].astype(BF16), preferred_element_type=F32) * inter
    den = out[:, dq:dq + 1]
    o_ref[...] = (out[:, :dq] / jnp.maximum(jnp.abs(den), jnp.exp(-m_t))).astype(o_ref.dtype)

    a = g - b_r + li_r
    m_new = jnp.maximum(g + m, jnp.max(a, axis=1, keepdims=True))
    w = jnp.exp(a - m_new) * scale
    decay = jnp.exp(g + m - m_new)
    S[...] = decay * S[...] + jnp.dot(kt * w.astype(BF16), v_aug, preferred_element_type=F32)
    m_ref[...] = m_new


def _mlstm_body(qf_ref, kf_ref, vf_ref, qb_ref, kb_ref, vb_ref, gr_f_ref, gr_b_ref, gc_f_ref, gc_b_ref,
                of_ref, ob_ref, S, m_ref, *, L, dq, heads):
    h0 = pl.program_id(1) * ML_HEADS_PER_STEP

    @pl.when(pl.program_id(2) == 0)
    def _():
        S[...] = jnp.zeros_like(S)
        m_ref[...] = jnp.zeros_like(m_ref)

    lane = lax.broadcasted_iota(jnp.int32, (L, LANES), 1)
    streams = ((qf_ref, kf_ref, vf_ref, gr_f_ref, gc_f_ref, of_ref),
               (qb_ref, kb_ref, vb_ref, gr_b_ref, gc_b_ref, ob_ref))
    for d, (q_ref, kt_ref, v_ref, gr_ref, gc_ref, o_ref) in enumerate(streams):
        for hh in range(ML_HEADS_PER_STEP):
            sl = slice(hh * dq, (hh + 1) * dq)
            i_idx = d * 2 * heads + h0 + hh
            f_idx = i_idx + heads
            lf_c = jnp.sum(jnp.where(lane == f_idx, gc_ref[...], 0.0), axis=1, keepdims=True)
            _mlstm_direction(q_ref[:, sl], kt_ref[sl, :], v_ref[:, sl],
                             gr_ref[pl.ds(i_idx, 1), :], gr_ref[pl.ds(f_idx, 1), :], lf_c,
                             S.at[d, hh], m_ref.at[d, hh], o_ref.at[:, sl], reverse=bool(d), L=L, dq=dq)


def _mlstm(q, kt, v, pre, *, L):
    B, T, inner = q.shape
    H = ML_HEADS
    dh = inner // H
    n = T // L
    n_gate = 4 * H
    gate_rows = jnp.transpose(pre[:, :, :n_gate], (0, 2, 1))

    fwd = lambda c: c
    bwd = lambda c: n - 1 - c
    hps = ML_HEADS_PER_STEP
    qv_spec = lambda at: pl.BlockSpec((None, L, hps * dh), lambda b, h, c: (b, at(c), h))
    kt_spec = lambda at: pl.BlockSpec((None, hps * dh, L), lambda b, h, c: (b, h, at(c)))
    row_spec = lambda at: pl.BlockSpec((None, n_gate, L), lambda b, h, c: (b, 0, at(c)))
    col_spec = lambda at: pl.BlockSpec((None, L, LANES), lambda b, h, c: (b, at(c), 0))
    out = jax.ShapeDtypeStruct((B, T, inner), BF16)
    body = functools.partial(_mlstm_body, L=L, dq=dh, heads=H)
    return pl.pallas_call(
        body,
        out_shape=(out, out),
        grid=(B, H // hps, n),
        in_specs=[qv_spec(fwd), kt_spec(fwd), qv_spec(fwd), qv_spec(bwd), kt_spec(bwd), qv_spec(bwd),
                  row_spec(fwd), row_spec(bwd), col_spec(fwd), col_spec(bwd)],
        out_specs=(qv_spec(fwd), qv_spec(bwd)),
        scratch_shapes=[pltpu.VMEM((2, hps, dh, dh + LANES), F32), pltpu.VMEM((2, hps, 1, 1), F32)],
        compiler_params=_params(("parallel", "parallel", "arbitrary")),
        name="mlstm_chunkwise",
    )(q, kt, v, q, kt, v, gate_rows, gate_rows, pre, pre)


def _mlstm_post_body(hf_ref, hb_ref, xc_ref, z_ref, ng_ref, skip_ref, o_ref, *, dh, inner):
    for h in range(inner // dh):
        sl = slice(h * dh, (h + 1) * dh)
        ht = hf_ref[:, sl].astype(F32) + hb_ref[:, sl].astype(F32)
        mu = jnp.mean(ht, axis=-1, keepdims=True)
        var = jnp.mean(jnp.square(ht - mu), axis=-1, keepdims=True)
        hn = (ht - mu) * lax.rsqrt(var + LN_EPS) * ng_ref[:, sl]
        out = (hn + skip_ref[:, sl] * xc_ref[:, sl].astype(F32)) * jax.nn.sigmoid(z_ref[:, sl].astype(F32))
        o_ref[:, sl] = out.astype(o_ref.dtype)


def _mlstm_post(hf, hb, xc, xz, norm_g, skip, *, tt):
    B, T, inner = hf.shape
    dh = inner // ML_HEADS
    body = functools.partial(_mlstm_post_body, dh=dh, inner=inner)
    vec = pl.BlockSpec((1, inner), lambda b, t: (0, 0))
    return pl.pallas_call(
        body,
        out_shape=jax.ShapeDtypeStruct((B, T, inner), BF16),
        grid=(B, T // tt),
        in_specs=[pl.BlockSpec((None, tt, inner), lambda b, t: (b, t, 0)),
                  pl.BlockSpec((None, tt, inner), lambda b, t: (b, t, 0)),
                  pl.BlockSpec((None, tt, inner), lambda b, t: (b, t, 0)),
                  pl.BlockSpec((None, tt, inner), lambda b, t: (b, t, 1)),
                  vec, vec],
        out_specs=pl.BlockSpec((None, tt, inner), lambda b, t: (b, t, 0)),
        compiler_params=_params(("parallel", "parallel")),
        name="mlstm_post",
    )(hf, hb, xc, xz, norm_g.reshape(1, inner), skip.reshape(1, inner))


def _pack_bf16_pairs(x):
    half = x.shape[1] // 2
    bits = pltpu.bitcast(x.astype(BF16).astype(F32), jnp.uint32)
    return (bits[:, :half] >> 16) | bits[:, half:]


def _unpack_bf16_pairs(w):
    lo = pltpu.bitcast(w << 16, F32).astype(BF16)
    hi = pltpu.bitcast(w & jnp.uint32(0xFFFF0000), F32).astype(BF16)
    return lo, hi


def _router_body(x_ref, g_ref, sc_ref, sh_ref, r_ref, h_ref, idx_ref, wt_ref, *, n_experts):
    h = _norm_mod(x_ref[...], g_ref[...], sc_ref[...], sh_ref[...])
    packed = _pack_bf16_pairs(h)
    n_lane_blocks = packed.shape[1] // LANES
    for c in range(n_lane_blocks):
        h_ref[pl.ds(c, packed.shape[0], stride=n_lane_blocks), :] = packed[:, c * LANES:(c + 1) * LANES]
    r = r_ref[...]
    h_hi, r_hi = h.astype(BF16), r.astype(BF16)
    h_lo = (h - h_hi.astype(F32)).astype(BF16)
    r_lo = (r - r_hi.astype(F32)).astype(BF16)
    logits = (jnp.dot(h_hi, r_hi, preferred_element_type=F32) + jnp.dot(h_lo, r_hi, preferred_element_type=F32)
              + jnp.dot(h_hi, r_lo, preferred_element_type=F32))
    lane = lax.broadcasted_iota(jnp.int32, logits.shape, 1)
    lg = jnp.where(lane < n_experts, logits, -jnp.inf)
    m1 = jnp.max(lg, axis=1, keepdims=True)
    i1 = jnp.min(jnp.where(lg == m1, lane, LANES), axis=1, keepdims=True)
    lg2 = jnp.where(lane == i1, -jnp.inf, lg)
    m2 = jnp.max(lg2, axis=1, keepdims=True)
    i2 = jnp.min(jnp.where(lg2 == m2, lane, LANES), axis=1, keepdims=True)
    e2 = jnp.exp(m2 - m1)
    w1 = 1.0 / (1.0 + e2)
    w2 = e2 * w1
    choices = jnp.where(lane == 0, i1, jnp.where(lane == 1, i2, 0)).astype(F32)
    idx_ref[...] = jnp.transpose(choices)[:idx_ref.shape[0], :].astype(jnp.int32)
    wt_ref[...] = jnp.where(lane == 0, w1, jnp.where(lane == 1, w2, 0.0))


def _router(x, g, layer, mod, router, *, tt):
    B, T, D = x.shape
    E = router.shape[1]
    r_pad = jnp.pad(router, ((0, 0), (0, LANES - E)))
    body = functools.partial(_router_body, n_experts=E)
    small = pl.BlockSpec((None, tt, LANES), lambda b, t: (b, t, 0))
    return pl.pallas_call(
        body,
        out_shape=(jax.ShapeDtypeStruct((B, T * (D // 2 // LANES), LANES), jnp.uint32),
                   jax.ShapeDtypeStruct((B, T // tt, 8, tt), jnp.int32),
                   jax.ShapeDtypeStruct((B, T, LANES), F32)),
        grid=(B, T // tt),
        in_specs=[pl.BlockSpec((None, tt, D), lambda b, t: (b, t, 0)),
                  pl.BlockSpec((None, 1, D), lambda b, t: (layer, 0, 0)),
                  _mod_spec(4, D), _mod_spec(3, D),
                  pl.BlockSpec((D, LANES), lambda b, t: (0, 0))],
        out_specs=(pl.BlockSpec((None, tt * (D // 2 // LANES), LANES), lambda b, t: (b, t, 0)),
                   pl.BlockSpec((None, None, 8, tt), lambda b, t: (b, t, 0, 0)), small),
        compiler_params=_params(("parallel", "parallel")),
        name="moe_router",
    )(x, g, mod, mod, r_pad)


def _route_tables(top_i, n_experts, n_tiles, n_slots):
    n_pairs = top_i.size
    e_flat = top_i.reshape(n_pairs)
    onehot = (e_flat[:, None] == jnp.arange(n_experts, dtype=jnp.int32)[None, :]).astype(jnp.int32)
    csum = jnp.cumsum(onehot, axis=0)
    counts = csum[-1]
    rank = jnp.sum(csum * onehot, axis=1) - 1
    nsub = (counts + FFN_SUB - 1) // FFN_SUB
    goff = jnp.cumsum(nsub) - nsub
    pos = jnp.sum(onehot * goff[None, :], axis=1) * FFN_SUB + rank
    src = jnp.zeros((n_slots,), jnp.int32).at[pos].set(jnp.arange(n_pairs, dtype=jnp.int32) // TOP_K)

    spr = FFN_ROWS // FFN_SUB
    ntile = (nsub + spr - 1) // spr
    per_tile = (nsub + jnp.maximum(ntile, 1) - 1) // jnp.maximum(ntile, 1)
    tcum = jnp.cumsum(ntile)
    toff = tcum - ntile
    ids = jnp.arange(n_tiles, dtype=jnp.int32)
    te = jnp.minimum(jnp.sum((ids[:, None] >= tcum[None, :]).astype(jnp.int32), axis=1), n_experts - 1)
    valid = ids < tcum[-1]
    j = ids - toff[te]
    t_start = goff[te] + j * per_tile[te]
    t_nsub = jnp.minimum(per_tile[te], nsub[te] - j * per_tile[te])
    last_e = te[jnp.maximum(tcum[-1] - 1, 0)]
    tile_e = jnp.where(valid, te, last_e).astype(jnp.int32)
    tile_start = jnp.where(valid, t_start, 0).astype(jnp.int32)
    tile_nsub = jnp.where(valid, t_nsub, 0).astype(jnp.int32)
    n_used_sub = jnp.sum(nsub).astype(jnp.int32).reshape(1)
    return pos.astype(jnp.int32), src, tile_e, tile_start, tile_nsub, n_used_sub


def _dense_tiles(n_rows):
    nsub = n_rows // FFN_SUB
    spr = FFN_ROWS // FFN_SUB
    ntile = -(-nsub // spr)
    per_tile = -(-nsub // ntile)
    starts = [j * per_tile for j in range(ntile)]
    counts = [min(per_tile, nsub - st) for st in starts]
    return (jnp.zeros((ntile,), jnp.int32), jnp.array(starts, jnp.int32), jnp.array(counts, jnp.int32))


def _dispatch_body(src_ref, h_ref, o_ref, buf, sem, *, chunk, slab):
    i = pl.program_id(0)

    def gather(step, slot):
        base = step * chunk

        def issue(r, carry):
            src_row = pl.multiple_of(src_ref[base + r] * slab, slab)
            dst_row = pl.multiple_of(r * slab, slab)
            pltpu.make_async_copy(h_ref.at[pl.ds(src_row, slab)], buf.at[slot, pl.ds(dst_row, slab)],
                                  sem.at[slot]).start()
            return carry

        lax.fori_loop(0, chunk, issue, 0, unroll=8)

    @pl.when(i == 0)
    def _():
        gather(0, 0)

    @pl.when(i + 1 < pl.num_programs(0))
    def _():
        gather(i + 1, (i + 1) % 2)

    slot = i % 2
    pltpu.make_async_copy(h_ref.at[pl.ds(0, chunk * slab)], buf.at[slot], sem.at[slot]).wait()
    half = slab * LANES
    for c in range(slab):
        lo, hi = _unpack_bf16_pairs(buf[slot, pl.ds(c, chunk, stride=slab), :])
        o_ref[:, c * LANES:(c + 1) * LANES] = lo
        o_ref[:, half + c * LANES:half + (c + 1) * LANES] = hi


def _dispatch(src, h, n_slots, *, chunk, slab):
    D = 2 * slab * LANES
    body = functools.partial(_dispatch_body, chunk=chunk, slab=slab)
    return pl.pallas_call(
        body,
        out_shape=jax.ShapeDtypeStruct((n_slots, D), BF16),
        grid_spec=pltpu.PrefetchScalarGridSpec(
            num_scalar_prefetch=1,
            grid=(n_slots // chunk,),
            in_specs=[pl.BlockSpec(memory_space=pl.ANY)],
            out_specs=pl.BlockSpec((chunk, D), lambda i, src: (i, 0)),
            scratch_shapes=[pltpu.VMEM((2, chunk * slab, LANES), jnp.uint32), pltpu.SemaphoreType.DMA((2,))]),
        compiler_params=_params(("arbitrary",)),
        name="moe_dispatch",
    )(src, h)


def _expert_body(te_ref, ts_ref, tn_ref, used_ref, xs_ref, wg_ref, wu_ref, wd_ref, y_ref,
                 xb, acc, wgb, wub, wdb, sem_in, sem_out, *, nf):
    s = pl.program_id(0)
    f = pl.program_id(1)
    nsub = tn_ref[s]
    start = ts_ref[s]
    d_model = acc.shape[1]

    def local_rows(j, n=1):
        return pl.ds(pl.multiple_of(j * FFN_SUB, FFN_SUB), n * FFN_SUB)

    def hbm_rows(j):
        return pl.ds(pl.multiple_of((start + j) * FFN_SUB, FFN_SUB), FFN_SUB)

    def in_copy(j):
        return pltpu.make_async_copy(xs_ref.at[hbm_rows(j)], xb.at[local_rows(j)], sem_in)

    def out_copy(j):
        return pltpu.make_async_copy(acc.at[local_rows(j)], y_ref.at[hbm_rows(j)], sem_out)

    def for_each_sub(lo, hi, fn):
        def step(j, carry):
            fn(j)
            return carry

        lax.fori_loop(lo, hi, step, 0)

    def zero_acc(j):
        acc[local_rows(j), :] = jnp.zeros((FFN_SUB, d_model), F32)

    @pl.when((s == 0) & (f == 0))
    def _():
        def fill_copy(j):
            row = pl.multiple_of(j * FFN_SUB, FFN_SUB)
            return pltpu.make_async_copy(acc.at[local_rows(0)], y_ref.at[pl.ds(row, FFN_SUB)], sem_out)

        n_sub_total = y_ref.shape[0] // FFN_SUB
        zero_acc(0)
        for_each_sub(used_ref[0], n_sub_total, lambda j: fill_copy(j).start())
        for_each_sub(used_ref[0], n_sub_total, lambda j: fill_copy(j).wait())

    @pl.when(nsub > 0)
    def _():
        @pl.when(f == 0)
        def _():
            for_each_sub(0, nsub, lambda j: in_copy(j).start())
            for_each_sub(0, nsub, zero_acc)
            for_each_sub(0, nsub, lambda j: in_copy(j).wait())

        def block(j0, n, cast_weights=False):
            if cast_weights:
                wg, wu, wd = (r[...].astype(BF16) for r in (wg_ref, wu_ref, wd_ref))
                wgb[...], wub[...], wdb[...] = wg, wu, wd
            else:
                wg, wu, wd = wgb[...], wub[...], wdb[...]
            rows = local_rows(j0, n)
            x = xb[rows, :]
            hg = jnp.dot(x, wg, preferred_element_type=F32)
            hu = jnp.dot(x, wu, preferred_element_type=F32)
            hid = (hg * jax.nn.sigmoid(hg) * hu).astype(BF16)
            acc[rows, :] += jnp.dot(hid, wd, preferred_element_type=F32)

            @pl.when(f == nf - 1)
            def _():
                for u in range(n):
                    out_copy(j0 + u).start()

        big = FFN_BLOCK_SUBS[0]
        n_big = nsub // big

        @pl.when(n_big > 0)
        def _():
            block(0, big, cast_weights=True)

        for_each_sub(1, n_big, lambda p: block(p * big, big))
        off = n_big * big
        started = n_big > 0
        for size in FFN_BLOCK_SUBS[1:]:
            has = ((nsub - off) // size) > 0

            @pl.when(has & jnp.logical_not(started))
            def _():
                block(0, size, cast_weights=True)

            @pl.when(has & started)
            def _():
                block(off, size)

            off = off + jnp.where(has, size, 0)
            started = started | has

        @pl.when(f == nf - 1)
        def _():
            for_each_sub(0, nsub, lambda j: out_copy(j).wait())


def _grouped_swiglu(tile_e, tile_start, tile_nsub, n_used_sub, xs, w_gate, w_up, w_down):
    P, D = xs.shape
    F = w_gate.shape[2]
    nf = F // FFN_TF
    n_tiles = tile_e.shape[0]

    def f_eff(s, f, tn):
        return jnp.where(tn[s] > 0, f, nf - 1)

    body = functools.partial(_expert_body, nf=nf)
    return pl.pallas_call(
        body,
        out_shape=jax.ShapeDtypeStruct((P, D), F32),
        grid_spec=pltpu.PrefetchScalarGridSpec(
            num_scalar_prefetch=4,
            grid=(n_tiles, nf),
            in_specs=[pl.BlockSpec(memory_space=pl.ANY),
                      pl.BlockSpec((None, D, FFN_TF), lambda s, f, te, ts, tn, used: (te[s], 0, f_eff(s, f, tn))),
                      pl.BlockSpec((None, D, FFN_TF), lambda s, f, te, ts, tn, used: (te[s], 0, f_eff(s, f, tn))),
                      pl.BlockSpec((None, FFN_TF, D), lambda s, f, te, ts, tn, used: (te[s], f_eff(s, f, tn), 0))],
            out_specs=pl.BlockSpec(memory_space=pl.ANY),
            scratch_shapes=[pltpu.VMEM((FFN_ROWS, D), BF16),
                            pltpu.VMEM((FFN_ROWS, D), F32),
                            pltpu.VMEM((D, FFN_TF), BF16),
                            pltpu.VMEM((D, FFN_TF), BF16),
                            pltpu.VMEM((FFN_TF, D), BF16),
                            pltpu.SemaphoreType.DMA,
                            pltpu.SemaphoreType.DMA]),
        compiler_params=_params(("arbitrary", "arbitrary")),
        name="grouped_swiglu",
    )(tile_e, tile_start, tile_nsub, n_used_sub, xs, w_gate, w_up, w_down)


def _combine_body(pos_ref, y_ref, x_ref, w_ref, g_ref, o_ref, ybuf, sem, *, tt):
    i = pl.program_id(0)

    def gather(step, slot):
        base = step * (tt * TOP_K)

        def issue(r, carry):
            for k in range(TOP_K):
                pltpu.make_async_copy(y_ref.at[pl.ds(pos_ref[base + r * TOP_K + k], 1)],
                                      ybuf.at[slot, k, pl.ds(r, 1)], sem.at[slot]).start()
            return carry

        lax.fori_loop(0, tt, issue, 0, unroll=4)

    @pl.when(i == 0)
    def _():
        gather(0, 0)

    @pl.when(i + 1 < pl.num_programs(0))
    def _():
        gather(i + 1, (i + 1) % 2)

    slot = i % 2
    for k in range(TOP_K):
        pltpu.make_async_copy(y_ref.at[pl.ds(0, tt)], ybuf.at[slot, k], sem.at[slot]).wait()
    w = w_ref[...]
    y = w[:, 0:1] * ybuf[slot, 0]
    for k in range(1, TOP_K):
        y = y + w[:, k:k + 1] * ybuf[slot, k]
    o_ref[...] = x_ref[...] + g_ref[...] * y


def _combine(pos, y, x, wts, mod, *, tt):
    N, D = x.shape
    tiles_per_batch = N // mod.shape[0] // tt
    body = functools.partial(_combine_body, tt=tt)
    return pl.pallas_call(
        body,
        out_shape=jax.ShapeDtypeStruct((N, D), F32),
        grid_spec=pltpu.PrefetchScalarGridSpec(
            num_scalar_prefetch=1,
            grid=(N // tt,),
            in_specs=[pl.BlockSpec(memory_space=pl.ANY),
                      pl.BlockSpec((tt, D), lambda i, p: (i, 0)),
                      pl.BlockSpec((tt, LANES), lambda i, p: (i, 0)),
                      pl.BlockSpec((None, None, 1, D), lambda i, p: (i // tiles_per_batch, 5, 0, 0))],
            out_specs=pl.BlockSpec((tt, D), lambda i, p: (i, 0)),
            scratch_shapes=[pltpu.VMEM((2, TOP_K, tt, D), F32), pltpu.SemaphoreType.DMA((2,))]),
        compiler_params=_params(("arbitrary",)),
        name="moe_combine",
    )(pos, y, x, wts, mod)


def _residual_normmod_body(x_ref, y_ref, gate_ref, g_ref, sc_ref, sh_ref, x_out_ref, h_ref):
    x_new = x_ref[...] + gate_ref[...] * y_ref[...]
    x_out_ref[...] = x_new
    h_ref[...] = _norm_mod(x_new, g_ref[...], sc_ref[...], sh_ref[...]).astype(h_ref.dtype)


def _residual_normmod(x, y, mod_prev, k_gate, g, layer, mod_next, k_scale, k_shift, *, tt, name):
    B, T, D = x.shape
    spec = pl.BlockSpec((None, tt, D), lambda b, t: (b, t, 0))
    return pl.pallas_call(
        _residual_normmod_body,
        out_shape=(jax.ShapeDtypeStruct((B, T, D), F32), jax.ShapeDtypeStruct((B, T, D), BF16)),
        grid=(B, T // tt),
        in_specs=[spec, spec, _mod_spec(k_gate, D),
                  pl.BlockSpec((None, 1, D), lambda b, t: (layer, 0, 0)),
                  _mod_spec(k_scale, D), _mod_spec(k_shift, D)],
        out_specs=(spec, spec),
        compiler_params=_params(("parallel", "parallel")),
        name=name,
    )(x, y, mod_prev, g, mod_next, mod_next)


def kernel(x, c, mod_w, mod_b, norm_mix_g, norm_ffn_g, na_w_in, na_q_norm_g, na_k_norm_g, na_rpb, na_w_out,
           ffn_w_gate, ffn_w_up, ffn_w_down, ml_w_in, ml_conv_w, ml_conv_b, ml_wq, ml_wk, ml_wv, ml_w_if,
           ml_b_if, ml_skip, ml_norm_g, ml_w_out, moe_router, moe_w_gate, moe_w_up, moe_w_down):
    B, T, D = x.shape
    N = B * T
    depth = mod_w.shape[0]
    c_pad = jnp.pad(c, ((0, 8 - B), (0, 0)))
    mix_g = norm_mix_g.reshape(depth, 1, D)
    ffn_g = norm_ffn_g.reshape(depth, 1, D)

    def modulation(i):
        mod = _matmul(c_pad, mod_w, i, tn=1024, tm=8, out_dtype=F32, a_silu=True, bias=mod_b,
                      name="adaln_modulation")
        return mod[:B].reshape(B, 6, 1, D)

    mod = modulation(0)
    h = _normmod(x, mix_g, 0, mod, 1, 0, tt=512, out_dtype=BF16, name="norm_mix0")
    na_dh = D // NA_HEADS
    qk_gain = jnp.concatenate([jnp.tile(na_q_norm_g[0] * na_dh ** -0.5, NA_HEADS), jnp.tile(na_k_norm_g[0], NA_HEADS),
                               jnp.ones((D,), F32)]).reshape(1, 3 * D)
    qkv = _matmul(h.reshape(N, D), na_w_in, 0, tn=1024, tm=MM_TM, out_dtype=BF16, head_gain=qk_gain,
                  norm_cols=2 * D, head_dim=na_dh, name="na_qkv")
    att = _neighborhood_attention(qkv.reshape(B, T, 3 * D), na_rpb[0])
    x, h = _matmul(att.reshape(N, D), na_w_out, 0, tn=D, tm=512, out_dtype=F32, res=x.reshape(N, D),
                   gate=mod, gate_idx=2, rows_per_batch=T, next_norm=(ffn_g, 0, 4, 3), name="na_out")
    x = x.reshape(B, T, D)
    dense_e, dense_start, dense_nsub = _dense_tiles(N)
    y = _grouped_swiglu(dense_e, dense_start, dense_nsub, jnp.full((1,), N // FFN_SUB, jnp.int32),
                        h.reshape(N, D), ffn_w_gate, ffn_w_up, ffn_w_down)

    mod0, mod = mod, modulation(1)
    x, h = _residual_normmod(x, y.reshape(B, T, D), mod0, 5, mix_g, 1, mod, 1, 0, tt=512,
                             name="ffn0_residual_norm_mix1")
    inner = ml_w_in.shape[2] // 2
    xz = _matmul(h.reshape(N, D), ml_w_in, 0, tn=1024, tm=MM_TM, out_dtype=BF16,
                 name="mlstm_in").reshape(B, T, 2 * inner)
    q, kt, v, xc, pre = _mlstm_pre(xz, ml_conv_w[0], ml_conv_b[0], ml_wq[0], ml_wk[0], ml_wv[0],
                                  ml_w_if[0], ml_b_if[0], tt=256)
    hf, hb = _mlstm(q, kt, v, pre, L=ML_CHUNK)
    u = _mlstm_post(hf, hb, xc, xz, ml_norm_g[0], ml_skip[0], tt=256)
    x = _matmul(u.reshape(N, inner), ml_w_out, 0, tn=512, tm=MM_TM, out_dtype=F32, res=x.reshape(N, D),
                gate=mod, gate_idx=2, rows_per_batch=T, name="mlstm_out").reshape(B, T, D)

    h2, top_idx, top_w = _router(x, ffn_g, 1, mod, moe_router[0], tt=256)
    n_slots = N * TOP_K + N_EXPERTS * FFN_SUB
    n_tiles = -(-N * TOP_K // FFN_ROWS) + N_EXPERTS
    pos, src, tile_e, tile_start, tile_nsub, n_used_sub = _route_tables(
        jnp.moveaxis(top_idx[:, :, :TOP_K, :], 2, 3).reshape(N, TOP_K), N_EXPERTS, n_tiles, n_slots)
    slab = D // 2 // LANES
    xs = _dispatch(src, h2.reshape(N * slab, LANES), n_slots, chunk=DISPATCH_CHUNK, slab=slab)
    ys = _grouped_swiglu(tile_e, tile_start, tile_nsub, n_used_sub, xs, moe_w_gate[0], moe_w_up[0], moe_w_down[0])
    return _combine(pos, ys, x.reshape(N, D), top_w.reshape(N, LANES), mod, tt=COMBINE_TOKENS).reshape(B, T, D)
```
